```python
import math
import jax
import jax.numpy as jnp
from jax import lax
import numpy as np

D_MODEL = 2048
BATCH = 8
SEQ = 2048
DEPTH = 2

HYENA_WIDTH = D_MODEL // 2
HYENA_ORDER = 2
HYENA_DIRS = 2
SHORT_CONV = 3
FILTER_EMB = 33
FILTER_BANDS = (FILTER_EMB - 1) // 2
FILTER_HIDDEN = 64
FILTER_SIN_W = 1.0
DECAY_TARGET = 1e-2
FAST_DECAY_PCT = 0.3
SLOW_DECAY_PCT = 1.5
MIN_DECAY = math.log(DECAY_TARGET) / FAST_DECAY_PCT
MAX_DECAY = math.log(DECAY_TARGET) / SLOW_DECAY_PCT

HEAD_DIM = 128
N_HEADS = (D_MODEL // 2) // HEAD_DIM
N_KV_HEADS = 2
GROUP = N_HEADS // N_KV_HEADS
ATTN_WIDTH = N_HEADS * HEAD_DIM
KV_WIDTH = N_KV_HEADS * HEAD_DIM
WINDOW = 128
BLOCK = 128
ROPE_THETA = 500000.0
ROPE_DIM = HEAD_DIM // 4
EPS = 1e-6

IN_SIZES = ((HYENA_ORDER + 1) * HYENA_WIDTH, HYENA_WIDTH, ATTN_WIDTH, KV_WIDTH, KV_WIDTH, ATTN_WIDTH, D_MODEL, D_MODEL)
IN_WIDTH = sum(IN_SIZES)

kernel_name = "hyena_swa_gated_hybrid_encoder"


def rms_norm(x, g):
    xf = x.astype(jnp.float32)
    y = xf * lax.rsqrt(jnp.mean(xf * xf, axis=-1, keepdims=True) + EPS)
    return (y * g.astype(jnp.float32)).astype(x.dtype)


def short_conv_centred(u, w, b):
    L = u.shape[1]
    p = SHORT_CONV // 2
    up = jnp.pad(u, ((0, 0), (p, SHORT_CONV - 1 - p), (0, 0)))
    out = b
    for j in range(SHORT_CONV):
        out = out + up[:, j:j + L] * w[j]
    return out


def hyena_kernels(L, w1, b1, w2, b2, w3, b3, w4, freq):
    f32 = jnp.float32
    t = jnp.linspace(0.0, 1.0, L, dtype=f32)[:, None]
    bands = jnp.linspace(1e-4, FILTER_BANDS - 1, FILTER_BANDS, dtype=f32)[None, :]
    ang = (2.0 * math.pi / L) * jnp.arange(L, dtype=f32)[:, None] * bands
    feats = jnp.concatenate([t, jnp.cos(ang), -jnp.sin(ang)], axis=-1)
    fr = freq.astype(f32)
    hdn = jnp.sin(fr * (feats @ w1.astype(f32) + b1.astype(f32)))
    hdn = jnp.sin(fr * (hdn @ w2.astype(f32) + b2.astype(f32)))
    hdn = jnp.sin(fr * (hdn @ w3.astype(f32) + b3.astype(f32)))
    filt = (hdn @ w4.astype(f32)).reshape(L, HYENA_ORDER, HYENA_DIRS, HYENA_WIDTH)
    deltas = jnp.abs(jnp.linspace(MIN_DECAY, MAX_DECAY, HYENA_WIDTH, dtype=f32))
    filt = filt * jnp.exp(-t * deltas)[:, None, None, :]
    fwd = filt[:, :, 0]
    bwd = filt[:, :, 1]
    zero = jnp.zeros((1, HYENA_ORDER, HYENA_WIDTH), f32)
    return jnp.concatenate([fwd, zero, bwd[1:][::-1]], axis=0)


def long_conv(u, kern2l, bias):
    L = u.shape[1]
    uf = jnp.fft.rfft(u.astype(jnp.float32), n=2 * L, axis=1)
    kf = jnp.fft.rfft(kern2l, n=2 * L, axis=0)
    y = jnp.fft.irfft(uf * kf[None], n=2 * L, axis=1)[:, :L]
    return (y + u.astype(jnp.float32) * bias.astype(jnp.float32)).astype(u.dtype)


def rope_partial(x, cos, sin):
    half = ROPE_DIM // 2
    xf = x.astype(jnp.float32)
    x1 = xf[..., :half]
    x2 = xf[..., half:ROPE_DIM]
    rot = jnp.concatenate([x1 * cos - x2 * sin, x2 * cos + x1 * sin], axis=-1)
    return jnp.concatenate([rot, xf[..., ROPE_DIM:]], axis=-1).astype(x.dtype)


def rope_tables(L):
    inv = ROPE_THETA ** (-jnp.arange(0, ROPE_DIM, 2, dtype=jnp.float32) / ROPE_DIM)
    ang = jnp.arange(L, dtype=jnp.float32)[:, None] * inv[None, :]
    return jnp.cos(ang)[:, None, :], jnp.sin(ang)[:, None, :]


def banded_sink_attention(q, k, v, sink):
    B, L = q.shape[0], q.shape[1]
    nb = L // BLOCK
    qb = q.reshape(B, nb, BLOCK, N_KV_HEADS, GROUP, HEAD_DIM)
    pad = ((0, 0), (BLOCK, BLOCK), (0, 0), (0, 0))
    kb = jnp.pad(k, pad).reshape(B, nb + 2, BLOCK, N_KV_HEADS, HEAD_DIM)
    vb = jnp.pad(v, pad).reshape(B, nb + 2, BLOCK, N_KV_HEADS, HEAD_DIM)
    kw = jnp.concatenate([kb[:, :-2], kb[:, 1:-1], kb[:, 2:]], axis=2)
    vw = jnp.concatenate([vb[:, :-2], vb[:, 1:-1], vb[:, 2:]], axis=2)
    s = jnp.einsum("bnqkgd,bnskd->bnkgqs", qb, kw, preferred_element_type=jnp.float32)
    s = s * (HEAD_DIM ** -0.5)
    blk = jnp.arange(nb)[:, None, None]
    qpos = blk * BLOCK + jnp.arange(BLOCK)[None, :, None]
    kpos = (blk - 1) * BLOCK + jnp.arange(3 * BLOCK)[None, None, :]
    valid = (jnp.abs(kpos - qpos) <= WINDOW) & (kpos >= 0) & (kpos < L)
    s = jnp.where(valid[None, :, None, None], s, -jnp.inf)
    sk = sink.astype(jnp.float32).reshape(1, 1, N_KV_HEADS, GROUP, 1, 1)
    m = jnp.maximum(jnp.max(s, axis=-1, keepdims=True), sk)
    p = jnp.exp(s - m)
    p = p / (jnp.sum(p, axis=-1, keepdims=True) + jnp.exp(sk - m))
    o = jnp.einsum("bnkgqs,bnskd->bnqkgd", p.astype(vw.dtype), vw)
    return o.reshape(B, L, N_HEADS * HEAD_DIM)


def hybrid_layer(x, norm_g, w_in, conv_w, conv_b, filt_w1, filt_b1, filt_w2, filt_b2,
                 filt_w3, filt_b3, filt_w4, filt_freq, hyena_bias, attn_sink,
                 w_hyena_out, w_attn_out, w_out):
    B, L, _ = x.shape
    h = rms_norm(x, norm_g)
    proj = h @ w_in
    points = np.cumsum(IN_SIZES)[:-1].tolist()
    u_hy, z_hy, q, k, v, z_at, g_hy, g_at = jnp.split(proj, points, axis=-1)

    u_hy = short_conv_centred(u_hy, conv_w, conv_b)
    hv, hx1, hx2 = jnp.split(u_hy, HYENA_ORDER + 1, axis=-1)
    kern = hyena_kernels(L, filt_w1, filt_b1, filt_w2, filt_b2, filt_w3, filt_b3, filt_w4, filt_freq)
    z = hx1 * long_conv(hv, kern[:, 0], hyena_bias[0])
    y_hy = hx2 * long_conv(z, kern[:, 1], hyena_bias[1])
    y_hy = y_hy * jax.nn.silu(z_hy)

    cos, sin = rope_tables(L)
    q = rope_partial(q.reshape(B, L, N_HEADS, HEAD_DIM), cos, sin)
    k = rope_partial(k.reshape(B, L, N_KV_HEADS, HEAD_DIM), cos, sin)
    v = v.reshape(B, L, N_KV_HEADS, HEAD_DIM)
    y_at = banded_sink_attention(q, k, v, attn_sink) * jax.nn.silu(z_at)

    merged = jax.nn.sigmoid(g_hy) * (y_hy @ w_hyena_out) + jax.nn.sigmoid(g_at) * (y_at @ w_attn_out)
    return x + merged @ w_out


def setup_inputs(seed: int = 0) -> dict:
    key = jax.random.key(seed)
    ks = jax.random.split(key, 20)
    f32 = jnp.float32

    def nrm(k, shape, scale):
        return jax.random.normal(k, shape, f32) * scale

    hw3 = (HYENA_ORDER + 1) * HYENA_WIDTH
    return {
        "x": nrm(ks[0], (BATCH, SEQ, D_MODEL), 1.0),
        "norm_g": 1.0 + nrm(ks[1], (DEPTH, D_MODEL), 0.02),
        "w_in": nrm(ks[2], (DEPTH, D_MODEL, IN_WIDTH), D_MODEL ** -0.5),
        "conv_w": nrm(ks[3], (DEPTH, SHORT_CONV, hw3), SHORT_CONV ** -0.5),
        "conv_b": nrm(ks[4], (DEPTH, hw3), 0.02),
        "filt_w1": nrm(ks[5], (DEPTH, FILTER_EMB, FILTER_HIDDEN), FILTER_EMB ** -0.5),
        "filt_b1": nrm(ks[6], (DEPTH, FILTER_HIDDEN), 0.1),
        "filt_w2": nrm(ks[7], (DEPTH, FILTER_HIDDEN, FILTER_HIDDEN), FILTER_HIDDEN ** -0.5),
        "filt_b2": nrm(ks[8], (DEPTH, FILTER_HIDDEN), 0.1),
        "filt_w3": nrm(ks[9], (DEPTH, FILTER_HIDDEN, FILTER_HIDDEN), FILTER_HIDDEN ** -0.5),
        "filt_b3": nrm(ks[10], (DEPTH, FILTER_HIDDEN), 0.1),
        "filt_w4": nrm(ks[11], (DEPTH, FILTER_HIDDEN, HYENA_ORDER * HYENA_DIRS * HYENA_WIDTH), 0.05 * FILTER_HIDDEN ** -0.5),
        "filt_freq": FILTER_SIN_W + nrm(ks[12], (DEPTH, FILTER_HIDDEN), 0.02),
        "hyena_bias": nrm(ks[13], (DEPTH, HYENA_ORDER, HYENA_WIDTH), 1.0),
        "attn_sink": nrm(ks[14], (DEPTH, N_HEADS), 0.5),
        "w_hyena_out": nrm(ks[15], (DEPTH, HYENA_WIDTH, D_MODEL), HYENA_WIDTH ** -0.5),
        "w_attn_out": nrm(ks[16], (DEPTH, ATTN_WIDTH, D_MODEL), ATTN_WIDTH ** -0.5),
        "w_out": nrm(ks[17], (DEPTH, D_MODEL, D_MODEL), D_MODEL ** -0.5),
        "final_norm": 1.0 + nrm(ks[18], (D_MODEL,), 0.02),
    }


def reference(x, norm_g, w_in, conv_w, conv_b, filt_w1, filt_b1, filt_w2, filt_b2,
              filt_w3, filt_b3, filt_w4, filt_freq, hyena_bias, attn_sink,
              w_hyena_out, w_attn_out, w_out, final_norm):
    for l in range(DEPTH):
        x = hybrid_layer(x, norm_g[l], w_in[l], conv_w[l], conv_b[l],
                         filt_w1[l], filt_b1[l], filt_w2[l], filt_b2[l],
                         filt_w3[l], filt_b3[l], filt_w4[l], filt_freq[l],
                         hyena_bias[l], attn_sink[l],
                         w_hyena_out[l], w_attn_out[l], w_out[l])
    return rms_norm(x, final_norm)
```

```python
import functools
import math

import jax
import jax.numpy as jnp
from jax import lax
from jax.experimental import pallas as pl
from jax.experimental.pallas import tpu as pltpu

F32 = jnp.float32
BF16 = jnp.bfloat16

HEAD_DIM = 128
N_HEADS = 8
N_KV_HEADS = 2
GROUP = N_HEADS // N_KV_HEADS
WINDOW = 128
QBLOCK = 128
KWIN = 3 * QBLOCK
ROPE_THETA = 500000.0
ROPE_DIM = HEAD_DIM // 4
ROPE_HALF = ROPE_DIM // 2
EPS = 1e-6
FILTER_BANDS = 16
FILTER_HIDDEN = 64
FEAT_PAD = 128
DECAY_TARGET = 1e-2
MIN_DECAY = math.log(DECAY_TARGET) / 0.3
MAX_DECAY = math.log(DECAY_TARGET) / 1.5

VMEM_LIMIT = 56 * 1024 * 1024


def _params(sem, vmem=VMEM_LIMIT):
    return pltpu.CompilerParams(dimension_semantics=sem, vmem_limit_bytes=vmem)


def _const_spec(shape):
    return pl.BlockSpec(shape, lambda *_: (0,) * len(shape), pipeline_mode=pl.Buffered(1))


def _norm_proj_kernel(x_ref, g_ref, w_ref, o_ref, h_ref):
    @pl.when(pl.program_id(1) == 0)
    def _():
        x = x_ref[...]
        ms = jnp.mean(x * x, axis=-1, keepdims=True)
        h_ref[...] = (x * lax.rsqrt(ms + EPS) * g_ref[...]).astype(BF16)

    o_ref[...] = jnp.dot(h_ref[...], w_ref[...], preferred_element_type=F32)


def _norm_proj(x2d, g, w_bf16, tm=512, tn=512):
    m, d = x2d.shape
    n = w_bf16.shape[1]
    return pl.pallas_call(
        _norm_proj_kernel,
        out_shape=jax.ShapeDtypeStruct((m, n), F32),
        grid=(m // tm, n // tn),
        in_specs=[
            pl.BlockSpec((tm, d), lambda i, j: (i, 0)),
            pl.BlockSpec((1, d), lambda i, j: (0, 0)),
            pl.BlockSpec((d, tn), lambda i, j: (0, j)),
        ],
        out_specs=pl.BlockSpec((tm, tn), lambda i, j: (i, j)),
        scratch_shapes=[pltpu.VMEM((tm, d), BF16)],
        compiler_params=_params(("parallel", "arbitrary")),
        name="norm_proj",
    )(x2d, g.reshape(1, d), w_bf16)


def _filter_kernel(feats_ref, t_ref, w1_ref, b1_ref, w2_ref, b2_ref, w3_ref, b3_ref, fr_ref,
                   w4f_ref, w4b_ref, dl_ref, c_ref, s_ref,
                   kr_ref, ki_ref, kn_ref, hdn_ref, ks_ref, kd_ref, *, row_chunk):
    L = feats_ref.shape[0]
    hp = lax.Precision.HIGHEST

    @pl.when((pl.program_id(0) == 0) & (pl.program_id(1) == 0))
    def _():
        fr = fr_ref[...]
        h = jnp.sin(fr * (jnp.dot(feats_ref[...], w1_ref[...], precision=hp,
                                  preferred_element_type=F32) + b1_ref[...]))
        h = jnp.sin(fr * (jnp.dot(h, w2_ref[...], precision=hp,
                                  preferred_element_type=F32) + b2_ref[...]))
        h = jnp.sin(fr * (jnp.dot(h, w3_ref[...], precision=hp,
                                  preferred_element_type=F32) + b3_ref[...]))
        hdn_ref[...] = h

    n_total = 2.0 * L
    nchunks = L // row_chunk
    alt = jnp.where((lax.broadcasted_iota(jnp.int32, (row_chunk, 1), 0) & 1) == 0, 1.0, -1.0)

    def taps(i, nyq):
        r0 = pl.multiple_of(i * row_chunk, row_chunk)
        h = hdn_ref[pl.ds(r0, row_chunk), :]
        decay = jnp.exp(-t_ref[pl.ds(r0, row_chunk), :] * dl_ref[...])
        fwd = jnp.dot(h, w4f_ref[...], precision=hp, preferred_element_type=F32) * decay
        bwd = jnp.dot(h, w4b_ref[...], precision=hp, preferred_element_type=F32) * decay
        row = lax.broadcasted_iota(jnp.int32, (row_chunk, 1), 0) + r0
        bwd = jnp.where(row == 0, 0.0, bwd)
        ksum = fwd + bwd
        ks_ref[pl.ds(r0, row_chunk), :] = ksum.astype(BF16)
        kd_ref[pl.ds(r0, row_chunk), :] = (bwd - fwd).astype(BF16)
        return nyq + jnp.sum(ksum * alt, axis=0, keepdims=True)

    nyq = lax.fori_loop(0, nchunks, taps, jnp.zeros((1, ks_ref.shape[1]), F32))
    kn_ref[...] = nyq * (1.0 / n_total)

    def spectrum(i, carry):
        r0 = pl.multiple_of(i * row_chunk, row_chunk)
        row = lax.broadcasted_iota(jnp.int32, (row_chunk, 1), 0) + r0
        wgt = jnp.where(row == 0, 1.0 / n_total, 2.0 / n_total)
        kr = jnp.dot(c_ref[pl.ds(r0, row_chunk), :], ks_ref[...], preferred_element_type=F32)
        ki = jnp.dot(s_ref[pl.ds(r0, row_chunk), :], kd_ref[...], preferred_element_type=F32)
        kr_ref[pl.ds(r0, row_chunk), :] = kr * wgt
        ki_ref[pl.ds(r0, row_chunk), :] = ki * wgt
        return carry

    lax.fori_loop(0, nchunks, spectrum, 0)


def _filters(feats, tcol, w1p, b1, w2, b2, w3, b3, freq, w4, deltas, cmat, smat, tc=256, row_chunk=512):
    L = feats.shape[0]
    width = deltas.shape[1]
    nct = width // tc
    hid = w2.shape[0]
    small = lambda a: pl.BlockSpec(a.shape, lambda o, c: (0,) * a.ndim)
    kern = functools.partial(_filter_kernel, row_chunk=row_chunk)
    return pl.pallas_call(
        kern,
        out_shape=(jax.ShapeDtypeStruct((2, L, width), F32),
                   jax.ShapeDtypeStruct((2, L, width), F32),
                   jax.ShapeDtypeStruct((2, 1, width), F32)),
        grid=(2, nct),
        in_specs=[small(feats), small(tcol), small(w1p), small(b1), small(w2), small(b2),
                  small(w3), small(b3), small(freq),
                  pl.BlockSpec((hid, tc), lambda o, c: (0, o * 2 * nct + c)),
                  pl.BlockSpec((hid, tc), lambda o, c: (0, o * 2 * nct + nct + c)),
                  pl.BlockSpec((1, tc), lambda o, c: (0, c)),
                  _const_spec((L, L)), _const_spec((L, L))],
        out_specs=(pl.BlockSpec((None, L, tc), lambda o, c: (o, 0, c)),
                   pl.BlockSpec((None, L, tc), lambda o, c: (o, 0, c)),
                   pl.BlockSpec((None, 1, tc), lambda o, c: (o, 0, c))),
        scratch_shapes=[pltpu.VMEM((L, hid), F32), pltpu.VMEM((L, tc), BF16), pltpu.VMEM((L, tc), BF16)],
        compiler_params=_params(("arbitrary", "arbitrary")),
        name="hyena_filters",
    )(feats, tcol, w1p, b1, w2, b2, w3, b3, freq, w4, w4, deltas, cmat, smat)


def _sconv_chunk(u_ref, w_ref, b_ref, i, rows, nchunks):
    L = u_ref.shape[0]
    r0 = pl.multiple_of(i * rows, rows)
    u = u_ref[pl.ds(r0, rows), :]
    up0 = pl.multiple_of(jnp.maximum(r0 - 8, 0), 8)
    dn0 = pl.multiple_of(jnp.minimum(r0 + rows, L - 8), 8)
    up = u_ref[pl.ds(up0, 8), :][7:8, :]
    dn = u_ref[pl.ds(dn0, 8), :][0:1, :]
    up = jnp.where(i > 0, up, 0.0)
    dn = jnp.where(i < nchunks - 1, dn, 0.0)
    row = lax.broadcasted_iota(jnp.int32, (rows, 1), 0)
    prev = jnp.where(row == 0, up, pltpu.roll(u, 1, 0))
    nxt = jnp.where(row == rows - 1, dn, pltpu.roll(u, rows - 1, 0))
    return b_ref[...] + prev * w_ref[0:1, :] + u * w_ref[1:2, :] + nxt * w_ref[2:3, :]


def _hyena_kernel(v_ref, x1_ref, x2_ref, zg_ref, wv_ref, wx1_ref, wx2_ref, bv_ref, bx1_ref, bx2_ref,
                  hb_ref, c_ref, s_ref, kr_ref, ki_ref, kn_ref, o_ref,
                  u_ref, ub_ref, a_ref, bn_ref, *, row_chunk):
    L, tc = u_ref.shape
    nchunks = L // row_chunk
    alt = jnp.where((lax.broadcasted_iota(jnp.int32, (row_chunk, 1), 0) & 1) == 0, 1.0, -1.0)
    zero_row = jnp.zeros((1, tc), F32)

    def rows(i):
        return pl.ds(pl.multiple_of(i * row_chunk, row_chunk), row_chunk)

    def forward(order):
        def body(i, carry):
            r = rows(i)
            ur = jnp.dot(c_ref[r, :], ub_ref[...], preferred_element_type=F32)
            us = jnp.dot(s_ref[r, :], ub_ref[...], preferred_element_type=F32)
            kr = kr_ref[order, r, :]
            ki = ki_ref[order, r, :]
            a_ref[r, :] = (ur * kr + us * ki).astype(BF16)
            bn_ref[r, :] = (us * kr - ur * ki).astype(BF16)
            return carry
        lax.fori_loop(0, nchunks, body, 0)

    def inverse_chunk(i, order, nyq):
        r = rows(i)
        y = jnp.dot(c_ref[r, :], a_ref[...], preferred_element_type=F32)
        y = y + jnp.dot(s_ref[r, :], bn_ref[...], preferred_element_type=F32)
        y = y + alt * (nyq * kn_ref[order])
        return y + u_ref[r, :] * hb_ref[order:order + 1, :]

    def p0(i, nyq):
        hv = _sconv_chunk(v_ref, wv_ref, bv_ref, i, row_chunk, nchunks)
        u_ref[rows(i), :] = hv
        ub_ref[rows(i), :] = hv.astype(BF16)
        return nyq + jnp.sum(hv * alt, axis=0, keepdims=True)
    nyq0 = lax.fori_loop(0, nchunks, p0, zero_row)

    forward(0)

    def p2(i, nyq):
        y = inverse_chunk(i, 0, nyq0)
        z = _sconv_chunk(x1_ref, wx1_ref, bx1_ref, i, row_chunk, nchunks) * y
        u_ref[rows(i), :] = z
        ub_ref[rows(i), :] = z.astype(BF16)
        return nyq + jnp.sum(z * alt, axis=0, keepdims=True)
    nyq1 = lax.fori_loop(0, nchunks, p2, zero_row)

    forward(1)

    def p4(i, carry):
        y = inverse_chunk(i, 1, nyq1)
        y = _sconv_chunk(x2_ref, wx2_ref, bx2_ref, i, row_chunk, nchunks) * y
        zg = zg_ref[rows(i), :]
        o_ref[rows(i), :] = (y * (zg * jax.nn.sigmoid(zg))).astype(o_ref.dtype)
        return carry
    lax.fori_loop(0, nchunks, p4, 0)


def _hyena(proj3, conv_w, conv_b, hyena_bias, cmat, smat, kr, ki, kn, tc=256, row_chunk=512):
    B, L, _ = proj3.shape
    width = hyena_bias.shape[1]
    nct = width // tc
    conv_b2 = conv_b.reshape(1, -1)
    sig = lambda part: pl.BlockSpec((None, L, tc), lambda j, b: (b, 0, part * nct + j))
    cw = lambda part: pl.BlockSpec((3, tc), lambda j, b: (0, part * nct + j))
    cb = lambda part: pl.BlockSpec((1, tc), lambda j, b: (0, part * nct + j))
    kspec = pl.BlockSpec((2, L, tc), lambda j, b: (0, 0, j), pipeline_mode=pl.Buffered(1))
    kern = functools.partial(_hyena_kernel, row_chunk=row_chunk)
    return pl.pallas_call(
        kern,
        out_shape=jax.ShapeDtypeStruct((B, L, width), BF16),
        grid=(nct, B),
        in_specs=[sig(0), sig(1), sig(2), sig(3),
                  cw(0), cw(1), cw(2), cb(0), cb(1), cb(2),
                  pl.BlockSpec((2, tc), lambda j, b: (0, j)),
                  _const_spec((L, L)), _const_spec((L, L)),
                  kspec, kspec,
                  pl.BlockSpec((2, 1, tc), lambda j, b: (0, 0, j))],
        out_specs=pl.BlockSpec((None, L, tc), lambda j, b: (b, 0, j)),
        scratch_shapes=[pltpu.VMEM((L, tc), F32), pltpu.VMEM((L, tc), BF16),
                        pltpu.VMEM((L, tc), BF16), pltpu.VMEM((L, tc), BF16)],
        compiler_params=_params(("parallel", "arbitrary")),
        name="hyena_mixer",
    )(proj3, proj3, proj3, proj3, conv_w, conv_w, conv_w, conv_b2, conv_b2, conv_b2,
      hyena_bias, cmat, smat, kr, ki, kn)


def _rope(x, cos, sin_lo, sin_hi):
    return (x * cos + pltpu.roll(x, HEAD_DIM - ROPE_HALF, 1) * sin_lo
            + pltpu.roll(x, ROPE_HALF, 1) * sin_hi)


def _attn_kernel(sink_ref, q_ref, k_ref, v_ref, zg_ref, cq_ref, slq_ref, shq_ref,
                 ck_ref, slk_ref, shk_ref, o_ref):
    L = k_ref.shape[0]
    kvh = pl.program_id(1)
    n = pl.program_id(2)
    start = pl.multiple_of(jnp.clip((n - 1) * QBLOCK, 0, L - KWIN), QBLOCK)
    win = pl.ds(start, KWIN)
    kb = _rope(k_ref[win, :], ck_ref[win, :], slk_ref[win, :], shk_ref[win, :]).astype(BF16)
    vb = v_ref[win, :].astype(BF16)
    qpos = n * QBLOCK + lax.broadcasted_iota(jnp.int32, (QBLOCK, 1), 0)
    kpos = start + lax.broadcasted_iota(jnp.int32, (1, KWIN), 1)
    valid = jnp.abs(kpos - qpos) <= WINDOW
    cq, slq, shq = cq_ref[...], slq_ref[...], shq_ref[...]
    scale = HEAD_DIM ** -0.5
    for g in range(GROUP):
        cols = slice(g * HEAD_DIM, (g + 1) * HEAD_DIM)
        qb = _rope(q_ref[:, cols], cq, slq, shq).astype(BF16)
        s = lax.dot_general(qb, kb, (((1,), (1,)), ((), ())), preferred_element_type=F32) * scale
        s = jnp.where(valid, s, -jnp.inf)
        sk = sink_ref[kvh * GROUP + g]
        m = jnp.maximum(jnp.max(s, axis=-1, keepdims=True), sk)
        p = jnp.exp(s - m)
        p = p / (jnp.sum(p, axis=-1, keepdims=True) + jnp.exp(sk - m))
        o = jnp.dot(p.astype(BF16), vb, preferred_element_type=F32)
        zg = zg_ref[:, cols]
        o_ref[:, cols] = (o * (zg * jax.nn.sigmoid(zg))).astype(o_ref.dtype)


def _attention(proj3, sink, rope_tabs, col_q, col_k, col_v, col_zg):
    B, L, _ = proj3.shape
    gw = GROUP * HEAD_DIM
    cos_t, sin_lo, sin_hi = rope_tabs
    qtab = pl.BlockSpec((QBLOCK, HEAD_DIM), lambda b, h, n: (n, 0))
    ktab = pl.BlockSpec((L, HEAD_DIM), lambda b, h, n: (0, 0))
    return pl.pallas_call(
        _attn_kernel,
        out_shape=jax.ShapeDtypeStruct((B, L, N_HEADS * HEAD_DIM), BF16),
        grid=(B, N_KV_HEADS, L // QBLOCK),
        in_specs=[pl.BlockSpec(memory_space=pltpu.SMEM),
                  pl.BlockSpec((None, QBLOCK, gw), lambda b, h, n: (b, n, col_q // gw + h)),
                  pl.BlockSpec((None, L, HEAD_DIM), lambda b, h, n: (b, 0, col_k // HEAD_DIM + h)),
                  pl.BlockSpec((None, L, HEAD_DIM), lambda b, h, n: (b, 0, col_v // HEAD_DIM + h)),
                  pl.BlockSpec((None, QBLOCK, gw), lambda b, h, n: (b, n, col_zg // gw + h)),
                  qtab, qtab, qtab, ktab, ktab, ktab],
        out_specs=pl.BlockSpec((None, QBLOCK, gw), lambda b, h, n: (b, n, h)),
        compiler_params=_params(("parallel", "parallel", "arbitrary")),
        name="window_attention",
    )(sink, proj3, proj3, proj3, proj3, cos_t, sin_lo, sin_hi, cos_t, sin_lo, sin_hi)


def _merge_out_kernel(*refs, n_gate_blocks, final):
    x_ref, yh_ref, ya_ref = refs[0:3]
    gh_refs = refs[3:3 + n_gate_blocks]
    ga_refs = refs[3 + n_gate_blocks:3 + 2 * n_gate_blocks]
    who_ref, wao_ref, wout_ref, fg_ref, o_ref, m_ref = refs[3 + 2 * n_gate_blocks:]
    gw = gh_refs[0].shape[1]
    yh = yh_ref[...]
    ya = ya_ref[...]
    for c in range(n_gate_blocks):
        cols = slice(c * gw, (c + 1) * gw)
        ph = jnp.dot(yh, who_ref[:, cols], preferred_element_type=F32)
        pa = jnp.dot(ya, wao_ref[:, cols], preferred_element_type=F32)
        merged = jax.nn.sigmoid(gh_refs[c][...]) * ph + jax.nn.sigmoid(ga_refs[c][...]) * pa
        m_ref[:, cols] = merged.astype(BF16)
    out = x_ref[...] + jnp.dot(m_ref[...], wout_ref[...], preferred_element_type=F32)
    if final:
        ms = jnp.mean(out * out, axis=-1, keepdims=True)
        out = out * lax.rsqrt(ms + EPS) * fg_ref[...]
    o_ref[...] = out


def _merge_out(x2d, proj, yh, ya, who, wao, wout, final_g, col_gh, col_ga, final, tm=256, gw=512):
    m, d = x2d.shape
    width = yh.shape[1]
    ngb = d // gw
    gspec = lambda col0, c: pl.BlockSpec((tm, gw), lambda i: (i, col0 // gw + c))
    kern = functools.partial(_merge_out_kernel, n_gate_blocks=ngb, final=final)
    return pl.pallas_call(
        kern,
        out_shape=jax.ShapeDtypeStruct((m, d), F32),
        grid=(m // tm,),
        in_specs=[pl.BlockSpec((tm, d), lambda i: (i, 0)),
                  pl.BlockSpec((tm, width), lambda i: (i, 0)),
                  pl.BlockSpec((tm, width), lambda i: (i, 0))]
                 + [gspec(col_gh, c) for c in range(ngb)]
                 + [gspec(col_ga, c) for c in range(ngb)]
                 + [_const_spec((width, d)), _const_spec((width, d)), _const_spec((d, d)),
                    pl.BlockSpec((1, d), lambda i: (0, 0))],
        out_specs=pl.BlockSpec((tm, d), lambda i: (i, 0)),
        scratch_shapes=[pltpu.VMEM((tm, d), BF16)],
        compiler_params=_params(("parallel",)),
        name="merge_out",
    )(x2d, yh, ya, *([proj] * (2 * ngb)), who, wao, wout, final_g.reshape(1, d))


def _dft_mats(L):
    idx = jnp.arange(L, dtype=jnp.int32)
    k = (idx[:, None] * idx[None, :]) % (2 * L)
    ang = k.astype(F32) * (math.pi / L)
    return jnp.cos(ang).astype(BF16), jnp.sin(ang).astype(BF16)


def _rope_tabs(L):
    inv = ROPE_THETA ** (-jnp.arange(0, ROPE_DIM, 2, dtype=F32) / ROPE_DIM)
    ang = jnp.arange(L, dtype=F32)[:, None] * inv[None, :]
    cos, sin = jnp.cos(ang), jnp.sin(ang)
    ones = jnp.ones((L, HEAD_DIM - ROPE_DIM), F32)
    zeros = jnp.zeros((L, HEAD_DIM - ROPE_HALF), F32)
    cos_t = jnp.concatenate([cos, cos, ones], axis=1)
    sin_lo = jnp.concatenate([-sin, zeros], axis=1)
    sin_hi = jnp.concatenate([jnp.zeros((L, ROPE_HALF), F32), sin, zeros[:, ROPE_HALF:]], axis=1)
    return cos_t, sin_lo, sin_hi


def _filter_feats(L):
    t = jnp.linspace(0.0, 1.0, L, dtype=F32)[:, None]
    bands = jnp.linspace(1e-4, FILTER_BANDS - 1, FILTER_BANDS, dtype=F32)[None, :]
    ang = (2.0 * math.pi / L) * jnp.arange(L, dtype=F32)[:, None] * bands
    feats = jnp.concatenate([t, jnp.cos(ang), -jnp.sin(ang)], axis=-1)
    feats = jnp.pad(feats, ((0, 0), (0, FEAT_PAD - feats.shape[1])))
    return feats, t


def kernel(x, norm_g, w_in, conv_w, conv_b, filt_w1, filt_b1, filt_w2, filt_b2, filt_w3, filt_b3,
           filt_w4, filt_freq, hyena_bias, attn_sink, w_hyena_out, w_attn_out, w_out, final_norm):
    B, L, D = x.shape
    depth = norm_g.shape[0]
    hw = hyena_bias.shape[2]
    aw = N_HEADS * HEAD_DIM
    kvw = N_KV_HEADS * HEAD_DIM
    sizes = (3 * hw, hw, aw, kvw, kvw, aw, D, D)
    cols = [0]
    for s in sizes:
        cols.append(cols[-1] + s)
    col_zhy, col_q, col_k, col_v, col_zat, col_gh, col_ga = cols[1:8]

    cmat, smat = _dft_mats(L)
    rope_tabs = _rope_tabs(L)
    feats, tcol = _filter_feats(L)
    deltas = jnp.abs(jnp.linspace(MIN_DECAY, MAX_DECAY, hw, dtype=F32))[None, :]

    xf = x.reshape(B * L, D)
    for l in range(depth):
        proj = _norm_proj(xf, norm_g[l], w_in[l].astype(BF16))
        proj3 = proj.reshape(B, L, -1)
        w1p = jnp.pad(filt_w1[l], ((0, FEAT_PAD - filt_w1.shape[1]), (0, 0)))
        kr, ki, kn = _filters(feats, tcol, w1p, filt_b1[l][None, :], filt_w2[l], filt_b2[l][None, :],
                              filt_w3[l], filt_b3[l][None, :], filt_freq[l][None, :], filt_w4[l],
                              deltas, cmat, smat)
        y_hy = _hyena(proj3, conv_w[l], conv_b[l], hyena_bias[l], cmat, smat, kr, ki, kn)
        y_at = _attention(proj3, attn_sink[l], rope_tabs, col_q, col_k, col_v, col_zat)
        xf = _merge_out(xf, proj, y_hy.reshape(B * L, hw), y_at.reshape(B * L, aw),
                        w_hyena_out[l].astype(BF16), w_attn_out[l].astype(BF16), w_out[l].astype(BF16),
                        final_norm, col_gh, col_ga, final=(l == depth - 1))
    return xf.reshape(B, L, D)
```

```python
import functools
import math

import jax
import jax.numpy as jnp
from jax import lax
from jax.experimental import pallas as pl
from jax.experimental.pallas import tpu as pltpu

F32 = jnp.float32
BF16 = jnp.bfloat16

HEAD_DIM = 128
N_HEADS = 8
N_KV_HEADS = 2
GROUP = N_HEADS // N_KV_HEADS
WINDOW = 128
ROPE_THETA = 500000.0
ROPE_DIM = HEAD_DIM // 4
ROPE_HALF = ROPE_DIM // 2
EPS = 1e-6
FILTER_BANDS = 16
FILTER_HIDDEN = 64
FEAT_PAD = 128
DECAY_TARGET = 1e-2
MIN_DECAY = math.log(DECAY_TARGET) / 0.3
MAX_DECAY = math.log(DECAY_TARGET) / 1.5

VMEM_LIMIT = 56 * 1024 * 1024


def _params(sem, vmem=VMEM_LIMIT):
    return pltpu.CompilerParams(dimension_semantics=sem, vmem_limit_bytes=vmem)


def _const_spec(shape):
    return pl.BlockSpec(shape, lambda *_: (0,) * len(shape), pipeline_mode=pl.Buffered(1))


def _norm_proj_kernel(x_ref, g_ref, w_ref, o_ref, h_ref):
    @pl.when(pl.program_id(1) == 0)
    def _():
        x = x_ref[...]
        ms = jnp.mean(x * x, axis=-1, keepdims=True)
        h_ref[...] = (x * lax.rsqrt(ms + EPS) * g_ref[...]).astype(BF16)

    o_ref[...] = jnp.dot(h_ref[...], w_ref[...], preferred_element_type=F32)


def _norm_proj(x2d, g, w_bf16, tm=1024, tn=768):
    m, d = x2d.shape
    n = w_bf16.shape[1]
    return pl.pallas_call(
        _norm_proj_kernel,
        out_shape=jax.ShapeDtypeStruct((m, n), F32),
        grid=(m // tm, n // tn),
        in_specs=[
            pl.BlockSpec((tm, d), lambda i, j: (i, 0)),
            pl.BlockSpec((1, d), lambda i, j: (0, 0)),
            pl.BlockSpec((d, tn), lambda i, j: (0, j)),
        ],
        out_specs=pl.BlockSpec((tm, tn), lambda i, j: (i, j)),
        scratch_shapes=[pltpu.VMEM((tm, d), BF16)],
        compiler_params=_params(("parallel", "arbitrary")),
        name="norm_proj",
    )(x2d, g.reshape(1, d), w_bf16)


def _filter_kernel(feats_ref, t_ref, w1_ref, b1_ref, w2_ref, b2_ref, w3_ref, b3_ref, fr_ref,
                   w4f_ref, w4b_ref, dl_ref, c_ref, s_ref,
                   kr_ref, ki_ref, kn_ref, hdn_ref, ks_ref, kd_ref, *, row_chunk):
    L = feats_ref.shape[0]
    hp = lax.Precision.HIGHEST

    @pl.when((pl.program_id(0) == 0) & (pl.program_id(1) == 0))
    def _():
        fr = fr_ref[...]
        h = jnp.sin(fr * (jnp.dot(feats_ref[...], w1_ref[...], precision=hp,
                                  preferred_element_type=F32) + b1_ref[...]))
        h = jnp.sin(fr * (jnp.dot(h, w2_ref[...], precision=hp,
                                  preferred_element_type=F32) + b2_ref[...]))
        h = jnp.sin(fr * (jnp.dot(h, w3_ref[...], precision=hp,
                                  preferred_element_type=F32) + b3_ref[...]))
        hdn_ref[...] = h

    n_total = 2.0 * L
    nchunks = L // row_chunk
    alt = jnp.where((lax.broadcasted_iota(jnp.int32, (row_chunk, 1), 0) & 1) == 0, 1.0, -1.0)

    def taps(i, nyq):
        r0 = pl.multiple_of(i * row_chunk, row_chunk)
        h = hdn_ref[pl.ds(r0, row_chunk), :]
        decay = jnp.exp(-t_ref[pl.ds(r0, row_chunk), :] * dl_ref[...])
        fwd = jnp.dot(h, w4f_ref[...], precision=hp, preferred_element_type=F32) * decay
        bwd = jnp.dot(h, w4b_ref[...], precision=hp, preferred_element_type=F32) * decay
        row = lax.broadcasted_iota(jnp.int32, (row_chunk, 1), 0) + r0
        bwd = jnp.where(row == 0, 0.0, bwd)
        ksum = fwd + bwd
        ks_ref[pl.ds(r0, row_chunk), :] = ksum.astype(BF16)
        kd_ref[pl.ds(r0, row_chunk), :] = (bwd - fwd).astype(BF16)
        return nyq + jnp.sum(ksum * alt, axis=0, keepdims=True)

    nyq = lax.fori_loop(0, nchunks, taps, jnp.zeros((1, ks_ref.shape[1]), F32))
    kn_ref[...] = nyq * (1.0 / n_total)

    def spectrum(i, carry):
        r0 = pl.multiple_of(i * row_chunk, row_chunk)
        row = lax.broadcasted_iota(jnp.int32, (row_chunk, 1), 0) + r0
        wgt = jnp.where(row == 0, 1.0 / n_total, 2.0 / n_total)
        kr = jnp.dot(c_ref[pl.ds(r0, row_chunk), :], ks_ref[...], preferred_element_type=F32)
        ki = jnp.dot(s_ref[pl.ds(r0, row_chunk), :], kd_ref[...], preferred_element_type=F32)
        kr_ref[pl.ds(r0, row_chunk), :] = kr * wgt
        ki_ref[pl.ds(r0, row_chunk), :] = ki * wgt
        return carry

    lax.fori_loop(0, nchunks, spectrum, 0)


def _filters(feats, tcol, w1p, b1, w2, b2, w3, b3, freq, w4, deltas, cmat, smat, tc=256, row_chunk=512):
    L = feats.shape[0]
    width = deltas.shape[1]
    nct = width // tc
    hid = w2.shape[0]
    small = lambda a: pl.BlockSpec(a.shape, lambda o, c: (0,) * a.ndim)
    kern = functools.partial(_filter_kernel, row_chunk=row_chunk)
    return pl.pallas_call(
        kern,
        out_shape=(jax.ShapeDtypeStruct((2, L, width), F32),
                   jax.ShapeDtypeStruct((2, L, width), F32),
                   jax.ShapeDtypeStruct((2, 1, width), F32)),
        grid=(2, nct),
        in_specs=[small(feats), small(tcol), small(w1p), small(b1), small(w2), small(b2),
                  small(w3), small(b3), small(freq),
                  pl.BlockSpec((hid, tc), lambda o, c: (0, o * 2 * nct + c)),
                  pl.BlockSpec((hid, tc), lambda o, c: (0, o * 2 * nct + nct + c)),
                  pl.BlockSpec((1, tc), lambda o, c: (0, c)),
                  _const_spec((L, L)), _const_spec((L, L))],
        out_specs=(pl.BlockSpec((None, L, tc), lambda o, c: (o, 0, c)),
                   pl.BlockSpec((None, L, tc), lambda o, c: (o, 0, c)),
                   pl.BlockSpec((None, 1, tc), lambda o, c: (o, 0, c))),
        scratch_shapes=[pltpu.VMEM((L, hid), F32), pltpu.VMEM((L, tc), BF16), pltpu.VMEM((L, tc), BF16)],
        compiler_params=_params(("arbitrary", "arbitrary")),
        name="hyena_filters",
    )(feats, tcol, w1p, b1, w2, b2, w3, b3, freq, w4, w4, deltas, cmat, smat)


def _sconv_chunk(u_ref, w_ref, b_ref, i, rows, nchunks):
    L = u_ref.shape[0]
    r0 = pl.multiple_of(i * rows, rows)
    u = u_ref[pl.ds(r0, rows), :]
    up0 = pl.multiple_of(jnp.maximum(r0 - 8, 0), 8)
    dn0 = pl.multiple_of(jnp.minimum(r0 + rows, L - 8), 8)
    up = u_ref[pl.ds(up0, 8), :][7:8, :]
    dn = u_ref[pl.ds(dn0, 8), :][0:1, :]
    up = jnp.where(i > 0, up, 0.0)
    dn = jnp.where(i < nchunks - 1, dn, 0.0)
    row = lax.broadcasted_iota(jnp.int32, (rows, 1), 0)
    prev = jnp.where(row == 0, up, pltpu.roll(u, 1, 0))
    nxt = jnp.where(row == rows - 1, dn, pltpu.roll(u, rows - 1, 0))
    return b_ref[...] + prev * w_ref[0:1, :] + u * w_ref[1:2, :] + nxt * w_ref[2:3, :]


def _hyena_kernel(v_ref, x1_ref, x2_ref, zg_ref, wv_ref, wx1_ref, wx2_ref, bv_ref, bx1_ref, bx2_ref,
                  hb_ref, c_ref, s_ref, kr_ref, ki_ref, kn_ref, o_ref,
                  u_ref, ub_ref, a_ref, bn_ref, *, row_chunk):
    L, tc = u_ref.shape
    nchunks = L // row_chunk
    alt = jnp.where((lax.broadcasted_iota(jnp.int32, (row_chunk, 1), 0) & 1) == 0, 1.0, -1.0)
    zero_row = jnp.zeros((1, tc), F32)

    def rows(i):
        return pl.ds(pl.multiple_of(i * row_chunk, row_chunk), row_chunk)

    def forward(order):
        def body(i, carry):
            r = rows(i)
            ur = jnp.dot(c_ref[r, :], ub_ref[...], preferred_element_type=F32)
            us = jnp.dot(s_ref[r, :], ub_ref[...], preferred_element_type=F32)
            kr = kr_ref[order, r, :]
            ki = ki_ref[order, r, :]
            a_ref[r, :] = (ur * kr + us * ki).astype(BF16)
            bn_ref[r, :] = (us * kr - ur * ki).astype(BF16)
            return carry
        lax.fori_loop(0, nchunks, body, 0)

    def inverse_chunk(i, order, nyq):
        r = rows(i)
        y = jnp.dot(c_ref[r, :], a_ref[...], preferred_element_type=F32)
        y = y + jnp.dot(s_ref[r, :], bn_ref[...], preferred_element_type=F32)
        y = y + alt * (nyq * kn_ref[order])
        return y + u_ref[r, :] * hb_ref[order:order + 1, :]

    def p0(i, nyq):
        hv = _sconv_chunk(v_ref, wv_ref, bv_ref, i, row_chunk, nchunks)
        u_ref[rows(i), :] = hv
        ub_ref[rows(i), :] = hv.astype(BF16)
        return nyq + jnp.sum(hv * alt, axis=0, keepdims=True)
    nyq0 = lax.fori_loop(0, nchunks, p0, zero_row)

    forward(0)

    def p2(i, nyq):
        y = inverse_chunk(i, 0, nyq0)
        z = _sconv_chunk(x1_ref, wx1_ref, bx1_ref, i, row_chunk, nchunks) * y
        u_ref[rows(i), :] = z
        ub_ref[rows(i), :] = z.astype(BF16)
        return nyq + jnp.sum(z * alt, axis=0, keepdims=True)
    nyq1 = lax.fori_loop(0, nchunks, p2, zero_row)

    forward(1)

    def p4(i, carry):
        y = inverse_chunk(i, 1, nyq1)
        y = _sconv_chunk(x2_ref, wx2_ref, bx2_ref, i, row_chunk, nchunks) * y
        zg = zg_ref[rows(i), :]
        o_ref[rows(i), :] = (y * (zg * jax.nn.sigmoid(zg))).astype(o_ref.dtype)
        return carry
    lax.fori_loop(0, nchunks, p4, 0)


def _hyena(proj3, conv_w, conv_b, hyena_bias, cmat, smat, kr, ki, kn, tc=256, row_chunk=512):
    B, L, _ = proj3.shape
    width = hyena_bias.shape[1]
    nct = width // tc
    conv_b2 = conv_b.reshape(1, -1)
    sig = lambda part: pl.BlockSpec((None, L, tc), lambda j, b: (b, 0, part * nct + j))
    cw = lambda part: pl.BlockSpec((3, tc), lambda j, b: (0, part * nct + j))
    cb = lambda part: pl.BlockSpec((1, tc), lambda j, b: (0, part * nct + j))
    kspec = pl.BlockSpec((2, L, tc), lambda j, b: (0, 0, j), pipeline_mode=pl.Buffered(1))
    kern = functools.partial(_hyena_kernel, row_chunk=row_chunk)
    return pl.pallas_call(
        kern,
        out_shape=jax.ShapeDtypeStruct((B, L, width), BF16),
        grid=(nct, B),
        in_specs=[sig(0), sig(1), sig(2), sig(3),
                  cw(0), cw(1), cw(2), cb(0), cb(1), cb(2),
                  pl.BlockSpec((2, tc), lambda j, b: (0, j)),
                  _const_spec((L, L)), _const_spec((L, L)),
                  kspec, kspec,
                  pl.BlockSpec((2, 1, tc), lambda j, b: (0, 0, j))],
        out_specs=pl.BlockSpec((None, L, tc), lambda j, b: (b, 0, j)),
        scratch_shapes=[pltpu.VMEM((L, tc), F32), pltpu.VMEM((L, tc), BF16),
                        pltpu.VMEM((L, tc), BF16), pltpu.VMEM((L, tc), BF16)],
        compiler_params=_params(("parallel", "arbitrary")),
        name="hyena_mixer",
    )(proj3, proj3, proj3, proj3, conv_w, conv_w, conv_w, conv_b2, conv_b2, conv_b2,
      hyena_bias, cmat, smat, kr, ki, kn)


def _rope(x, cos, sin_lo, sin_hi):
    return (x * cos + pltpu.roll(x, HEAD_DIM - ROPE_HALF, 1) * sin_lo
            + pltpu.roll(x, ROPE_HALF, 1) * sin_hi)


def _attn_kernel(sink_ref, q_ref, k_ref, v_ref, zg_ref, cq_ref, slq_ref, shq_ref,
                 ck_ref, slk_ref, shk_ref, o_ref):
    L = k_ref.shape[0]
    qb = q_ref.shape[0]
    kw = qb + 2 * WINDOW
    kvh = pl.program_id(1)
    n = pl.program_id(2)
    start = pl.multiple_of(jnp.clip(n * qb - WINDOW, 0, L - kw), WINDOW)
    win = pl.ds(start, kw)
    kb = _rope(k_ref[win, :], ck_ref[win, :], slk_ref[win, :], shk_ref[win, :]).astype(BF16)
    vb = v_ref[win, :].astype(BF16)
    qpos = n * qb + lax.broadcasted_iota(jnp.int32, (qb, 1), 0)
    kpos = start + lax.broadcasted_iota(jnp.int32, (1, kw), 1)
    valid = jnp.abs(kpos - qpos) <= WINDOW
    cq, slq, shq = cq_ref[...], slq_ref[...], shq_ref[...]
    q = jnp.concatenate(
        [_rope(q_ref[:, g * HEAD_DIM:(g + 1) * HEAD_DIM], cq, slq, shq).astype(BF16) for g in range(GROUP)],
        axis=0)
    s = lax.dot_general(q, kb, (((1,), (1,)), ((), ())), preferred_element_type=F32) * (HEAD_DIM ** -0.5)
    s = jnp.where(valid[None], s.reshape(GROUP, qb, kw), -jnp.inf)
    head = lax.broadcasted_iota(jnp.int32, (GROUP, 1, 1), 0)
    sk = jnp.zeros((GROUP, 1, 1), F32)
    for g in range(GROUP):
        sk = jnp.where(head == g, sink_ref[kvh * GROUP + g], sk)
    m = jnp.maximum(jnp.max(s, axis=-1, keepdims=True), sk)
    p = jnp.exp(s - m)
    p = p / (jnp.sum(p, axis=-1, keepdims=True) + jnp.exp(sk - m))
    o = jnp.dot(p.reshape(GROUP * qb, kw).astype(BF16), vb, preferred_element_type=F32)
    for g in range(GROUP):
        cols = slice(g * HEAD_DIM, (g + 1) * HEAD_DIM)
        zg = zg_ref[:, cols]
        o_ref[:, cols] = (o[g * qb:(g + 1) * qb] * (zg * jax.nn.sigmoid(zg))).astype(o_ref.dtype)


def _attention(proj3, sink, rope_tabs, col_q, col_k, col_v, col_zg, qb=256):
    B, L, _ = proj3.shape
    gw = GROUP * HEAD_DIM
    cos_t, sin_lo, sin_hi = rope_tabs
    qtab = pl.BlockSpec((qb, HEAD_DIM), lambda b, h, n: (n, 0))
    ktab = pl.BlockSpec((L, HEAD_DIM), lambda b, h, n: (0, 0))
    return pl.pallas_call(
        _attn_kernel,
        out_shape=jax.ShapeDtypeStruct((B, L, N_HEADS * HEAD_DIM), BF16),
        grid=(B, N_KV_HEADS, L // qb),
        in_specs=[pl.BlockSpec(memory_space=pltpu.SMEM),
                  pl.BlockSpec((None, qb, gw), lambda b, h, n: (b, n, col_q // gw + h)),
                  pl.BlockSpec((None, L, HEAD_DIM), lambda b, h, n: (b, 0, col_k // HEAD_DIM + h)),
                  pl.BlockSpec((None, L, HEAD_DIM), lambda b, h, n: (b, 0, col_v // HEAD_DIM + h)),
                  pl.BlockSpec((None, qb, gw), lambda b, h, n: (b, n, col_zg // gw + h)),
                  qtab, qtab, qtab, ktab, ktab, ktab],
        out_specs=pl.BlockSpec((None, qb, gw), lambda b, h, n: (b, n, h)),
        compiler_params=_params(("parallel", "parallel", "arbitrary")),
        name="window_attention",
    )(sink, proj3, proj3, proj3, proj3, cos_t, sin_lo, sin_hi, cos_t, sin_lo, sin_hi)


def _merge_out_kernel(*refs, n_gate_blocks, final):
    x_ref, yh_ref, ya_ref = refs[0:3]
    gh_refs = refs[3:3 + n_gate_blocks]
    ga_refs = refs[3 + n_gate_blocks:3 + 2 * n_gate_blocks]
    who_ref, wao_ref, wout_ref, fg_ref, o_ref, m_ref = refs[3 + 2 * n_gate_blocks:]
    gw = gh_refs[0].shape[1]
    yh = yh_ref[...]
    ya = ya_ref[...]
    for c in range(n_gate_blocks):
        cols = slice(c * gw, (c + 1) * gw)
        ph = jnp.dot(yh, who_ref[:, cols], preferred_element_type=F32)
        pa = jnp.dot(ya, wao_ref[:, cols], preferred_element_type=F32)
        merged = jax.nn.sigmoid(gh_refs[c][...]) * ph + jax.nn.sigmoid(ga_refs[c][...]) * pa
        m_ref[:, cols] = merged.astype(BF16)
    out = x_ref[...] + jnp.dot(m_ref[...], wout_ref[...], preferred_element_type=F32)
    if final:
        ms = jnp.mean(out * out, axis=-1, keepdims=True)
        out = out * lax.rsqrt(ms + EPS) * fg_ref[...]
    o_ref[...] = out


def _merge_out(x2d, proj, yh, ya, who, wao, wout, final_g, col_gh, col_ga, final, tm=256, gw=512):
    m, d = x2d.shape
    width = yh.shape[1]
    ngb = d // gw
    gspec = lambda col0, c: pl.BlockSpec((tm, gw), lambda i: (i, col0 // gw + c))
    kern = functools.partial(_merge_out_kernel, n_gate_blocks=ngb, final=final)
    return pl.pallas_call(
        kern,
        out_shape=jax.ShapeDtypeStruct((m, d), F32),
        grid=(m // tm,),
        in_specs=[pl.BlockSpec((tm, d), lambda i: (i, 0)),
                  pl.BlockSpec((tm, width), lambda i: (i, 0)),
                  pl.BlockSpec((tm, width), lambda i: (i, 0))]
                 + [gspec(col_gh, c) for c in range(ngb)]
                 + [gspec(col_ga, c) for c in range(ngb)]
                 + [_const_spec((width, d)), _const_spec((width, d)), _const_spec((d, d)),
                    pl.BlockSpec((1, d), lambda i: (0, 0))],
        out_specs=pl.BlockSpec((tm, d), lambda i: (i, 0)),
        scratch_shapes=[pltpu.VMEM((tm, d), BF16)],
        compiler_params=_params(("parallel",)),
        name="merge_out",
    )(x2d, yh, ya, *([proj] * (2 * ngb)), who, wao, wout, final_g.reshape(1, d))


def _dft_mats(L):
    idx = jnp.arange(L, dtype=jnp.int32)
    k = (idx[:, None] * idx[None, :]) % (2 * L)
    ang = k.astype(F32) * (math.pi / L)
    return jnp.cos(ang).astype(BF16), jnp.sin(ang).astype(BF16)


def _rope_tabs(L):
    inv = ROPE_THETA ** (-jnp.arange(0, ROPE_DIM, 2, dtype=F32) / ROPE_DIM)
    ang = jnp.arange(L, dtype=F32)[:, None] * inv[None, :]
    cos, sin = jnp.cos(ang), jnp.sin(ang)
    ones = jnp.ones((L, HEAD_DIM - ROPE_DIM), F32)
    zeros = jnp.zeros((L, HEAD_DIM - ROPE_HALF), F32)
    cos_t = jnp.concatenate([cos, cos, ones], axis=1)
    sin_lo = jnp.concatenate([-sin, zeros], axis=1)
    sin_hi = jnp.concatenate([jnp.zeros((L, ROPE_HALF), F32), sin, zeros[:, ROPE_HALF:]], axis=1)
    return cos_t, sin_lo, sin_hi


def _filter_feats(L):
    t = jnp.linspace(0.0, 1.0, L, dtype=F32)[:, None]
    bands = jnp.linspace(1e-4, FILTER_BANDS - 1, FILTER_BANDS, dtype=F32)[None, :]
    ang = (2.0 * math.pi / L) * jnp.arange(L, dtype=F32)[:, None] * bands
    feats = jnp.concatenate([t, jnp.cos(ang), -jnp.sin(ang)], axis=-1)
    feats = jnp.pad(feats, ((0, 0), (0, FEAT_PAD - feats.shape[1])))
    return feats, t


def kernel(x, norm_g, w_in, conv_w, conv_b, filt_w1, filt_b1, filt_w2, filt_b2, filt_w3, filt_b3,
           filt_w4, filt_freq, hyena_bias, attn_sink, w_hyena_out, w_attn_out, w_out, final_norm):
    B, L, D = x.shape
    depth = norm_g.shape[0]
    hw = hyena_bias.shape[2]
    aw = N_HEADS * HEAD_DIM
    kvw = N_KV_HEADS * HEAD_DIM
    sizes = (3 * hw, hw, aw, kvw, kvw, aw, D, D)
    cols = [0]
    for s in sizes:
        cols.append(cols[-1] + s)
    col_zhy, col_q, col_k, col_v, col_zat, col_gh, col_ga = cols[1:8]

    cmat, smat = _dft_mats(L)
    rope_tabs = _rope_tabs(L)
    feats, tcol = _filter_feats(L)
    deltas = jnp.abs(jnp.linspace(MIN_DECAY, MAX_DECAY, hw, dtype=F32))[None, :]

    xf = x.reshape(B * L, D)
    for l in range(depth):
        proj = _norm_proj(xf, norm_g[l], w_in[l].astype(BF16))
        proj3 = proj.reshape(B, L, -1)
        w1p = jnp.pad(filt_w1[l], ((0, FEAT_PAD - filt_w1.shape[1]), (0, 0)))
        kr, ki, kn = _filters(feats, tcol, w1p, filt_b1[l][None, :], filt_w2[l], filt_b2[l][None, :],
                              filt_w3[l], filt_b3[l][None, :], filt_freq[l][None, :], filt_w4[l],
                              deltas, cmat, smat)
        y_hy = _hyena(proj3, conv_w[l], conv_b[l], hyena_bias[l], cmat, smat, kr, ki, kn)
        y_at = _attention(proj3, attn_sink[l], rope_tabs, col_q, col_k, col_v, col_zat)
        xf = _merge_out(xf, proj, y_hy.reshape(B * L, hw), y_at.reshape(B * L, aw),
                        w_hyena_out[l].astype(BF16), w_attn_out[l].astype(BF16), w_out[l].astype(BF16),
                        final_norm, col_gh, col_ga, final=(l == depth - 1))
    return xf.reshape(B, L, D)
```

```python
import functools
import math

import jax
import jax.numpy as jnp
from jax import lax
from jax.experimental import pallas as pl
from jax.experimental.pallas import tpu as pltpu

F32 = jnp.float32
BF16 = jnp.bfloat16

HEAD_DIM = 128
N_HEADS = 8
N_KV_HEADS = 2
GROUP = N_HEADS // N_KV_HEADS
WINDOW = 128
ROPE_THETA = 500000.0
ROPE_DIM = HEAD_DIM // 4
ROPE_HALF = ROPE_DIM // 2
EPS = 1e-6
FILTER_BANDS = 16
FEAT_PAD = 128
DECAY_TARGET = 1e-2
MIN_DECAY = math.log(DECAY_TARGET) / 0.3
MAX_DECAY = math.log(DECAY_TARGET) / 1.5

NB = 2
VMEM_LIMIT = 56 * 1024 * 1024


def _params(sem, vmem=VMEM_LIMIT):
    return pltpu.CompilerParams(dimension_semantics=sem, vmem_limit_bytes=vmem)


def _const_spec(shape):
    return pl.BlockSpec(shape, lambda *_: (0,) * len(shape), pipeline_mode=pl.Buffered(1))


def _alt_sign(rows):
    return jnp.where((lax.broadcasted_iota(jnp.int32, (rows, 1), 0) & 1) == 0, 1.0, -1.0)


def _norm_proj_kernel(x_ref, g_ref, w_ref, o_ref, h_ref):
    @pl.when(pl.program_id(1) == 0)
    def _():
        x = x_ref[...]
        ms = jnp.mean(x * x, axis=-1, keepdims=True)
        h_ref[...] = (x * lax.rsqrt(ms + EPS) * g_ref[...]).astype(BF16)

    o_ref[...] = jnp.dot(h_ref[...], w_ref[...], preferred_element_type=F32)


def _norm_proj(x2d, g, w_bf16, tm=1024, tn=768):
    m, d = x2d.shape
    n = w_bf16.shape[1]
    return pl.pallas_call(
        _norm_proj_kernel,
        out_shape=jax.ShapeDtypeStruct((m, n), F32),
        grid=(m // tm, n // tn),
        in_specs=[
            pl.BlockSpec((tm, d), lambda i, j: (i, 0)),
            pl.BlockSpec((1, d), lambda i, j: (0, 0)),
            pl.BlockSpec((d, tn), lambda i, j: (0, j)),
        ],
        out_specs=pl.BlockSpec((tm, tn), lambda i, j: (i, j)),
        scratch_shapes=[pltpu.VMEM((tm, d), BF16)],
        compiler_params=_params(("parallel", "arbitrary")),
        name="norm_proj",
    )(x2d, g.reshape(1, d), w_bf16)


def _filter_kernel(feats_ref, t_ref, w1_ref, b1_ref, w2_ref, b2_ref, w3_ref, b3_ref, fr_ref,
                   w4f_ref, w4b_ref, dl_ref, c_ref, s_ref,
                   kr_ref, ki_ref, kn_ref,
                   hdn_ref, xf_ref, xb_ref, cf_ref, sf_ref, cb_ref, sb_ref, *, nb, row_chunk):
    L = feats_ref.shape[0]
    b = L // nb
    tc = dl_ref.shape[1]
    hp = lax.Precision.HIGHEST
    nchunks = b // row_chunk
    alt = _alt_sign(row_chunk)

    @pl.when((pl.program_id(0) == 0) & (pl.program_id(1) == 0))
    def _():
        fr = fr_ref[...]
        h = jnp.sin(fr * (jnp.dot(feats_ref[...], w1_ref[...], precision=hp,
                                  preferred_element_type=F32) + b1_ref[...]))
        h = jnp.sin(fr * (jnp.dot(h, w2_ref[...], precision=hp,
                                  preferred_element_type=F32) + b2_ref[...]))
        h = jnp.sin(fr * (jnp.dot(h, w3_ref[...], precision=hp,
                                  preferred_element_type=F32) + b3_ref[...]))
        hdn_ref[...] = h

    def taps(rows):
        h = hdn_ref[rows, :]
        decay = jnp.exp(-t_ref[rows, :] * dl_ref[...])
        fwd = jnp.dot(h, w4f_ref[...], precision=hp, preferred_element_type=F32) * decay
        bwd = jnp.dot(h, w4b_ref[...], precision=hp, preferred_element_type=F32) * decay
        return fwd, bwd

    zero_row = jnp.zeros((1, tc), F32)
    f0, b0, af, ab = [], [], [], []
    for q in range(nb):
        head_f, head_b = taps(pl.ds(q * b, 8))
        f0.append(head_f[0:1, :])
        b0.append(head_b[0:1, :])

        def tap_chunk(i, carry, q=q):
            l0 = pl.multiple_of(i * row_chunk, row_chunk)
            fwd, bwd = taps(pl.ds(pl.multiple_of(q * b + l0, row_chunk), row_chunk))
            first = (lax.broadcasted_iota(jnp.int32, (row_chunk, 1), 0) + l0) == 0
            fz = jnp.where(first, 0.0, fwd)
            bz = jnp.where(first, 0.0, bwd)
            xf_ref[q, pl.ds(l0, row_chunk), :] = fz.astype(BF16)
            xb_ref[q, pl.ds(l0, row_chunk), :] = bz.astype(BF16)
            return (carry[0] + jnp.sum(fz * alt, axis=0, keepdims=True),
                    carry[1] + jnp.sum(bz * alt, axis=0, keepdims=True))

        a_f, a_b = lax.fori_loop(0, nchunks, tap_chunk, (zero_row, zero_row))
        af.append(a_f)
        ab.append(a_b)

        def spec_chunk(i, carry, q=q):
            r = pl.ds(pl.multiple_of(i * row_chunk, row_chunk), row_chunk)
            cf_ref[q, r, :] = jnp.dot(c_ref[r, :], xf_ref[q], preferred_element_type=F32)
            sf_ref[q, r, :] = jnp.dot(s_ref[r, :], xf_ref[q], preferred_element_type=F32)
            cb_ref[q, r, :] = jnp.dot(c_ref[r, :], xb_ref[q], preferred_element_type=F32)
            sb_ref[q, r, :] = jnp.dot(s_ref[r, :], xb_ref[q], preferred_element_type=F32)
            return carry

        lax.fori_loop(0, nchunks, spec_chunk, 0)

    inv_n = 1.0 / (2 * b)
    for d in range(-(nb - 1), nb):
        slot = d + nb - 1
        e = -d
        if d >= 1:
            kn = af[d] + f0[d] + af[d - 1]
        elif d == 0:
            kn = af[0] + f0[0] + ab[0]
        else:
            kn = b0[e] + ab[e] + ab[e - 1]
        kn_ref[slot] = kn * inv_n

        def combine(i, carry, d=d, e=e, slot=slot):
            r0 = pl.multiple_of(i * row_chunk, row_chunk)
            r = pl.ds(r0, row_chunk)
            first = (lax.broadcasted_iota(jnp.int32, (row_chunk, 1), 0) + r0) == 0
            wgt = jnp.where(first, inv_n, 2.0 * inv_n)
            if d >= 1:
                kr = cf_ref[d, r, :] + f0[d] + alt * cf_ref[d - 1, r, :]
                ki = -sf_ref[d, r, :] - alt * sf_ref[d - 1, r, :]
            elif d == 0:
                kr = cf_ref[0, r, :] + f0[0] + cb_ref[0, r, :]
                ki = sb_ref[0, r, :] - sf_ref[0, r, :]
            else:
                kr = b0[e] + cb_ref[e, r, :] + alt * cb_ref[e - 1, r, :]
                ki = sb_ref[e, r, :] + alt * sb_ref[e - 1, r, :]
            kr_ref[slot, r, :] = kr * wgt
            ki_ref[slot, r, :] = ki * wgt
            return carry

        lax.fori_loop(0, nchunks, combine, 0)


def _filters(feats, tcol, w1p, b1, w2, b2, w3, b3, freq, w4, deltas, cmat, smat, nb, tc=256, row_chunk=512):
    L = feats.shape[0]
    b = L // nb
    nd = 2 * nb - 1
    width = deltas.shape[1]
    nct = width // tc
    hid = w2.shape[0]
    small = lambda a: pl.BlockSpec(a.shape, lambda o, c: (0,) * a.ndim)
    kern = functools.partial(_filter_kernel, nb=nb, row_chunk=row_chunk)
    kspec = pl.BlockSpec((None, nd, b, tc), lambda o, c: (o, 0, 0, c))
    blk = lambda dt: pltpu.VMEM((nb, b, tc), dt)
    return pl.pallas_call(
        kern,
        out_shape=(jax.ShapeDtypeStruct((2, nd, b, width), F32),
                   jax.ShapeDtypeStruct((2, nd, b, width), F32),
                   jax.ShapeDtypeStruct((2, nd, 1, width), F32)),
        grid=(2, nct),
        in_specs=[small(feats), small(tcol), small(w1p), small(b1), small(w2), small(b2),
                  small(w3), small(b3), small(freq),
                  pl.BlockSpec((hid, tc), lambda o, c: (0, o * 2 * nct + c)),
                  pl.BlockSpec((hid, tc), lambda o, c: (0, o * 2 * nct + nct + c)),
                  pl.BlockSpec((1, tc), lambda o, c: (0, c)),
                  _const_spec((b, b)), _const_spec((b, b))],
        out_specs=(kspec, kspec, pl.BlockSpec((None, nd, 1, tc), lambda o, c: (o, 0, 0, c))),
        scratch_shapes=[pltpu.VMEM((L, hid), F32), blk(BF16), blk(BF16),
                        blk(F32), blk(F32), blk(F32), blk(F32)],
        compiler_params=_params(("arbitrary", "arbitrary")),
        name="hyena_filters",
    )(feats, tcol, w1p, b1, w2, b2, w3, b3, freq, w4, w4, deltas, cmat, smat)


def _sconv_chunk(u_ref, w_ref, b_ref, i, rows, nchunks):
    r0 = i * rows
    tc = u_ref.shape[1]
    u = u_ref[r0:r0 + rows, :]
    zero = jnp.zeros((1, tc), F32)
    up = u_ref[r0 - 8:r0, :][7:8, :] if i > 0 else zero
    dn = u_ref[r0 + rows:r0 + rows + 8, :][0:1, :] if i < nchunks - 1 else zero
    row = lax.broadcasted_iota(jnp.int32, (rows, 1), 0)
    prev = jnp.where(row == 0, up, pltpu.roll(u, 1, 0))
    nxt = jnp.where(row == rows - 1, dn, pltpu.roll(u, rows - 1, 0))
    return b_ref[...] + prev * w_ref[0:1, :] + u * w_ref[1:2, :] + nxt * w_ref[2:3, :]


def _hyena_kernel(v_ref, x1_ref, x2_ref, zg_ref, wv_ref, wx1_ref, wx2_ref, bv_ref, bx1_ref, bx2_ref,
                  hb_ref, c_ref, s_ref, kr_ref, ki_ref, kn_ref, o_ref,
                  u_ref, ub_ref, a_ref, bn_ref, *, nb, seq_chunk, freq_chunk):
    L, tc = u_ref.shape
    b = L // nb
    per_block = b // seq_chunk
    nseq = L // seq_chunk
    alt = _alt_sign(seq_chunk)

    def lanes(j):
        return slice(j * tc, (j + 1) * tc)

    def put_signal(chunk, val, nyq):
        j, local = divmod(chunk, per_block)
        rows = slice(chunk * seq_chunk, (chunk + 1) * seq_chunk)
        u_ref[rows, :] = val
        ub_ref[local * seq_chunk:(local + 1) * seq_chunk, lanes(j)] = val.astype(BF16)
        nyq[j] = nyq[j] + jnp.sum(val * alt, axis=0, keepdims=True)

    def forward(order):
        for fc in range(b // freq_chunk):
            r = slice(fc * freq_chunk, (fc + 1) * freq_chunk)
            ur = jnp.dot(c_ref[r, :], ub_ref[...], preferred_element_type=F32)
            us = jnp.dot(s_ref[r, :], ub_ref[...], preferred_element_type=F32)
            for i in range(nb):
                acc_a = acc_b = None
                for j in range(nb):
                    kr = kr_ref[order, i - j + nb - 1, r, :]
                    ki = ki_ref[order, i - j + nb - 1, r, :]
                    urj, usj = ur[:, lanes(j)], us[:, lanes(j)]
                    ta = urj * kr + usj * ki
                    tb = usj * kr - urj * ki
                    acc_a = ta if acc_a is None else acc_a + ta
                    acc_b = tb if acc_b is None else acc_b + tb
                a_ref[r, lanes(i)] = acc_a.astype(BF16)
                bn_ref[r, lanes(i)] = acc_b.astype(BF16)

    def inverse(order, nyq, finish):
        nyq_term = []
        for i in range(nb):
            acc = None
            for j in range(nb):
                t = nyq[j] * kn_ref[order, i - j + nb - 1]
                acc = t if acc is None else acc + t
            nyq_term.append(acc)
        for tcn in range(per_block):
            r = slice(tcn * seq_chunk, (tcn + 1) * seq_chunk)
            y2 = jnp.dot(c_ref[r, :], a_ref[...], preferred_element_type=F32)
            y2 = y2 + jnp.dot(s_ref[r, :], bn_ref[...], preferred_element_type=F32)
            for i in range(nb):
                chunk = i * per_block + tcn
                rows = slice(chunk * seq_chunk, (chunk + 1) * seq_chunk)
                y = y2[:, lanes(i)] + alt * nyq_term[i]
                finish(chunk, rows, y + u_ref[rows, :] * hb_ref[order:order + 1, :])

    zero_row = jnp.zeros((1, tc), F32)
    nyq0 = [zero_row] * nb
    for chunk in range(nseq):
        put_signal(chunk, _sconv_chunk(v_ref, wv_ref, bv_ref, chunk, seq_chunk, nseq), nyq0)

    forward(0)

    nyq1 = [zero_row] * nb

    def finish_z(chunk, rows, y):
        put_signal(chunk, _sconv_chunk(x1_ref, wx1_ref, bx1_ref, chunk, seq_chunk, nseq) * y, nyq1)

    inverse(0, nyq0, finish_z)
    forward(1)

    def finish_out(chunk, rows, y):
        y = _sconv_chunk(x2_ref, wx2_ref, bx2_ref, chunk, seq_chunk, nseq) * y
        zg = zg_ref[rows, :]
        o_ref[rows, :] = (y * (zg * jax.nn.sigmoid(zg))).astype(o_ref.dtype)

    inverse(1, nyq1, finish_out)


def _hyena(proj3, conv_w, conv_b, hyena_bias, cmat, smat, kr, ki, kn, nb, tc=256,
           seq_chunk=512, freq_chunk=256):
    B, L, _ = proj3.shape
    b = L // nb
    nd = 2 * nb - 1
    width = hyena_bias.shape[1]
    nct = width // tc
    conv_b2 = conv_b.reshape(1, -1)
    sig = lambda part: pl.BlockSpec((None, L, tc), lambda j, bb: (bb, 0, part * nct + j))
    cw = lambda part: pl.BlockSpec((3, tc), lambda j, bb: (0, part * nct + j))
    cb = lambda part: pl.BlockSpec((1, tc), lambda j, bb: (0, part * nct + j))
    kspec = pl.BlockSpec((2, nd, b, tc), lambda j, bb: (0, 0, 0, j), pipeline_mode=pl.Buffered(1))
    kern = functools.partial(_hyena_kernel, nb=nb, seq_chunk=seq_chunk, freq_chunk=freq_chunk)
    return pl.pallas_call(
        kern,
        out_shape=jax.ShapeDtypeStruct((B, L, width), BF16),
        grid=(nct, B),
        in_specs=[sig(0), sig(1), sig(2), sig(3),
                  cw(0), cw(1), cw(2), cb(0), cb(1), cb(2),
                  pl.BlockSpec((2, tc), lambda j, bb: (0, j)),
                  _const_spec((b, b)), _const_spec((b, b)),
                  kspec, kspec,
                  pl.BlockSpec((2, nd, 1, tc), lambda j, bb: (0, 0, 0, j))],
        out_specs=pl.BlockSpec((None, L, tc), lambda j, bb: (bb, 0, j)),
        scratch_shapes=[pltpu.VMEM((L, tc), F32), pltpu.VMEM((b, nb * tc), BF16),
                        pltpu.VMEM((b, nb * tc), BF16), pltpu.VMEM((b, nb * tc), BF16)],
        compiler_params=_params(("parallel", "arbitrary")),
        name="hyena_mixer",
    )(proj3, proj3, proj3, proj3, conv_w, conv_w, conv_w, conv_b2, conv_b2, conv_b2,
      hyena_bias, cmat, smat, kr, ki, kn)


def _rope(x, cos, sin_lo, sin_hi):
    return (x * cos + pltpu.roll(x, HEAD_DIM - ROPE_HALF, 1) * sin_lo
            + pltpu.roll(x, ROPE_HALF, 1) * sin_hi)


def _attn_kernel(sink_ref, q_ref, k_ref, v_ref, zg_ref, cq_ref, slq_ref, shq_ref,
                 ck_ref, slk_ref, shk_ref, o_ref):
    L = k_ref.shape[0]
    qb = q_ref.shape[0]
    kw = qb + 2 * WINDOW
    kvh = pl.program_id(1)
    n = pl.program_id(2)
    start = pl.multiple_of(jnp.clip(n * qb - WINDOW, 0, L - kw), WINDOW)
    win = pl.ds(start, kw)
    kb = _rope(k_ref[win, :], ck_ref[win, :], slk_ref[win, :], shk_ref[win, :]).astype(BF16)
    vb = v_ref[win, :].astype(BF16)
    qpos = n * qb + lax.broadcasted_iota(jnp.int32, (qb, 1), 0)
    kpos = start + lax.broadcasted_iota(jnp.int32, (1, kw), 1)
    valid = jnp.abs(kpos - qpos) <= WINDOW
    cq, slq, shq = cq_ref[...], slq_ref[...], shq_ref[...]
    q = jnp.concatenate(
        [_rope(q_ref[:, g * HEAD_DIM:(g + 1) * HEAD_DIM], cq, slq, shq).astype(BF16) for g in range(GROUP)],
        axis=0)
    s = lax.dot_general(q, kb, (((1,), (1,)), ((), ())), preferred_element_type=F32) * (HEAD_DIM ** -0.5)
    s = jnp.where(valid[None], s.reshape(GROUP, qb, kw), -jnp.inf)
    head = lax.broadcasted_iota(jnp.int32, (GROUP, 1, 1), 0)
    sk = jnp.zeros((GROUP, 1, 1), F32)
    for g in range(GROUP):
        sk = jnp.where(head == g, sink_ref[kvh * GROUP + g], sk)
    m = jnp.maximum(jnp.max(s, axis=-1, keepdims=True), sk)
    p = jnp.exp(s - m)
    p = p / (jnp.sum(p, axis=-1, keepdims=True) + jnp.exp(sk - m))
    o = jnp.dot(p.reshape(GROUP * qb, kw).astype(BF16), vb, preferred_element_type=F32)
    for g in range(GROUP):
        cols = slice(g * HEAD_DIM, (g + 1) * HEAD_DIM)
        zg = zg_ref[:, cols]
        o_ref[:, cols] = (o[g * qb:(g + 1) * qb] * (zg * jax.nn.sigmoid(zg))).astype(o_ref.dtype)


def _attention(proj3, sink, rope_tabs, col_q, col_k, col_v, col_zg, qb=256):
    B, L, _ = proj3.shape
    gw = GROUP * HEAD_DIM
    cos_t, sin_lo, sin_hi = rope_tabs
    qtab = pl.BlockSpec((qb, HEAD_DIM), lambda b, h, n: (n, 0))
    ktab = pl.BlockSpec((L, HEAD_DIM), lambda b, h, n: (0, 0))
    return pl.pallas_call(
        _attn_kernel,
        out_shape=jax.ShapeDtypeStruct((B, L, N_HEADS * HEAD_DIM), BF16),
        grid=(B, N_KV_HEADS, L // qb),
        in_specs=[pl.BlockSpec(memory_space=pltpu.SMEM),
                  pl.BlockSpec((None, qb, gw), lambda b, h, n: (b, n, col_q // gw + h)),
                  pl.BlockSpec((None, L, HEAD_DIM), lambda b, h, n: (b, 0, col_k // HEAD_DIM + h)),
                  pl.BlockSpec((None, L, HEAD_DIM), lambda b, h, n: (b, 0, col_v // HEAD_DIM + h)),
                  pl.BlockSpec((None, qb, gw), lambda b, h, n: (b, n, col_zg // gw + h)),
                  qtab, qtab, qtab, ktab, ktab, ktab],
        out_specs=pl.BlockSpec((None, qb, gw), lambda b, h, n: (b, n, h)),
        compiler_params=_params(("parallel", "parallel", "arbitrary")),
        name="window_attention",
    )(sink, proj3, proj3, proj3, proj3, cos_t, sin_lo, sin_hi, cos_t, sin_lo, sin_hi)


def _merge_out_kernel(*refs, n_gate_blocks, final):
    x_ref, yh_ref, ya_ref = refs[0:3]
    gh_refs = refs[3:3 + n_gate_blocks]
    ga_refs = refs[3 + n_gate_blocks:3 + 2 * n_gate_blocks]
    who_ref, wao_ref, wout_ref, fg_ref, o_ref, m_ref = refs[3 + 2 * n_gate_blocks:]
    gw = gh_refs[0].shape[1]
    yh = yh_ref[...]
    ya = ya_ref[...]
    for c in range(n_gate_blocks):
        cols = slice(c * gw, (c + 1) * gw)
        ph = jnp.dot(yh, who_ref[:, cols], preferred_element_type=F32)
        pa = jnp.dot(ya, wao_ref[:, cols], preferred_element_type=F32)
        merged = jax.nn.sigmoid(gh_refs[c][...]) * ph + jax.nn.sigmoid(ga_refs[c][...]) * pa
        m_ref[:, cols] = merged.astype(BF16)
    out = x_ref[...] + jnp.dot(m_ref[...], wout_ref[...], preferred_element_type=F32)
    if final:
        ms = jnp.mean(out * out, axis=-1, keepdims=True)
        out = out * lax.rsqrt(ms + EPS) * fg_ref[...]
    o_ref[...] = out


def _merge_out(x2d, proj, yh, ya, who, wao, wout, final_g, col_gh, col_ga, final, tm=256, gw=512):
    m, d = x2d.shape
    width = yh.shape[1]
    ngb = d // gw
    gspec = lambda col0, c: pl.BlockSpec((tm, gw), lambda i: (i, col0 // gw + c))
    kern = functools.partial(_merge_out_kernel, n_gate_blocks=ngb, final=final)
    return pl.pallas_call(
        kern,
        out_shape=jax.ShapeDtypeStruct((m, d), F32),
        grid=(m // tm,),
        in_specs=[pl.BlockSpec((tm, d), lambda i: (i, 0)),
                  pl.BlockSpec((tm, width), lambda i: (i, 0)),
                  pl.BlockSpec((tm, width), lambda i: (i, 0))]
                 + [gspec(col_gh, c) for c in range(ngb)]
                 + [gspec(col_ga, c) for c in range(ngb)]
                 + [_const_spec((width, d)), _const_spec((width, d)), _const_spec((d, d)),
                    pl.BlockSpec((1, d), lambda i: (0, 0))],
        out_specs=pl.BlockSpec((tm, d), lambda i: (i, 0)),
        scratch_shapes=[pltpu.VMEM((tm, d), BF16)],
        compiler_params=_params(("parallel",)),
        name="merge_out",
    )(x2d, yh, ya, *([proj] * (2 * ngb)), who, wao, wout, final_g.reshape(1, d))


def _dft_mats(b):
    idx = jnp.arange(b, dtype=jnp.int32)
    k = (idx[:, None] * idx[None, :]) % (2 * b)
    ang = k.astype(F32) * (math.pi / b)
    return jnp.cos(ang).astype(BF16), jnp.sin(ang).astype(BF16)


def _rope_tabs(L):
    inv = ROPE_THETA ** (-jnp.arange(0, ROPE_DIM, 2, dtype=F32) / ROPE_DIM)
    ang = jnp.arange(L, dtype=F32)[:, None] * inv[None, :]
    cos, sin = jnp.cos(ang), jnp.sin(ang)
    ones = jnp.ones((L, HEAD_DIM - ROPE_DIM), F32)
    zeros = jnp.zeros((L, HEAD_DIM - ROPE_HALF), F32)
    cos_t = jnp.concatenate([cos, cos, ones], axis=1)
    sin_lo = jnp.concatenate([-sin, zeros], axis=1)
    sin_hi = jnp.concatenate([jnp.zeros((L, ROPE_HALF), F32), sin, zeros[:, ROPE_HALF:]], axis=1)
    return cos_t, sin_lo, sin_hi


def _filter_feats(L):
    t = jnp.linspace(0.0, 1.0, L, dtype=F32)[:, None]
    bands = jnp.linspace(1e-4, FILTER_BANDS - 1, FILTER_BANDS, dtype=F32)[None, :]
    ang = (2.0 * math.pi / L) * jnp.arange(L, dtype=F32)[:, None] * bands
    feats = jnp.concatenate([t, jnp.cos(ang), -jnp.sin(ang)], axis=-1)
    feats = jnp.pad(feats, ((0, 0), (0, FEAT_PAD - feats.shape[1])))
    return feats, t


def kernel(x, norm_g, w_in, conv_w, conv_b, filt_w1, filt_b1, filt_w2, filt_b2, filt_w3, filt_b3,
           filt_w4, filt_freq, hyena_bias, attn_sink, w_hyena_out, w_attn_out, w_out, final_norm):
    B, L, D = x.shape
    depth = norm_g.shape[0]
    hw = hyena_bias.shape[2]
    aw = N_HEADS * HEAD_DIM
    kvw = N_KV_HEADS * HEAD_DIM
    sizes = (3 * hw, hw, aw, kvw, kvw, aw, D, D)
    cols = [0]
    for s in sizes:
        cols.append(cols[-1] + s)
    col_zhy, col_q, col_k, col_v, col_zat, col_gh, col_ga = cols[1:8]

    cmat, smat = _dft_mats(L // NB)
    rope_tabs = _rope_tabs(L)
    feats, tcol = _filter_feats(L)
    deltas = jnp.abs(jnp.linspace(MIN_DECAY, MAX_DECAY, hw, dtype=F32))[None, :]

    xf = x.reshape(B * L, D)
    for l in range(depth):
        proj = _norm_proj(xf, norm_g[l], w_in[l].astype(BF16))
        proj3 = proj.reshape(B, L, -1)
        w1p = jnp.pad(filt_w1[l], ((0, FEAT_PAD - filt_w1.shape[1]), (0, 0)))
        kr, ki, kn = _filters(feats, tcol, w1p, filt_b1[l][None, :], filt_w2[l], filt_b2[l][None, :],
                              filt_w3[l], filt_b3[l][None, :], filt_freq[l][None, :], filt_w4[l],
                              deltas, cmat, smat, NB)
        y_hy = _hyena(proj3, conv_w[l], conv_b[l], hyena_bias[l], cmat, smat, kr, ki, kn, NB)
        y_at = _attention(proj3, attn_sink[l], rope_tabs, col_q, col_k, col_v, col_zat)
        xf = _merge_out(xf, proj, y_hy.reshape(B * L, hw), y_at.reshape(B * L, aw),
                        w_hyena_out[l].astype(BF16), w_attn_out[l].astype(BF16), w_out[l].astype(BF16),
                        final_norm, col_gh, col_ga, final=(l == depth - 1))
    return xf.reshape(B, L, D)
```

```python
import functools
import math

import jax
import jax.numpy as jnp
from jax import lax
from jax.experimental import pallas as pl
from jax.experimental.pallas import tpu as pltpu

F32 = jnp.float32
BF16 = jnp.bfloat16

HEAD_DIM = 128
N_HEADS = 8
N_KV_HEADS = 2
GROUP = N_HEADS // N_KV_HEADS
WINDOW = 128
ROPE_THETA = 500000.0
ROPE_DIM = HEAD_DIM // 4
ROPE_HALF = ROPE_DIM // 2
EPS = 1e-6
FILTER_BANDS = 16
FEAT_PAD = 128
DECAY_TARGET = 1e-2
MIN_DECAY = math.log(DECAY_TARGET) / 0.3
MAX_DECAY = math.log(DECAY_TARGET) / 1.5

NB = 4
VMEM_LIMIT = 56 * 1024 * 1024


def _params(sem, vmem=VMEM_LIMIT):
    return pltpu.CompilerParams(dimension_semantics=sem, vmem_limit_bytes=vmem)


def _const_spec(shape):
    return pl.BlockSpec(shape, lambda *_: (0,) * len(shape), pipeline_mode=pl.Buffered(1))


def _alt_sign(rows):
    return jnp.where((lax.broadcasted_iota(jnp.int32, (rows, 1), 0) & 1) == 0, 1.0, -1.0)


def _norm_proj_kernel(x_ref, g_ref, w_ref, o_ref, h_ref):
    @pl.when(pl.program_id(1) == 0)
    def _():
        x = x_ref[...]
        ms = jnp.mean(x * x, axis=-1, keepdims=True)
        h_ref[...] = (x * lax.rsqrt(ms + EPS) * g_ref[...]).astype(BF16)

    o_ref[...] = jnp.dot(h_ref[...], w_ref[...], preferred_element_type=F32)


def _norm_proj(x2d, g_all, w_all, l, tm=1024, tn=768):
    m, d = x2d.shape
    n = w_all.shape[2]
    return pl.pallas_call(
        _norm_proj_kernel,
        out_shape=jax.ShapeDtypeStruct((m, n), F32),
        grid=(m // tm, n // tn),
        in_specs=[
            pl.BlockSpec((tm, d), lambda i, j: (i, 0)),
            pl.BlockSpec((None, 1, d), lambda i, j: (l, 0, 0)),
            pl.BlockSpec((None, d, tn), lambda i, j: (l, 0, j)),
        ],
        out_specs=pl.BlockSpec((tm, tn), lambda i, j: (i, j)),
        scratch_shapes=[pltpu.VMEM((tm, d), BF16)],
        compiler_params=_params(("parallel", "arbitrary")),
        name="norm_proj",
    )(x2d, g_all, w_all)


def _filter_kernel(feats_ref, t_ref, w1_ref, b1_ref, w2_ref, b2_ref, w3_ref, b3_ref, fr_ref,
                   w4f_ref, w4b_ref, dl_ref, c_ref, s_ref,
                   kr_ref, ki_ref, kn_ref,
                   hdn_ref, xf_ref, xb_ref, cf_ref, sf_ref, cb_ref, sb_ref, *, nb, row_chunk):
    L = feats_ref.shape[0]
    b = L // nb
    tc = dl_ref.shape[1]
    hp = lax.Precision.HIGHEST
    nchunks = b // row_chunk
    alt = _alt_sign(row_chunk)

    @pl.when((pl.program_id(0) == 0) & (pl.program_id(1) == 0))
    def _():
        fr = fr_ref[...]
        h = jnp.sin(fr * (jnp.dot(feats_ref[...], w1_ref[...], precision=hp,
                                  preferred_element_type=F32) + b1_ref[...]))
        h = jnp.sin(fr * (jnp.dot(h, w2_ref[...], precision=hp,
                                  preferred_element_type=F32) + b2_ref[...]))
        h = jnp.sin(fr * (jnp.dot(h, w3_ref[...], precision=hp,
                                  preferred_element_type=F32) + b3_ref[...]))
        hdn_ref[...] = h

    def taps(rows):
        h = hdn_ref[rows, :]
        decay = jnp.exp(-t_ref[rows, :] * dl_ref[...])
        fwd = jnp.dot(h, w4f_ref[...], precision=hp, preferred_element_type=F32) * decay
        bwd = jnp.dot(h, w4b_ref[...], precision=hp, preferred_element_type=F32) * decay
        return fwd, bwd

    zero_row = jnp.zeros((1, tc), F32)
    f0, b0, af, ab = [], [], [], []
    for q in range(nb):
        head_f, head_b = taps(pl.ds(q * b, 8))
        f0.append(head_f[0:1, :])
        b0.append(head_b[0:1, :])

        def tap_chunk(i, carry, q=q):
            l0 = pl.multiple_of(i * row_chunk, row_chunk)
            fwd, bwd = taps(pl.ds(pl.multiple_of(q * b + l0, row_chunk), row_chunk))
            first = (lax.broadcasted_iota(jnp.int32, (row_chunk, 1), 0) + l0) == 0
            fz = jnp.where(first, 0.0, fwd)
            bz = jnp.where(first, 0.0, bwd)
            xf_ref[q, pl.ds(l0, row_chunk), :] = fz.astype(BF16)
            xb_ref[q, pl.ds(l0, row_chunk), :] = bz.astype(BF16)
            return (carry[0] + jnp.sum(fz * alt, axis=0, keepdims=True),
                    carry[1] + jnp.sum(bz * alt, axis=0, keepdims=True))

        a_f, a_b = lax.fori_loop(0, nchunks, tap_chunk, (zero_row, zero_row))
        af.append(a_f)
        ab.append(a_b)

        def spec_chunk(i, carry, q=q):
            r = pl.ds(pl.multiple_of(i * row_chunk, row_chunk), row_chunk)
            cf_ref[q, r, :] = jnp.dot(c_ref[r, :], xf_ref[q], preferred_element_type=F32)
            sf_ref[q, r, :] = jnp.dot(s_ref[r, :], xf_ref[q], preferred_element_type=F32)
            cb_ref[q, r, :] = jnp.dot(c_ref[r, :], xb_ref[q], preferred_element_type=F32)
            sb_ref[q, r, :] = jnp.dot(s_ref[r, :], xb_ref[q], preferred_element_type=F32)
            return carry

        lax.fori_loop(0, nchunks, spec_chunk, 0)

    inv_n = 1.0 / (2 * b)
    for d in range(-(nb - 1), nb):
        slot = d + nb - 1
        e = -d
        if d >= 1:
            kn = af[d] + f0[d] + af[d - 1]
        elif d == 0:
            kn = af[0] + f0[0] + ab[0]
        else:
            kn = b0[e] + ab[e] + ab[e - 1]
        kn_ref[slot] = kn * inv_n

        def combine(i, carry, d=d, e=e, slot=slot):
            r0 = pl.multiple_of(i * row_chunk, row_chunk)
            r = pl.ds(r0, row_chunk)
            first = (lax.broadcasted_iota(jnp.int32, (row_chunk, 1), 0) + r0) == 0
            wgt = jnp.where(first, inv_n, 2.0 * inv_n)
            if d >= 1:
                kr = cf_ref[d, r, :] + f0[d] + alt * cf_ref[d - 1, r, :]
                ki = -sf_ref[d, r, :] - alt * sf_ref[d - 1, r, :]
            elif d == 0:
                kr = cf_ref[0, r, :] + f0[0] + cb_ref[0, r, :]
                ki = sb_ref[0, r, :] - sf_ref[0, r, :]
            else:
                kr = b0[e] + cb_ref[e, r, :] + alt * cb_ref[e - 1, r, :]
                ki = sb_ref[e, r, :] + alt * sb_ref[e - 1, r, :]
            kr_ref[slot, r, :] = kr * wgt
            ki_ref[slot, r, :] = ki * wgt
            return carry

        lax.fori_loop(0, nchunks, combine, 0)


def _filters(feats, tcol, w1p, b1, w2, b2, w3, b3, freq, w4, deltas, cmat, smat, l, nb, tc=256,
             row_chunk=512):
    L = feats.shape[0]
    b = L // nb
    nd = 2 * nb - 1
    width = deltas.shape[1]
    nct = width // tc
    hid = w2.shape[1]
    small = lambda a: pl.BlockSpec(a.shape, lambda o, c: (0,) * a.ndim)
    layer = lambda a: pl.BlockSpec((None,) + a.shape[1:], lambda o, c: (l,) + (0,) * (a.ndim - 1))
    kern = functools.partial(_filter_kernel, nb=nb, row_chunk=row_chunk)
    kspec = pl.BlockSpec((None, nd, b, tc), lambda o, c: (o, 0, 0, c))
    blk = lambda dt: pltpu.VMEM((nb, b, tc), dt)
    return pl.pallas_call(
        kern,
        out_shape=(jax.ShapeDtypeStruct((2, nd, b, width), F32),
                   jax.ShapeDtypeStruct((2, nd, b, width), F32),
                   jax.ShapeDtypeStruct((2, nd, 1, width), F32)),
        grid=(2, nct),
        in_specs=[small(feats), small(tcol), layer(w1p), layer(b1), layer(w2), layer(b2),
                  layer(w3), layer(b3), layer(freq),
                  pl.BlockSpec((None, hid, tc), lambda o, c: (l, 0, o * 2 * nct + c)),
                  pl.BlockSpec((None, hid, tc), lambda o, c: (l, 0, o * 2 * nct + nct + c)),
                  pl.BlockSpec((1, tc), lambda o, c: (0, c)),
                  _const_spec((b, b)), _const_spec((b, b))],
        out_specs=(kspec, kspec, pl.BlockSpec((None, nd, 1, tc), lambda o, c: (o, 0, 0, c))),
        scratch_shapes=[pltpu.VMEM((L, hid), F32), blk(BF16), blk(BF16),
                        blk(F32), blk(F32), blk(F32), blk(F32)],
        compiler_params=_params(("arbitrary", "arbitrary")),
        name="hyena_filters",
    )(feats, tcol, w1p, b1, w2, b2, w3, b3, freq, w4, w4, deltas, cmat, smat)


def _sconv_chunk(u_ref, w_ref, b_ref, i, rows, nchunks):
    r0 = i * rows
    tc = u_ref.shape[1]
    u = u_ref[r0:r0 + rows, :]
    zero = jnp.zeros((1, tc), F32)
    up = u_ref[r0 - 8:r0, :][7:8, :] if i > 0 else zero
    dn = u_ref[r0 + rows:r0 + rows + 8, :][0:1, :] if i < nchunks - 1 else zero
    row = lax.broadcasted_iota(jnp.int32, (rows, 1), 0)
    prev = jnp.where(row == 0, up, pltpu.roll(u, 1, 0))
    nxt = jnp.where(row == rows - 1, dn, pltpu.roll(u, rows - 1, 0))
    return b_ref[...] + prev * w_ref[0:1, :] + u * w_ref[1:2, :] + nxt * w_ref[2:3, :]


def _hyena_kernel(v_ref, x1_ref, x2_ref, zg_ref, wv_ref, wx1_ref, wx2_ref, bv_ref, bx1_ref, bx2_ref,
                  hb_ref, c_ref, s_ref, kr_ref, ki_ref, kn_ref, o_ref,
                  u_ref, ub_ref, a_ref, bn_ref, *, nb, seq_chunk, freq_chunk):
    L, tc = u_ref.shape
    b = L // nb
    per_block = b // seq_chunk
    nseq = L // seq_chunk
    alt = _alt_sign(seq_chunk)

    def lanes(j):
        return slice(j * tc, (j + 1) * tc)

    def put_signal(chunk, val, nyq):
        j, local = divmod(chunk, per_block)
        rows = slice(chunk * seq_chunk, (chunk + 1) * seq_chunk)
        u_ref[rows, :] = val
        ub_ref[local * seq_chunk:(local + 1) * seq_chunk, lanes(j)] = val.astype(BF16)
        nyq[j] = nyq[j] + jnp.sum(val * alt, axis=0, keepdims=True)

    def forward(order):
        for fc in range(b // freq_chunk):
            r = slice(fc * freq_chunk, (fc + 1) * freq_chunk)
            ur = jnp.dot(c_ref[r, :], ub_ref[...], preferred_element_type=F32)
            us = jnp.dot(s_ref[r, :], ub_ref[...], preferred_element_type=F32)
            for i in range(nb):
                acc_a = acc_b = None
                for j in range(nb):
                    kr = kr_ref[order, i - j + nb - 1, r, :]
                    ki = ki_ref[order, i - j + nb - 1, r, :]
                    urj, usj = ur[:, lanes(j)], us[:, lanes(j)]
                    ta = urj * kr + usj * ki
                    tb = usj * kr - urj * ki
                    acc_a = ta if acc_a is None else acc_a + ta
                    acc_b = tb if acc_b is None else acc_b + tb
                a_ref[r, lanes(i)] = acc_a.astype(BF16)
                bn_ref[r, lanes(i)] = acc_b.astype(BF16)

    def inverse(order, nyq, finish):
        nyq_term = []
        for i in range(nb):
            acc = None
            for j in range(nb):
                t = nyq[j] * kn_ref[order, i - j + nb - 1]
                acc = t if acc is None else acc + t
            nyq_term.append(acc)
        for tcn in range(per_block):
            r = slice(tcn * seq_chunk, (tcn + 1) * seq_chunk)
            y2 = jnp.dot(c_ref[r, :], a_ref[...], preferred_element_type=F32)
            y2 = y2 + jnp.dot(s_ref[r, :], bn_ref[...], preferred_element_type=F32)
            for i in range(nb):
                chunk = i * per_block + tcn
                rows = slice(chunk * seq_chunk, (chunk + 1) * seq_chunk)
                y = y2[:, lanes(i)] + alt * nyq_term[i]
                finish(chunk, rows, y + u_ref[rows, :] * hb_ref[order:order + 1, :])

    zero_row = jnp.zeros((1, tc), F32)
    nyq0 = [zero_row] * nb
    for chunk in range(nseq):
        put_signal(chunk, _sconv_chunk(v_ref, wv_ref, bv_ref, chunk, seq_chunk, nseq), nyq0)

    forward(0)

    nyq1 = [zero_row] * nb

    def finish_z(chunk, rows, y):
        put_signal(chunk, _sconv_chunk(x1_ref, wx1_ref, bx1_ref, chunk, seq_chunk, nseq) * y, nyq1)

    inverse(0, nyq0, finish_z)
    forward(1)

    def finish_out(chunk, rows, y):
        y = _sconv_chunk(x2_ref, wx2_ref, bx2_ref, chunk, seq_chunk, nseq) * y
        zg = zg_ref[rows, :]
        o_ref[rows, :] = (y * (zg * jax.nn.sigmoid(zg))).astype(o_ref.dtype)

    inverse(1, nyq1, finish_out)


def _hyena(proj3, conv_w, conv_b, hyena_bias, cmat, smat, kr, ki, kn, l, nb, tc=256,
           seq_chunk=512, freq_chunk=256):
    B, L, _ = proj3.shape
    b = L // nb
    nd = 2 * nb - 1
    width = hyena_bias.shape[2]
    nct = width // tc
    sig = lambda part: pl.BlockSpec((None, L, tc), lambda j, bb: (bb, 0, part * nct + j))
    cw = lambda part: pl.BlockSpec((None, 3, tc), lambda j, bb: (l, 0, part * nct + j))
    cb = lambda part: pl.BlockSpec((None, 1, tc), lambda j, bb: (l, 0, part * nct + j))
    kspec = pl.BlockSpec((2, nd, b, tc), lambda j, bb: (0, 0, 0, j), pipeline_mode=pl.Buffered(1))
    kern = functools.partial(_hyena_kernel, nb=nb, seq_chunk=seq_chunk, freq_chunk=freq_chunk)
    return pl.pallas_call(
        kern,
        out_shape=jax.ShapeDtypeStruct((B, L, width), BF16),
        grid=(nct, B),
        in_specs=[sig(0), sig(1), sig(2), sig(3),
                  cw(0), cw(1), cw(2), cb(0), cb(1), cb(2),
                  pl.BlockSpec((None, 2, tc), lambda j, bb: (l, 0, j)),
                  _const_spec((b, b)), _const_spec((b, b)),
                  kspec, kspec,
                  pl.BlockSpec((2, nd, 1, tc), lambda j, bb: (0, 0, 0, j))],
        out_specs=pl.BlockSpec((None, L, tc), lambda j, bb: (bb, 0, j)),
        scratch_shapes=[pltpu.VMEM((L, tc), F32), pltpu.VMEM((b, nb * tc), BF16),
                        pltpu.VMEM((b, nb * tc), BF16), pltpu.VMEM((b, nb * tc), BF16)],
        compiler_params=_params(("parallel", "arbitrary")),
        name="hyena_mixer",
    )(proj3, proj3, proj3, proj3, conv_w, conv_w, conv_w, conv_b, conv_b, conv_b,
      hyena_bias, cmat, smat, kr, ki, kn)


def _rope(x, cos, sin_lo, sin_hi):
    return (x * cos + pltpu.roll(x, HEAD_DIM - ROPE_HALF, 1) * sin_lo
            + pltpu.roll(x, ROPE_HALF, 1) * sin_hi)


def _attn_kernel(sink_ref, q_ref, k_ref, v_ref, zg_ref, cq_ref, slq_ref, shq_ref,
                 ck_ref, slk_ref, shk_ref, o_ref, *, layer):
    L = k_ref.shape[0]
    qb = q_ref.shape[0]
    kw = qb + 2 * WINDOW
    kvh = pl.program_id(1)
    n = pl.program_id(2)
    start = pl.multiple_of(jnp.clip(n * qb - WINDOW, 0, L - kw), WINDOW)
    win = pl.ds(start, kw)
    kb = _rope(k_ref[win, :], ck_ref[win, :], slk_ref[win, :], shk_ref[win, :]).astype(BF16)
    vb = v_ref[win, :].astype(BF16)
    qpos = n * qb + lax.broadcasted_iota(jnp.int32, (qb, 1), 0)
    kpos = start + lax.broadcasted_iota(jnp.int32, (1, kw), 1)
    valid = jnp.abs(kpos - qpos) <= WINDOW
    cq, slq, shq = cq_ref[...], slq_ref[...], shq_ref[...]
    q = jnp.concatenate(
        [_rope(q_ref[:, g * HEAD_DIM:(g + 1) * HEAD_DIM], cq, slq, shq).astype(BF16) for g in range(GROUP)],
        axis=0)
    s = lax.dot_general(q, kb, (((1,), (1,)), ((), ())), preferred_element_type=F32) * (HEAD_DIM ** -0.5)
    s = jnp.where(valid[None], s.reshape(GROUP, qb, kw), -jnp.inf)
    head = lax.broadcasted_iota(jnp.int32, (GROUP, 1, 1), 0)
    sk = jnp.zeros((GROUP, 1, 1), F32)
    for g in range(GROUP):
        sk = jnp.where(head == g, sink_ref[layer, kvh * GROUP + g], sk)
    m = jnp.maximum(jnp.max(s, axis=-1, keepdims=True), sk)
    p = jnp.exp(s - m)
    p = p / (jnp.sum(p, axis=-1, keepdims=True) + jnp.exp(sk - m))
    o = jnp.dot(p.reshape(GROUP * qb, kw).astype(BF16), vb, preferred_element_type=F32)
    for g in range(GROUP):
        cols = slice(g * HEAD_DIM, (g + 1) * HEAD_DIM)
        zg = zg_ref[:, cols]
        o_ref[:, cols] = (o[g * qb:(g + 1) * qb] * (zg * jax.nn.sigmoid(zg))).astype(o_ref.dtype)


def _attention(proj3, sink, rope_tabs, col_q, col_k, col_v, col_zg, l, qb=256):
    B, L, _ = proj3.shape
    gw = GROUP * HEAD_DIM
    cos_t, sin_lo, sin_hi = rope_tabs
    qtab = pl.BlockSpec((qb, HEAD_DIM), lambda b, h, n: (n, 0))
    ktab = pl.BlockSpec((L, HEAD_DIM), lambda b, h, n: (0, 0))
    return pl.pallas_call(
        functools.partial(_attn_kernel, layer=l),
        out_shape=jax.ShapeDtypeStruct((B, L, N_HEADS * HEAD_DIM), BF16),
        grid=(B, N_KV_HEADS, L // qb),
        in_specs=[pl.BlockSpec(memory_space=pltpu.SMEM),
                  pl.BlockSpec((None, qb, gw), lambda b, h, n: (b, n, col_q // gw + h)),
                  pl.BlockSpec((None, L, HEAD_DIM), lambda b, h, n: (b, 0, col_k // HEAD_DIM + h)),
                  pl.BlockSpec((None, L, HEAD_DIM), lambda b, h, n: (b, 0, col_v // HEAD_DIM + h)),
                  pl.BlockSpec((None, qb, gw), lambda b, h, n: (b, n, col_zg // gw + h)),
                  qtab, qtab, qtab, ktab, ktab, ktab],
        out_specs=pl.BlockSpec((None, qb, gw), lambda b, h, n: (b, n, h)),
        compiler_params=_params(("parallel", "parallel", "arbitrary")),
        name="window_attention",
    )(sink, proj3, proj3, proj3, proj3, cos_t, sin_lo, sin_hi, cos_t, sin_lo, sin_hi)


def _merge_out_kernel(*refs, n_gate_blocks, final):
    x_ref, yh_ref, ya_ref = refs[0:3]
    gh_refs = refs[3:3 + n_gate_blocks]
    ga_refs = refs[3 + n_gate_blocks:3 + 2 * n_gate_blocks]
    who_ref, wao_ref, wout_ref, fg_ref, o_ref, m_ref = refs[3 + 2 * n_gate_blocks:]
    gw = gh_refs[0].shape[1]
    yh = yh_ref[...]
    ya = ya_ref[...]
    for c in range(n_gate_blocks):
        cols = slice(c * gw, (c + 1) * gw)
        ph = jnp.dot(yh, who_ref[:, cols], preferred_element_type=F32)
        pa = jnp.dot(ya, wao_ref[:, cols], preferred_element_type=F32)
        merged = jax.nn.sigmoid(gh_refs[c][...]) * ph + jax.nn.sigmoid(ga_refs[c][...]) * pa
        m_ref[:, cols] = merged.astype(BF16)
    out = x_ref[...] + jnp.dot(m_ref[...], wout_ref[...], preferred_element_type=F32)
    if final:
        ms = jnp.mean(out * out, axis=-1, keepdims=True)
        out = out * lax.rsqrt(ms + EPS) * fg_ref[...]
    o_ref[...] = out


def _merge_out(x2d, proj, yh, ya, who, wao, wout, final_g, col_gh, col_ga, l, final, tm=256, gw=512):
    m, d = x2d.shape
    width = yh.shape[1]
    ngb = d // gw
    gspec = lambda col0, c: pl.BlockSpec((tm, gw), lambda i: (i, col0 // gw + c))
    wspec = lambda rows: pl.BlockSpec((None, rows, d), lambda i: (l, 0, 0), pipeline_mode=pl.Buffered(1))
    kern = functools.partial(_merge_out_kernel, n_gate_blocks=ngb, final=final)
    return pl.pallas_call(
        kern,
        out_shape=jax.ShapeDtypeStruct((m, d), F32),
        grid=(m // tm,),
        in_specs=[pl.BlockSpec((tm, d), lambda i: (i, 0)),
                  pl.BlockSpec((tm, width), lambda i: (i, 0)),
                  pl.BlockSpec((tm, width), lambda i: (i, 0))]
                 + [gspec(col_gh, c) for c in range(ngb)]
                 + [gspec(col_ga, c) for c in range(ngb)]
                 + [wspec(width), wspec(width), wspec(d),
                    pl.BlockSpec((1, d), lambda i: (0, 0))],
        out_specs=pl.BlockSpec((tm, d), lambda i: (i, 0)),
        scratch_shapes=[pltpu.VMEM((tm, d), BF16)],
        compiler_params=_params(("parallel",)),
        name="merge_out",
    )(x2d, yh, ya, *([proj] * (2 * ngb)), who, wao, wout, final_g.reshape(1, d))


def _dft_mats(b):
    idx = jnp.arange(b, dtype=jnp.int32)
    k = (idx[:, None] * idx[None, :]) % (2 * b)
    ang = k.astype(F32) * (math.pi / b)
    return jnp.cos(ang).astype(BF16), jnp.sin(ang).astype(BF16)


def _rope_tabs(L):
    inv = ROPE_THETA ** (-jnp.arange(0, ROPE_DIM, 2, dtype=F32) / ROPE_DIM)
    ang = jnp.arange(L, dtype=F32)[:, None] * inv[None, :]
    cos, sin = jnp.cos(ang), jnp.sin(ang)
    ones = jnp.ones((L, HEAD_DIM - ROPE_DIM), F32)
    zeros = jnp.zeros((L, HEAD_DIM - ROPE_HALF), F32)
    cos_t = jnp.concatenate([cos, cos, ones], axis=1)
    sin_lo = jnp.concatenate([-sin, zeros], axis=1)
    sin_hi = jnp.concatenate([jnp.zeros((L, ROPE_HALF), F32), sin, zeros[:, ROPE_HALF:]], axis=1)
    return cos_t, sin_lo, sin_hi


def _filter_feats(L):
    t = jnp.linspace(0.0, 1.0, L, dtype=F32)[:, None]
    bands = jnp.linspace(1e-4, FILTER_BANDS - 1, FILTER_BANDS, dtype=F32)[None, :]
    ang = (2.0 * math.pi / L) * jnp.arange(L, dtype=F32)[:, None] * bands
    feats = jnp.concatenate([t, jnp.cos(ang), -jnp.sin(ang)], axis=-1)
    feats = jnp.pad(feats, ((0, 0), (0, FEAT_PAD - feats.shape[1])))
    return feats, t


def kernel(x, norm_g, w_in, conv_w, conv_b, filt_w1, filt_b1, filt_w2, filt_b2, filt_w3, filt_b3,
           filt_w4, filt_freq, hyena_bias, attn_sink, w_hyena_out, w_attn_out, w_out, final_norm):
    B, L, D = x.shape
    depth = norm_g.shape[0]
    hw = hyena_bias.shape[2]
    aw = N_HEADS * HEAD_DIM
    kvw = N_KV_HEADS * HEAD_DIM
    sizes = (3 * hw, hw, aw, kvw, kvw, aw, D, D)
    cols = [0]
    for s in sizes:
        cols.append(cols[-1] + s)
    col_zhy, col_q, col_k, col_v, col_zat, col_gh, col_ga = cols[1:8]

    cmat, smat = _dft_mats(L // NB)
    rope_tabs = _rope_tabs(L)
    feats, tcol = _filter_feats(L)
    deltas = jnp.abs(jnp.linspace(MIN_DECAY, MAX_DECAY, hw, dtype=F32))[None, :]

    w_in_b, who_b, wao_b, wout_b = (w.astype(BF16) for w in (w_in, w_hyena_out, w_attn_out, w_out))
    row = lambda a: a[:, None, :]
    w1p = jnp.pad(filt_w1, ((0, 0), (0, FEAT_PAD - filt_w1.shape[1]), (0, 0)))
    xf = x.reshape(B * L, D)
    for l in range(depth):
        proj = _norm_proj(xf, row(norm_g), w_in_b, l)
        proj3 = proj.reshape(B, L, -1)
        kr, ki, kn = _filters(feats, tcol, w1p, row(filt_b1), filt_w2, row(filt_b2), filt_w3, row(filt_b3),
                              row(filt_freq), filt_w4, deltas, cmat, smat, l, NB)
        y_hy = _hyena(proj3, conv_w, row(conv_b), hyena_bias, cmat, smat, kr, ki, kn, l, NB)
        y_at = _attention(proj3, attn_sink, rope_tabs, col_q, col_k, col_v, col_zat, l)
        xf = _merge_out(xf, proj, y_hy.reshape(B * L, hw), y_at.reshape(B * L, aw),
                        who_b, wao_b, wout_b, final_norm, col_gh, col_ga, l, final=(l == depth - 1))
    return xf.reshape(B, L, D)
```

```python
import functools
import math

import jax
import jax.numpy as jnp
from jax import lax
from jax.experimental import pallas as pl
from jax.experimental.pallas import tpu as pltpu

F32 = jnp.float32
BF16 = jnp.bfloat16

HEAD_DIM = 128
N_HEADS = 8
N_KV_HEADS = 2
GROUP = N_HEADS // N_KV_HEADS
WINDOW = 128
QSUB = 128
ROPE_THETA = 500000.0
ROPE_DIM = HEAD_DIM // 4
ROPE_HALF = ROPE_DIM // 2
EPS = 1e-6
FILTER_BANDS = 16
FEAT_PAD = 128
DECAY_TARGET = 1e-2
MIN_DECAY = math.log(DECAY_TARGET) / 0.3
MAX_DECAY = math.log(DECAY_TARGET) / 1.5

NB = 4
VMEM_LIMIT = 56 * 1024 * 1024


def _params(sem, vmem=VMEM_LIMIT):
    return pltpu.CompilerParams(dimension_semantics=sem, vmem_limit_bytes=vmem)


def _const_spec(shape):
    return pl.BlockSpec(shape, lambda *_: (0,) * len(shape), pipeline_mode=pl.Buffered(1))


def _alt_sign(rows):
    return jnp.where((lax.broadcasted_iota(jnp.int32, (rows, 1), 0) & 1) == 0, 1.0, -1.0)


def _norm_proj_kernel(x_ref, g_ref, w_ref, o_ref, h_ref):
    @pl.when(pl.program_id(1) == 0)
    def _():
        x = x_ref[...]
        ms = jnp.mean(x * x, axis=-1, keepdims=True)
        h_ref[...] = (x * lax.rsqrt(ms + EPS) * g_ref[...]).astype(BF16)

    o_ref[...] = jnp.dot(h_ref[...], w_ref[...], preferred_element_type=F32)


def _norm_proj(x2d, g_all, w_all, l, tm=1024, tn=1536):
    m, d = x2d.shape
    n = w_all.shape[2]
    return pl.pallas_call(
        _norm_proj_kernel,
        out_shape=jax.ShapeDtypeStruct((m, n), F32),
        grid=(m // tm, n // tn),
        in_specs=[
            pl.BlockSpec((tm, d), lambda i, j: (i, 0)),
            pl.BlockSpec((None, 1, d), lambda i, j: (l, 0, 0)),
            pl.BlockSpec((None, d, tn), lambda i, j: (l, 0, j)),
        ],
        out_specs=pl.BlockSpec((tm, tn), lambda i, j: (i, j)),
        scratch_shapes=[pltpu.VMEM((tm, d), BF16)],
        compiler_params=_params(("parallel", "arbitrary")),
        name="norm_proj",
    )(x2d, g_all, w_all)


def _filter_kernel(feats_ref, t_ref, w1_ref, b1_ref, w2_ref, b2_ref, w3_ref, b3_ref, fr_ref,
                   w4f_ref, w4b_ref, dl_ref, c_ref, s_ref,
                   kr_ref, ki_ref, kn_ref,
                   hdn_ref, xf_ref, xb_ref, cf_ref, sf_ref, cb_ref, sb_ref, *, nb, row_chunk):
    L = feats_ref.shape[0]
    b = L // nb
    tc = dl_ref.shape[1]
    hp = lax.Precision.HIGHEST
    nchunks = b // row_chunk
    alt = _alt_sign(row_chunk)

    @pl.when((pl.program_id(0) == 0) & (pl.program_id(1) == 0))
    def _():
        fr = fr_ref[...]
        h = jnp.sin(fr * (jnp.dot(feats_ref[...], w1_ref[...], precision=hp,
                                  preferred_element_type=F32) + b1_ref[...]))
        h = jnp.sin(fr * (jnp.dot(h, w2_ref[...], precision=hp,
                                  preferred_element_type=F32) + b2_ref[...]))
        h = jnp.sin(fr * (jnp.dot(h, w3_ref[...], precision=hp,
                                  preferred_element_type=F32) + b3_ref[...]))
        hdn_ref[...] = h

    def taps(rows):
        h = hdn_ref[rows, :]
        decay = jnp.exp(-t_ref[rows, :] * dl_ref[...])
        fwd = jnp.dot(h, w4f_ref[...], precision=hp, preferred_element_type=F32) * decay
        bwd = jnp.dot(h, w4b_ref[...], precision=hp, preferred_element_type=F32) * decay
        return fwd, bwd

    zero_row = jnp.zeros((1, tc), F32)
    f0, b0, af, ab = [], [], [], []
    for q in range(nb):
        head_f, head_b = taps(pl.ds(q * b, 8))
        f0.append(head_f[0:1, :])
        b0.append(head_b[0:1, :])

        def tap_chunk(i, carry, q=q):
            l0 = pl.multiple_of(i * row_chunk, row_chunk)
            fwd, bwd = taps(pl.ds(pl.multiple_of(q * b + l0, row_chunk), row_chunk))
            first = (lax.broadcasted_iota(jnp.int32, (row_chunk, 1), 0) + l0) == 0
            fz = jnp.where(first, 0.0, fwd)
            bz = jnp.where(first, 0.0, bwd)
            xf_ref[q, pl.ds(l0, row_chunk), :] = fz.astype(BF16)
            xb_ref[q, pl.ds(l0, row_chunk), :] = bz.astype(BF16)
            return (carry[0] + jnp.sum(fz * alt, axis=0, keepdims=True),
                    carry[1] + jnp.sum(bz * alt, axis=0, keepdims=True))

        a_f, a_b = lax.fori_loop(0, nchunks, tap_chunk, (zero_row, zero_row))
        af.append(a_f)
        ab.append(a_b)

        def spec_chunk(i, carry, q=q):
            r = pl.ds(pl.multiple_of(i * row_chunk, row_chunk), row_chunk)
            cf_ref[q, r, :] = jnp.dot(c_ref[r, :], xf_ref[q], preferred_element_type=F32)
            sf_ref[q, r, :] = jnp.dot(s_ref[r, :], xf_ref[q], preferred_element_type=F32)
            cb_ref[q, r, :] = jnp.dot(c_ref[r, :], xb_ref[q], preferred_element_type=F32)
            sb_ref[q, r, :] = jnp.dot(s_ref[r, :], xb_ref[q], preferred_element_type=F32)
            return carry

        lax.fori_loop(0, nchunks, spec_chunk, 0)

    inv_n = 1.0 / (2 * b)
    for d in range(-(nb - 1), nb):
        slot = d + nb - 1
        e = -d
        if d >= 1:
            kn = af[d] + f0[d] + af[d - 1]
        elif d == 0:
            kn = af[0] + f0[0] + ab[0]
        else:
            kn = b0[e] + ab[e] + ab[e - 1]
        kn_ref[slot] = kn * inv_n

        def combine(i, carry, d=d, e=e, slot=slot):
            r0 = pl.multiple_of(i * row_chunk, row_chunk)
            r = pl.ds(r0, row_chunk)
            first = (lax.broadcasted_iota(jnp.int32, (row_chunk, 1), 0) + r0) == 0
            wgt = jnp.where(first, inv_n, 2.0 * inv_n)
            if d >= 1:
                kr = cf_ref[d, r, :] + f0[d] + alt * cf_ref[d - 1, r, :]
                ki = -sf_ref[d, r, :] - alt * sf_ref[d - 1, r, :]
            elif d == 0:
                kr = cf_ref[0, r, :] + f0[0] + cb_ref[0, r, :]
                ki = sb_ref[0, r, :] - sf_ref[0, r, :]
            else:
                kr = b0[e] + cb_ref[e, r, :] + alt * cb_ref[e - 1, r, :]
                ki = sb_ref[e, r, :] + alt * sb_ref[e - 1, r, :]
            kr_ref[slot, r, :] = kr * wgt
            ki_ref[slot, r, :] = ki * wgt
            return carry

        lax.fori_loop(0, nchunks, combine, 0)


def _filters(feats, tcol, w1p, b1, w2, b2, w3, b3, freq, w4, deltas, cmat, smat, l, nb, tc=256,
             row_chunk=512):
    L = feats.shape[0]
    b = L // nb
    nd = 2 * nb - 1
    width = deltas.shape[1]
    nct = width // tc
    hid = w2.shape[1]
    small = lambda a: pl.BlockSpec(a.shape, lambda o, c: (0,) * a.ndim)
    layer = lambda a: pl.BlockSpec((None,) + a.shape[1:], lambda o, c: (l,) + (0,) * (a.ndim - 1))
    kern = functools.partial(_filter_kernel, nb=nb, row_chunk=row_chunk)
    kspec = pl.BlockSpec((None, nd, b, tc), lambda o, c: (o, 0, 0, c))
    blk = lambda dt: pltpu.VMEM((nb, b, tc), dt)
    return pl.pallas_call(
        kern,
        out_shape=(jax.ShapeDtypeStruct((2, nd, b, width), F32),
                   jax.ShapeDtypeStruct((2, nd, b, width), F32),
                   jax.ShapeDtypeStruct((2, nd, 1, width), F32)),
        grid=(2, nct),
        in_specs=[small(feats), small(tcol), layer(w1p), layer(b1), layer(w2), layer(b2),
                  layer(w3), layer(b3), layer(freq),
                  pl.BlockSpec((None, hid, tc), lambda o, c: (l, 0, o * 2 * nct + c)),
                  pl.BlockSpec((None, hid, tc), lambda o, c: (l, 0, o * 2 * nct + nct + c)),
                  pl.BlockSpec((1, tc), lambda o, c: (0, c)),
                  _const_spec((b, b)), _const_spec((b, b))],
        out_specs=(kspec, kspec, pl.BlockSpec((None, nd, 1, tc), lambda o, c: (o, 0, 0, c))),
        scratch_shapes=[pltpu.VMEM((L, hid), F32), blk(BF16), blk(BF16),
                        blk(F32), blk(F32), blk(F32), blk(F32)],
        compiler_params=_params(("arbitrary", "arbitrary")),
        name="hyena_filters",
    )(feats, tcol, w1p, b1, w2, b2, w3, b3, freq, w4, w4, deltas, cmat, smat)


def _sconv_chunk(u_ref, w_ref, b_ref, i, rows, nchunks):
    r0 = i * rows
    tc = u_ref.shape[1]
    u = u_ref[r0:r0 + rows, :]
    zero = jnp.zeros((1, tc), F32)
    up = u_ref[r0 - 8:r0, :][7:8, :] if i > 0 else zero
    dn = u_ref[r0 + rows:r0 + rows + 8, :][0:1, :] if i < nchunks - 1 else zero
    row = lax.broadcasted_iota(jnp.int32, (rows, 1), 0)
    prev = jnp.where(row == 0, up, pltpu.roll(u, 1, 0))
    nxt = jnp.where(row == rows - 1, dn, pltpu.roll(u, rows - 1, 0))
    return b_ref[...] + prev * w_ref[0:1, :] + u * w_ref[1:2, :] + nxt * w_ref[2:3, :]


def _hyena_kernel(v_ref, x1_ref, x2_ref, zg_ref, wv_ref, wx1_ref, wx2_ref, bv_ref, bx1_ref, bx2_ref,
                  hb_ref, c_ref, s_ref, kr_ref, ki_ref, kn_ref, o_ref,
                  u_ref, ub_ref, a_ref, bn_ref, *, nb, seq_chunk, freq_chunk):
    L, tc = u_ref.shape
    b = L // nb
    per_block = b // seq_chunk
    nseq = L // seq_chunk
    alt = _alt_sign(seq_chunk)

    def lanes(j):
        return slice(j * tc, (j + 1) * tc)

    def put_signal(chunk, val, nyq):
        j, local = divmod(chunk, per_block)
        rows = slice(chunk * seq_chunk, (chunk + 1) * seq_chunk)
        u_ref[rows, :] = val
        ub_ref[local * seq_chunk:(local + 1) * seq_chunk, lanes(j)] = val.astype(BF16)
        nyq[j] = nyq[j] + jnp.sum(val * alt, axis=0, keepdims=True)

    def forward(order):
        for fc in range(b // freq_chunk):
            r = slice(fc * freq_chunk, (fc + 1) * freq_chunk)
            ur = jnp.dot(c_ref[r, :], ub_ref[...], preferred_element_type=F32)
            us = jnp.dot(s_ref[r, :], ub_ref[...], preferred_element_type=F32)
            for i in range(nb):
                acc_a = acc_b = None
                for j in range(nb):
                    kr = kr_ref[order, i - j + nb - 1, r, :]
                    ki = ki_ref[order, i - j + nb - 1, r, :]
                    urj, usj = ur[:, lanes(j)], us[:, lanes(j)]
                    ta = urj * kr + usj * ki
                    tb = usj * kr - urj * ki
                    acc_a = ta if acc_a is None else acc_a + ta
                    acc_b = tb if acc_b is None else acc_b + tb
                a_ref[r, lanes(i)] = acc_a.astype(BF16)
                bn_ref[r, lanes(i)] = acc_b.astype(BF16)

    def inverse(order, nyq, finish):
        nyq_term = []
        for i in range(nb):
            acc = None
            for j in range(nb):
                t = nyq[j] * kn_ref[order, i - j + nb - 1]
                acc = t if acc is None else acc + t
            nyq_term.append(acc)
        for tcn in range(per_block):
            r = slice(tcn * seq_chunk, (tcn + 1) * seq_chunk)
            y2 = jnp.dot(c_ref[r, :], a_ref[...], preferred_element_type=F32)
            y2 = y2 + jnp.dot(s_ref[r, :], bn_ref[...], preferred_element_type=F32)
            for i in range(nb):
                chunk = i * per_block + tcn
                rows = slice(chunk * seq_chunk, (chunk + 1) * seq_chunk)
                y = y2[:, lanes(i)] + alt * nyq_term[i]
                finish(chunk, rows, y + u_ref[rows, :] * hb_ref[order:order + 1, :])

    zero_row = jnp.zeros((1, tc), F32)
    nyq0 = [zero_row] * nb
    for chunk in range(nseq):
        put_signal(chunk, _sconv_chunk(v_ref, wv_ref, bv_ref, chunk, seq_chunk, nseq), nyq0)

    forward(0)

    nyq1 = [zero_row] * nb

    def finish_z(chunk, rows, y):
        put_signal(chunk, _sconv_chunk(x1_ref, wx1_ref, bx1_ref, chunk, seq_chunk, nseq) * y, nyq1)

    inverse(0, nyq0, finish_z)
    forward(1)

    def finish_out(chunk, rows, y):
        y = _sconv_chunk(x2_ref, wx2_ref, bx2_ref, chunk, seq_chunk, nseq) * y
        zg = zg_ref[rows, :]
        o_ref[rows, :] = (y * (zg * jax.nn.sigmoid(zg))).astype(o_ref.dtype)

    inverse(1, nyq1, finish_out)


def _hyena(proj3, conv_w, conv_b, hyena_bias, cmat, smat, kr, ki, kn, l, nb, tc=256,
           seq_chunk=512, freq_chunk=256):
    B, L, _ = proj3.shape
    b = L // nb
    nd = 2 * nb - 1
    width = hyena_bias.shape[2]
    nct = width // tc
    sig = lambda part: pl.BlockSpec((None, L, tc), lambda j, bb: (bb, 0, part * nct + j))
    cw = lambda part: pl.BlockSpec((None, 3, tc), lambda j, bb: (l, 0, part * nct + j))
    cb = lambda part: pl.BlockSpec((None, 1, tc), lambda j, bb: (l, 0, part * nct + j))
    kspec = pl.BlockSpec((2, nd, b, tc), lambda j, bb: (0, 0, 0, j), pipeline_mode=pl.Buffered(1))
    kern = functools.partial(_hyena_kernel, nb=nb, seq_chunk=seq_chunk, freq_chunk=freq_chunk)
    return pl.pallas_call(
        kern,
        out_shape=jax.ShapeDtypeStruct((B, L, width), BF16),
        grid=(nct, B),
        in_specs=[sig(0), sig(1), sig(2), sig(3),
                  cw(0), cw(1), cw(2), cb(0), cb(1), cb(2),
                  pl.BlockSpec((None, 2, tc), lambda j, bb: (l, 0, j)),
                  _const_spec((b, b)), _const_spec((b, b)),
                  kspec, kspec,
                  pl.BlockSpec((2, nd, 1, tc), lambda j, bb: (0, 0, 0, j))],
        out_specs=pl.BlockSpec((None, L, tc), lambda j, bb: (bb, 0, j)),
        scratch_shapes=[pltpu.VMEM((L, tc), F32), pltpu.VMEM((b, nb * tc), BF16),
                        pltpu.VMEM((b, nb * tc), BF16), pltpu.VMEM((b, nb * tc), BF16)],
        compiler_params=_params(("parallel", "arbitrary")),
        name="hyena_mixer",
    )(proj3, proj3, proj3, proj3, conv_w, conv_w, conv_w, conv_b, conv_b, conv_b,
      hyena_bias, cmat, smat, kr, ki, kn)


def _rope(x, cos, sin_lo, sin_hi):
    return (x * cos + pltpu.roll(x, HEAD_DIM - ROPE_HALF, 1) * sin_lo
            + pltpu.roll(x, ROPE_HALF, 1) * sin_hi)


def _attn_kernel(sink_ref, q_ref, k_ref, v_ref, zg_ref, cos_ref, slo_ref, shi_ref, o_ref,
                 kb_ref, vb_ref, *, layer):
    L = k_ref.shape[0]
    qb = q_ref.shape[0]
    kw = QSUB + 2 * WINDOW
    kvh = pl.program_id(1)
    n = pl.program_id(2)

    @pl.when(n == 0)
    def _():
        kb_ref[...] = _rope(k_ref[...], cos_ref[...], slo_ref[...], shi_ref[...]).astype(BF16)
        vb_ref[...] = v_ref[...].astype(BF16)

    head = lax.broadcasted_iota(jnp.int32, (GROUP, 1, 1), 0)
    sk = jnp.zeros((GROUP, 1, 1), F32)
    for g in range(GROUP):
        sk = jnp.where(head == g, sink_ref[layer, kvh * GROUP + g], sk)

    for sb in range(qb // QSUB):
        rows = slice(sb * QSUB, (sb + 1) * QSUB)
        q0 = pl.multiple_of(n * qb + sb * QSUB, QSUB)
        start = pl.multiple_of(jnp.clip(q0 - WINDOW, 0, L - kw), WINDOW)
        win = pl.ds(start, kw)
        tab = pl.ds(q0, QSUB)
        cq, slq, shq = cos_ref[tab, :], slo_ref[tab, :], shi_ref[tab, :]
        q = jnp.concatenate(
            [_rope(q_ref[rows, g * HEAD_DIM:(g + 1) * HEAD_DIM], cq, slq, shq).astype(BF16)
             for g in range(GROUP)], axis=0)
        s = lax.dot_general(q, kb_ref[win, :], (((1,), (1,)), ((), ())),
                            preferred_element_type=F32) * (HEAD_DIM ** -0.5)
        qpos = q0 + lax.broadcasted_iota(jnp.int32, (QSUB, 1), 0)
        kpos = start + lax.broadcasted_iota(jnp.int32, (1, kw), 1)
        valid = jnp.abs(kpos - qpos) <= WINDOW
        s = jnp.where(valid[None], s.reshape(GROUP, QSUB, kw), -jnp.inf)
        m = jnp.maximum(jnp.max(s, axis=-1, keepdims=True), sk)
        p = jnp.exp(s - m)
        denom = jnp.sum(p, axis=-1, keepdims=True) + jnp.exp(sk - m)
        o = jnp.dot(p.reshape(GROUP * QSUB, kw).astype(BF16), vb_ref[win, :], preferred_element_type=F32)
        o = o.reshape(GROUP, QSUB, HEAD_DIM) / denom
        for g in range(GROUP):
            cols = slice(g * HEAD_DIM, (g + 1) * HEAD_DIM)
            zg = zg_ref[rows, cols]
            o_ref[rows, cols] = (o[g] * (zg * jax.nn.sigmoid(zg))).astype(o_ref.dtype)


def _attention(proj3, sink, rope_tabs, col_q, col_k, col_v, col_zg, l, qb=512):
    B, L, _ = proj3.shape
    gw = GROUP * HEAD_DIM
    cos_t, sin_lo, sin_hi = rope_tabs
    tab = pl.BlockSpec((L, HEAD_DIM), lambda b, h, n: (0, 0))
    return pl.pallas_call(
        functools.partial(_attn_kernel, layer=l),
        out_shape=jax.ShapeDtypeStruct((B, L, N_HEADS * HEAD_DIM), BF16),
        grid=(B, N_KV_HEADS, L // qb),
        in_specs=[pl.BlockSpec(memory_space=pltpu.SMEM),
                  pl.BlockSpec((None, qb, gw), lambda b, h, n: (b, n, col_q // gw + h)),
                  pl.BlockSpec((None, L, HEAD_DIM), lambda b, h, n: (b, 0, col_k // HEAD_DIM + h)),
                  pl.BlockSpec((None, L, HEAD_DIM), lambda b, h, n: (b, 0, col_v // HEAD_DIM + h)),
                  pl.BlockSpec((None, qb, gw), lambda b, h, n: (b, n, col_zg // gw + h)),
                  tab, tab, tab],
        out_specs=pl.BlockSpec((None, qb, gw), lambda b, h, n: (b, n, h)),
        scratch_shapes=[pltpu.VMEM((L, HEAD_DIM), BF16), pltpu.VMEM((L, HEAD_DIM), BF16)],
        compiler_params=_params(("parallel", "parallel", "arbitrary")),
        name="window_attention",
    )(sink, proj3, proj3, proj3, proj3, cos_t, sin_lo, sin_hi)


def _merge_out_kernel(*refs, n_gate_blocks, final):
    x_ref, yh_ref, ya_ref = refs[0:3]
    gh_refs = refs[3:3 + n_gate_blocks]
    ga_refs = refs[3 + n_gate_blocks:3 + 2 * n_gate_blocks]
    who_ref, wao_ref, wout_ref, fg_ref, o_ref, m_ref = refs[3 + 2 * n_gate_blocks:]
    gw = gh_refs[0].shape[1]
    yh = yh_ref[...]
    ya = ya_ref[...]
    for c in range(n_gate_blocks):
        cols = slice(c * gw, (c + 1) * gw)
        ph = jnp.dot(yh, who_ref[:, cols], preferred_element_type=F32)
        pa = jnp.dot(ya, wao_ref[:, cols], preferred_element_type=F32)
        merged = jax.nn.sigmoid(gh_refs[c][...]) * ph + jax.nn.sigmoid(ga_refs[c][...]) * pa
        m_ref[:, cols] = merged.astype(BF16)
    out = x_ref[...] + jnp.dot(m_ref[...], wout_ref[...], preferred_element_type=F32)
    if final:
        ms = jnp.mean(out * out, axis=-1, keepdims=True)
        out = out * lax.rsqrt(ms + EPS) * fg_ref[...]
    o_ref[...] = out


def _merge_out(x2d, proj, yh, ya, who, wao, wout, final_g, col_gh, col_ga, l, final, tm=256, gw=512):
    m, d = x2d.shape
    width = yh.shape[1]
    ngb = d // gw
    gspec = lambda col0, c: pl.BlockSpec((tm, gw), lambda i: (i, col0 // gw + c))
    wspec = lambda rows: pl.BlockSpec((None, rows, d), lambda i: (l, 0, 0), pipeline_mode=pl.Buffered(1))
    kern = functools.partial(_merge_out_kernel, n_gate_blocks=ngb, final=final)
    return pl.pallas_call(
        kern,
        out_shape=jax.ShapeDtypeStruct((m, d), F32),
        grid=(m // tm,),
        in_specs=[pl.BlockSpec((tm, d), lambda i: (i, 0)),
                  pl.BlockSpec((tm, width), lambda i: (i, 0)),
                  pl.BlockSpec((tm, width), lambda i: (i, 0))]
                 + [gspec(col_gh, c) for c in range(ngb)]
                 + [gspec(col_ga, c) for c in range(ngb)]
                 + [wspec(width), wspec(width), wspec(d),
                    pl.BlockSpec((1, d), lambda i: (0, 0))],
        out_specs=pl.BlockSpec((tm, d), lambda i: (i, 0)),
        scratch_shapes=[pltpu.VMEM((tm, d), BF16)],
        compiler_params=_params(("parallel",)),
        name="merge_out",
    )(x2d, yh, ya, *([proj] * (2 * ngb)), who, wao, wout, final_g.reshape(1, d))


def _dft_mats(b):
    idx = jnp.arange(b, dtype=jnp.int32)
    k = (idx[:, None] * idx[None, :]) % (2 * b)
    ang = k.astype(F32) * (math.pi / b)
    return jnp.cos(ang).astype(BF16), jnp.sin(ang).astype(BF16)


def _rope_tabs(L):
    inv = ROPE_THETA ** (-jnp.arange(0, ROPE_DIM, 2, dtype=F32) / ROPE_DIM)
    ang = jnp.arange(L, dtype=F32)[:, None] * inv[None, :]
    cos, sin = jnp.cos(ang), jnp.sin(ang)
    ones = jnp.ones((L, HEAD_DIM - ROPE_DIM), F32)
    zeros = jnp.zeros((L, HEAD_DIM - ROPE_HALF), F32)
    cos_t = jnp.concatenate([cos, cos, ones], axis=1)
    sin_lo = jnp.concatenate([-sin, zeros], axis=1)
    sin_hi = jnp.concatenate([jnp.zeros((L, ROPE_HALF), F32), sin, zeros[:, ROPE_HALF:]], axis=1)
    return cos_t, sin_lo, sin_hi


def _filter_feats(L):
    t = jnp.linspace(0.0, 1.0, L, dtype=F32)[:, None]
    bands = jnp.linspace(1e-4, FILTER_BANDS - 1, FILTER_BANDS, dtype=F32)[None, :]
    ang = (2.0 * math.pi / L) * jnp.arange(L, dtype=F32)[:, None] * bands
    feats = jnp.concatenate([t, jnp.cos(ang), -jnp.sin(ang)], axis=-1)
    feats = jnp.pad(feats, ((0, 0), (0, FEAT_PAD - feats.shape[1])))
    return feats, t


def kernel(x, norm_g, w_in, conv_w, conv_b, filt_w1, filt_b1, filt_w2, filt_b2, filt_w3, filt_b3,
           filt_w4, filt_freq, hyena_bias, attn_sink, w_hyena_out, w_attn_out, w_out, final_norm):
    B, L, D = x.shape
    depth = norm_g.shape[0]
    hw = hyena_bias.shape[2]
    aw = N_HEADS * HEAD_DIM
    kvw = N_KV_HEADS * HEAD_DIM
    sizes = (3 * hw, hw, aw, kvw, kvw, aw, D, D)
    cols = [0]
    for s in sizes:
        cols.append(cols[-1] + s)
    col_zhy, col_q, col_k, col_v, col_zat, col_gh, col_ga = cols[1:8]

    cmat, smat = _dft_mats(L // NB)
    rope_tabs = _rope_tabs(L)
    feats, tcol = _filter_feats(L)
    deltas = jnp.abs(jnp.linspace(MIN_DECAY, MAX_DECAY, hw, dtype=F32))[None, :]

    w_in_b, who_b, wao_b, wout_b = (w.astype(BF16) for w in (w_in, w_hyena_out, w_attn_out, w_out))
    row = lambda a: a[:, None, :]
    w1p = jnp.pad(filt_w1, ((0, 0), (0, FEAT_PAD - filt_w1.shape[1]), (0, 0)))
    xf = x.reshape(B * L, D)
    for l in range(depth):
        proj = _norm_proj(xf, row(norm_g), w_in_b, l)
        proj3 = proj.reshape(B, L, -1)
        kr, ki, kn = _filters(feats, tcol, w1p, row(filt_b1), filt_w2, row(filt_b2), filt_w3, row(filt_b3),
                              row(filt_freq), filt_w4, deltas, cmat, smat, l, NB)
        y_hy = _hyena(proj3, conv_w, row(conv_b), hyena_bias, cmat, smat, kr, ki, kn, l, NB)
        y_at = _attention(proj3, attn_sink, rope_tabs, col_q, col_k, col_v, col_zat, l)
        xf = _merge_out(xf, proj, y_hy.reshape(B * L, hw), y_at.reshape(B * L, aw),
                        who_b, wao_b, wout_b, final_norm, col_gh, col_ga, l, final=(l == depth - 1))
    return xf.reshape(B, L, D)
```

```python
import functools
import math

import jax
import jax.numpy as jnp
from jax import lax
from jax.experimental import pallas as pl
from jax.experimental.pallas import tpu as pltpu

F32 = jnp.float32
BF16 = jnp.bfloat16

HEAD_DIM = 128
N_HEADS = 8
N_KV_HEADS = 2
GROUP = N_HEADS // N_KV_HEADS
WINDOW = 128
QSUB = 128
ROPE_THETA = 500000.0
ROPE_DIM = HEAD_DIM // 4
ROPE_HALF = ROPE_DIM // 2
EPS = 1e-6
FILTER_BANDS = 16
FEAT_PAD = 128
DECAY_TARGET = 1e-2
MIN_DECAY = math.log(DECAY_TARGET) / 0.3
MAX_DECAY = math.log(DECAY_TARGET) / 1.5

NB = 4
VMEM_LIMIT = 56 * 1024 * 1024


def _params(sem, vmem=VMEM_LIMIT):
    return pltpu.CompilerParams(dimension_semantics=sem, vmem_limit_bytes=vmem)


def _const_spec(shape):
    return pl.BlockSpec(shape, lambda *_: (0,) * len(shape), pipeline_mode=pl.Buffered(1))


def _alt_sign(rows):
    return jnp.where((lax.broadcasted_iota(jnp.int32, (rows, 1), 0) & 1) == 0, 1.0, -1.0)


def _norm_proj_kernel(x_ref, g_ref, w_ref, o_ref, h_ref):
    @pl.when(pl.program_id(1) == 0)
    def _():
        x = x_ref[...]
        ms = jnp.mean(x * x, axis=-1, keepdims=True)
        h_ref[...] = (x * lax.rsqrt(ms + EPS) * g_ref[...]).astype(BF16)

    o_ref[...] = jnp.dot(h_ref[...], w_ref[...], preferred_element_type=F32)


def _norm_proj(x2d, g_all, w_all, l, tm=1024, tn=1536):
    m, d = x2d.shape
    n = w_all.shape[2]
    return pl.pallas_call(
        _norm_proj_kernel,
        out_shape=jax.ShapeDtypeStruct((m, n), F32),
        grid=(m // tm, n // tn),
        in_specs=[
            pl.BlockSpec((tm, d), lambda i, j: (i, 0)),
            pl.BlockSpec((None, 1, d), lambda i, j: (l, 0, 0)),
            pl.BlockSpec((None, d, tn), lambda i, j: (l, 0, j)),
        ],
        out_specs=pl.BlockSpec((tm, tn), lambda i, j: (i, j)),
        scratch_shapes=[pltpu.VMEM((tm, d), BF16)],
        compiler_params=_params(("parallel", "arbitrary")),
        name="norm_proj",
    )(x2d, g_all, w_all)


def _filter_kernel(feats_ref, t_ref, w1_ref, b1_ref, w2_ref, b2_ref, w3_ref, b3_ref, fr_ref,
                   w4f_ref, w4b_ref, dl_ref, c_ref, s_ref,
                   kr_ref, ki_ref, kn_ref,
                   hdn_ref, xf_ref, xb_ref, cf_ref, sf_ref, cb_ref, sb_ref, *, nb, row_chunk):
    L = feats_ref.shape[0]
    b = L // nb
    tc = dl_ref.shape[1]
    hp = lax.Precision.HIGHEST
    nchunks = b // row_chunk
    alt = _alt_sign(row_chunk)

    @pl.when((pl.program_id(0) == 0) & (pl.program_id(1) == 0))
    def _():
        fr = fr_ref[...]
        h = jnp.sin(fr * (jnp.dot(feats_ref[...], w1_ref[...], precision=hp,
                                  preferred_element_type=F32) + b1_ref[...]))
        h = jnp.sin(fr * (jnp.dot(h, w2_ref[...], precision=hp,
                                  preferred_element_type=F32) + b2_ref[...]))
        h = jnp.sin(fr * (jnp.dot(h, w3_ref[...], precision=hp,
                                  preferred_element_type=F32) + b3_ref[...]))
        hdn_ref[...] = h

    def taps(rows):
        h = hdn_ref[rows, :]
        decay = jnp.exp(-t_ref[rows, :] * dl_ref[...])
        fwd = jnp.dot(h, w4f_ref[...], precision=hp, preferred_element_type=F32) * decay
        bwd = jnp.dot(h, w4b_ref[...], precision=hp, preferred_element_type=F32) * decay
        return fwd, bwd

    zero_row = jnp.zeros((1, tc), F32)
    f0, b0, af, ab = [], [], [], []
    for q in range(nb):
        head_f, head_b = taps(pl.ds(q * b, 8))
        f0.append(head_f[0:1, :])
        b0.append(head_b[0:1, :])

        def tap_chunk(i, carry, q=q):
            l0 = pl.multiple_of(i * row_chunk, row_chunk)
            fwd, bwd = taps(pl.ds(pl.multiple_of(q * b + l0, row_chunk), row_chunk))
            first = (lax.broadcasted_iota(jnp.int32, (row_chunk, 1), 0) + l0) == 0
            fz = jnp.where(first, 0.0, fwd)
            bz = jnp.where(first, 0.0, bwd)
            xf_ref[q, pl.ds(l0, row_chunk), :] = fz.astype(BF16)
            xb_ref[q, pl.ds(l0, row_chunk), :] = bz.astype(BF16)
            return (carry[0] + jnp.sum(fz * alt, axis=0, keepdims=True),
                    carry[1] + jnp.sum(bz * alt, axis=0, keepdims=True))

        a_f, a_b = lax.fori_loop(0, nchunks, tap_chunk, (zero_row, zero_row))
        af.append(a_f)
        ab.append(a_b)

        def spec_chunk(i, carry, q=q):
            r = pl.ds(pl.multiple_of(i * row_chunk, row_chunk), row_chunk)
            cf_ref[q, r, :] = jnp.dot(c_ref[r, :], xf_ref[q], preferred_element_type=F32)
            sf_ref[q, r, :] = jnp.dot(s_ref[r, :], xf_ref[q], preferred_element_type=F32)
            cb_ref[q, r, :] = jnp.dot(c_ref[r, :], xb_ref[q], preferred_element_type=F32)
            sb_ref[q, r, :] = jnp.dot(s_ref[r, :], xb_ref[q], preferred_element_type=F32)
            return carry

        lax.fori_loop(0, nchunks, spec_chunk, 0)

    inv_n = 1.0 / (2 * b)
    for d in range(-(nb - 1), nb):
        slot = d + nb - 1
        e = -d
        if d >= 1:
            kn = af[d] + f0[d] + af[d - 1]
        elif d == 0:
            kn = af[0] + f0[0] + ab[0]
        else:
            kn = b0[e] + ab[e] + ab[e - 1]
        kn_ref[slot] = kn * inv_n

        def combine(i, carry, d=d, e=e, slot=slot):
            r0 = pl.multiple_of(i * row_chunk, row_chunk)
            r = pl.ds(r0, row_chunk)
            first = (lax.broadcasted_iota(jnp.int32, (row_chunk, 1), 0) + r0) == 0
            wgt = jnp.where(first, inv_n, 2.0 * inv_n)
            if d >= 1:
                kr = cf_ref[d, r, :] + f0[d] + alt * cf_ref[d - 1, r, :]
                ki = -sf_ref[d, r, :] - alt * sf_ref[d - 1, r, :]
            elif d == 0:
                kr = cf_ref[0, r, :] + f0[0] + cb_ref[0, r, :]
                ki = sb_ref[0, r, :] - sf_ref[0, r, :]
            else:
                kr = b0[e] + cb_ref[e, r, :] + alt * cb_ref[e - 1, r, :]
                ki = sb_ref[e, r, :] + alt * sb_ref[e - 1, r, :]
            kr_ref[slot, r, :] = kr * wgt
            ki_ref[slot, r, :] = ki * wgt
            return carry

        lax.fori_loop(0, nchunks, combine, 0)


def _filters(feats, tcol, w1p, b1, w2, b2, w3, b3, freq, w4, deltas, cmat, smat, l, nb, tc=256,
             row_chunk=512):
    L = feats.shape[0]
    b = L // nb
    nd = 2 * nb - 1
    width = deltas.shape[1]
    nct = width // tc
    hid = w2.shape[1]
    small = lambda a: pl.BlockSpec(a.shape, lambda o, c: (0,) * a.ndim)
    layer = lambda a: pl.BlockSpec((None,) + a.shape[1:], lambda o, c: (l,) + (0,) * (a.ndim - 1))
    kern = functools.partial(_filter_kernel, nb=nb, row_chunk=row_chunk)
    kspec = pl.BlockSpec((None, nd, b, tc), lambda o, c: (o, 0, 0, c))
    blk = lambda dt: pltpu.VMEM((nb, b, tc), dt)
    return pl.pallas_call(
        kern,
        out_shape=(jax.ShapeDtypeStruct((2, nd, b, width), F32),
                   jax.ShapeDtypeStruct((2, nd, b, width), F32),
                   jax.ShapeDtypeStruct((2, nd, 1, width), F32)),
        grid=(2, nct),
        in_specs=[small(feats), small(tcol), layer(w1p), layer(b1), layer(w2), layer(b2),
                  layer(w3), layer(b3), layer(freq),
                  pl.BlockSpec((None, hid, tc), lambda o, c: (l, 0, o * 2 * nct + c)),
                  pl.BlockSpec((None, hid, tc), lambda o, c: (l, 0, o * 2 * nct + nct + c)),
                  pl.BlockSpec((1, tc), lambda o, c: (0, c)),
                  _const_spec((b, b)), _const_spec((b, b))],
        out_specs=(kspec, kspec, pl.BlockSpec((None, nd, 1, tc), lambda o, c: (o, 0, 0, c))),
        scratch_shapes=[pltpu.VMEM((L, hid), F32), blk(BF16), blk(BF16),
                        blk(F32), blk(F32), blk(F32), blk(F32)],
        compiler_params=_params(("arbitrary", "arbitrary")),
        name="hyena_filters",
    )(feats, tcol, w1p, b1, w2, b2, w3, b3, freq, w4, w4, deltas, cmat, smat)


def _sconv_chunk(u_ref, w_ref, b_ref, i, rows, nchunks):
    r0 = i * rows
    tc = u_ref.shape[1]
    u = u_ref[r0:r0 + rows, :]
    zero = jnp.zeros((1, tc), F32)
    up = u_ref[r0 - 8:r0, :][7:8, :] if i > 0 else zero
    dn = u_ref[r0 + rows:r0 + rows + 8, :][0:1, :] if i < nchunks - 1 else zero
    row = lax.broadcasted_iota(jnp.int32, (rows, 1), 0)
    prev = jnp.where(row == 0, up, pltpu.roll(u, 1, 0))
    nxt = jnp.where(row == rows - 1, dn, pltpu.roll(u, rows - 1, 0))
    return b_ref[...] + prev * w_ref[0:1, :] + u * w_ref[1:2, :] + nxt * w_ref[2:3, :]


def _hyena_kernel(v_ref, x1_ref, x2_ref, zg_ref, wv_ref, wx1_ref, wx2_ref, bv_ref, bx1_ref, bx2_ref,
                  hb_ref, c_ref, s_ref, kr_ref, ki_ref, kn_ref, o_ref,
                  u_ref, ub_ref, a_ref, bn_ref, *, nb, seq_chunk, freq_chunk):
    L, tc = u_ref.shape
    b = L // nb
    per_block = b // seq_chunk
    nseq = L // seq_chunk
    alt = _alt_sign(seq_chunk)

    def lanes(j):
        return slice(j * tc, (j + 1) * tc)

    def put_signal(chunk, val, nyq):
        j, local = divmod(chunk, per_block)
        rows = slice(chunk * seq_chunk, (chunk + 1) * seq_chunk)
        u_ref[rows, :] = val
        ub_ref[local * seq_chunk:(local + 1) * seq_chunk, lanes(j)] = val.astype(BF16)
        nyq[j] = nyq[j] + jnp.sum(val * alt, axis=0, keepdims=True)

    def forward(order):
        for fc in range(b // freq_chunk):
            r = slice(fc * freq_chunk, (fc + 1) * freq_chunk)
            ur = jnp.dot(c_ref[r, :], ub_ref[...], preferred_element_type=F32)
            us = jnp.dot(s_ref[r, :], ub_ref[...], preferred_element_type=F32)
            for i in range(nb):
                acc_a = acc_b = None
                for j in range(nb):
                    kr = kr_ref[order, i - j + nb - 1, r, :]
                    ki = ki_ref[order, i - j + nb - 1, r, :]
                    urj, usj = ur[:, lanes(j)], us[:, lanes(j)]
                    ta = urj * kr + usj * ki
                    tb = usj * kr - urj * ki
                    acc_a = ta if acc_a is None else acc_a + ta
                    acc_b = tb if acc_b is None else acc_b + tb
                a_ref[r, lanes(i)] = acc_a.astype(BF16)
                bn_ref[r, lanes(i)] = acc_b.astype(BF16)

    def inverse(order, nyq, finish):
        nyq_term = []
        for i in range(nb):
            acc = None
            for j in range(nb):
                t = nyq[j] * kn_ref[order, i - j + nb - 1]
                acc = t if acc is None else acc + t
            nyq_term.append(acc)
        for tcn in range(per_block):
            r = slice(tcn * seq_chunk, (tcn + 1) * seq_chunk)
            y2 = jnp.dot(c_ref[r, :], a_ref[...], preferred_element_type=F32)
            y2 = y2 + jnp.dot(s_ref[r, :], bn_ref[...], preferred_element_type=F32)
            for i in range(nb):
                chunk = i * per_block + tcn
                rows = slice(chunk * seq_chunk, (chunk + 1) * seq_chunk)
                y = y2[:, lanes(i)] + alt * nyq_term[i]
                finish(chunk, rows, y + u_ref[rows, :] * hb_ref[order:order + 1, :])

    zero_row = jnp.zeros((1, tc), F32)
    nyq0 = [zero_row] * nb
    for chunk in range(nseq):
        put_signal(chunk, _sconv_chunk(v_ref, wv_ref, bv_ref, chunk, seq_chunk, nseq), nyq0)

    forward(0)

    nyq1 = [zero_row] * nb

    def finish_z(chunk, rows, y):
        put_signal(chunk, _sconv_chunk(x1_ref, wx1_ref, bx1_ref, chunk, seq_chunk, nseq) * y, nyq1)

    inverse(0, nyq0, finish_z)
    forward(1)

    def finish_out(chunk, rows, y):
        y = _sconv_chunk(x2_ref, wx2_ref, bx2_ref, chunk, seq_chunk, nseq) * y
        zg = zg_ref[rows, :]
        o_ref[rows, :] = (y * (zg * jax.nn.sigmoid(zg))).astype(o_ref.dtype)

    inverse(1, nyq1, finish_out)


def _hyena(proj3, conv_w, conv_b, hyena_bias, cmat, smat, kr, ki, kn, l, nb, tc=256,
           seq_chunk=512, freq_chunk=512):
    B, L, _ = proj3.shape
    b = L // nb
    nd = 2 * nb - 1
    width = hyena_bias.shape[2]
    nct = width // tc
    sig = lambda part: pl.BlockSpec((None, L, tc), lambda j, bb: (bb, 0, part * nct + j))
    cw = lambda part: pl.BlockSpec((None, 3, tc), lambda j, bb: (l, 0, part * nct + j))
    cb = lambda part: pl.BlockSpec((None, 1, tc), lambda j, bb: (l, 0, part * nct + j))
    kspec = pl.BlockSpec((2, nd, b, tc), lambda j, bb: (0, 0, 0, j), pipeline_mode=pl.Buffered(1))
    kern = functools.partial(_hyena_kernel, nb=nb, seq_chunk=seq_chunk, freq_chunk=freq_chunk)
    return pl.pallas_call(
        kern,
        out_shape=jax.ShapeDtypeStruct((B, L, width), BF16),
        grid=(nct, B),
        in_specs=[sig(0), sig(1), sig(2), sig(3),
                  cw(0), cw(1), cw(2), cb(0), cb(1), cb(2),
                  pl.BlockSpec((None, 2, tc), lambda j, bb: (l, 0, j)),
                  _const_spec((b, b)), _const_spec((b, b)),
                  kspec, kspec,
                  pl.BlockSpec((2, nd, 1, tc), lambda j, bb: (0, 0, 0, j))],
        out_specs=pl.BlockSpec((None, L, tc), lambda j, bb: (bb, 0, j)),
        scratch_shapes=[pltpu.VMEM((L, tc), F32), pltpu.VMEM((b, nb * tc), BF16),
                        pltpu.VMEM((b, nb * tc), BF16), pltpu.VMEM((b, nb * tc), BF16)],
        compiler_params=_params(("parallel", "arbitrary")),
        name="hyena_mixer",
    )(proj3, proj3, proj3, proj3, conv_w, conv_w, conv_w, conv_b, conv_b, conv_b,
      hyena_bias, cmat, smat, kr, ki, kn)


def _rope(x, cos, sin_lo, sin_hi):
    return (x * cos + pltpu.roll(x, HEAD_DIM - ROPE_HALF, 1) * sin_lo
            + pltpu.roll(x, ROPE_HALF, 1) * sin_hi)


def _attn_kernel(sink_ref, q_ref, k_ref, v_ref, zg_ref, cos_ref, slo_ref, shi_ref, o_ref,
                 kb_ref, vb_ref, *, layer):
    L = k_ref.shape[0]
    qb = q_ref.shape[0]
    kw = QSUB + 2 * WINDOW
    kvh = pl.program_id(1)
    n = pl.program_id(2)

    @pl.when(n == 0)
    def _():
        kb_ref[...] = _rope(k_ref[...], cos_ref[...], slo_ref[...], shi_ref[...]).astype(BF16)
        vb_ref[...] = v_ref[...].astype(BF16)

    head = lax.broadcasted_iota(jnp.int32, (GROUP, 1, 1), 0)
    sk = jnp.zeros((GROUP, 1, 1), F32)
    for g in range(GROUP):
        sk = jnp.where(head == g, sink_ref[layer, kvh * GROUP + g], sk)

    for sb in range(qb // QSUB):
        rows = slice(sb * QSUB, (sb + 1) * QSUB)
        q0 = pl.multiple_of(n * qb + sb * QSUB, QSUB)
        start = pl.multiple_of(jnp.clip(q0 - WINDOW, 0, L - kw), WINDOW)
        win = pl.ds(start, kw)
        tab = pl.ds(q0, QSUB)
        cq, slq, shq = cos_ref[tab, :], slo_ref[tab, :], shi_ref[tab, :]
        q = jnp.concatenate(
            [_rope(q_ref[rows, g * HEAD_DIM:(g + 1) * HEAD_DIM], cq, slq, shq).astype(BF16)
             for g in range(GROUP)], axis=0)
        s = lax.dot_general(q, kb_ref[win, :], (((1,), (1,)), ((), ())),
                            preferred_element_type=F32) * (HEAD_DIM ** -0.5)
        qpos = q0 + lax.broadcasted_iota(jnp.int32, (QSUB, 1), 0)
        kpos = start + lax.broadcasted_iota(jnp.int32, (1, kw), 1)
        valid = jnp.abs(kpos - qpos) <= WINDOW
        s = jnp.where(valid[None], s.reshape(GROUP, QSUB, kw), -jnp.inf)
        m = jnp.maximum(jnp.max(s, axis=-1, keepdims=True), sk)
        p = jnp.exp(s - m)
        denom = jnp.sum(p, axis=-1, keepdims=True) + jnp.exp(sk - m)
        o = jnp.dot(p.reshape(GROUP * QSUB, kw).astype(BF16), vb_ref[win, :], preferred_element_type=F32)
        o = o.reshape(GROUP, QSUB, HEAD_DIM) / denom
        for g in range(GROUP):
            cols = slice(g * HEAD_DIM, (g + 1) * HEAD_DIM)
            zg = zg_ref[rows, cols]
            o_ref[rows, cols] = (o[g] * (zg * jax.nn.sigmoid(zg))).astype(o_ref.dtype)


def _attention(proj3, sink, rope_tabs, col_q, col_k, col_v, col_zg, l, qb=512):
    B, L, _ = proj3.shape
    gw = GROUP * HEAD_DIM
    cos_t, sin_lo, sin_hi = rope_tabs
    tab = pl.BlockSpec((L, HEAD_DIM), lambda b, h, n: (0, 0))
    return pl.pallas_call(
        functools.partial(_attn_kernel, layer=l),
        out_shape=jax.ShapeDtypeStruct((B, L, N_HEADS * HEAD_DIM), BF16),
        grid=(B, N_KV_HEADS, L // qb),
        in_specs=[pl.BlockSpec(memory_space=pltpu.SMEM),
                  pl.BlockSpec((None, qb, gw), lambda b, h, n: (b, n, col_q // gw + h)),
                  pl.BlockSpec((None, L, HEAD_DIM), lambda b, h, n: (b, 0, col_k // HEAD_DIM + h)),
                  pl.BlockSpec((None, L, HEAD_DIM), lambda b, h, n: (b, 0, col_v // HEAD_DIM + h)),
                  pl.BlockSpec((None, qb, gw), lambda b, h, n: (b, n, col_zg // gw + h)),
                  tab, tab, tab],
        out_specs=pl.BlockSpec((None, qb, gw), lambda b, h, n: (b, n, h)),
        scratch_shapes=[pltpu.VMEM((L, HEAD_DIM), BF16), pltpu.VMEM((L, HEAD_DIM), BF16)],
        compiler_params=_params(("parallel", "parallel", "arbitrary")),
        name="window_attention",
    )(sink, proj3, proj3, proj3, proj3, cos_t, sin_lo, sin_hi)


def _merge_out_kernel(*refs, n_gate_blocks, final):
    x_ref, yh_ref, ya_ref = refs[0:3]
    gh_refs = refs[3:3 + n_gate_blocks]
    ga_refs = refs[3 + n_gate_blocks:3 + 2 * n_gate_blocks]
    who_ref, wao_ref, wout_ref, fg_ref, o_ref, m_ref = refs[3 + 2 * n_gate_blocks:]
    gw = gh_refs[0].shape[1]
    yh = yh_ref[...]
    ya = ya_ref[...]
    for c in range(n_gate_blocks):
        cols = slice(c * gw, (c + 1) * gw)
        ph = jnp.dot(yh, who_ref[:, cols], preferred_element_type=F32)
        pa = jnp.dot(ya, wao_ref[:, cols], preferred_element_type=F32)
        merged = jax.nn.sigmoid(gh_refs[c][...]) * ph + jax.nn.sigmoid(ga_refs[c][...]) * pa
        m_ref[:, cols] = merged.astype(BF16)
    out = x_ref[...] + jnp.dot(m_ref[...], wout_ref[...], preferred_element_type=F32)
    if final:
        ms = jnp.mean(out * out, axis=-1, keepdims=True)
        out = out * lax.rsqrt(ms + EPS) * fg_ref[...]
    o_ref[...] = out


def _merge_out(x2d, proj, yh, ya, who, wao, wout, final_g, col_gh, col_ga, l, final, tm=256, gw=512):
    m, d = x2d.shape
    width = yh.shape[1]
    ngb = d // gw
    gspec = lambda col0, c: pl.BlockSpec((tm, gw), lambda i: (i, col0 // gw + c))
    wspec = lambda rows: pl.BlockSpec((None, rows, d), lambda i: (l, 0, 0), pipeline_mode=pl.Buffered(1))
    kern = functools.partial(_merge_out_kernel, n_gate_blocks=ngb, final=final)
    return pl.pallas_call(
        kern,
        out_shape=jax.ShapeDtypeStruct((m, d), F32),
        grid=(m // tm,),
        in_specs=[pl.BlockSpec((tm, d), lambda i: (i, 0)),
                  pl.BlockSpec((tm, width), lambda i: (i, 0)),
                  pl.BlockSpec((tm, width), lambda i: (i, 0))]
                 + [gspec(col_gh, c) for c in range(ngb)]
                 + [gspec(col_ga, c) for c in range(ngb)]
                 + [wspec(width), wspec(width), wspec(d),
                    pl.BlockSpec((1, d), lambda i: (0, 0))],
        out_specs=pl.BlockSpec((tm, d), lambda i: (i, 0)),
        scratch_shapes=[pltpu.VMEM((tm, d), BF16)],
        compiler_params=_params(("parallel",)),
        name="merge_out",
    )(x2d, yh, ya, *([proj] * (2 * ngb)), who, wao, wout, final_g.reshape(1, d))


def _dft_mats(b):
    idx = jnp.arange(b, dtype=jnp.int32)
    k = (idx[:, None] * idx[None, :]) % (2 * b)
    ang = k.astype(F32) * (math.pi / b)
    return jnp.cos(ang).astype(BF16), jnp.sin(ang).astype(BF16)


def _rope_tabs(L):
    inv = ROPE_THETA ** (-jnp.arange(0, ROPE_DIM, 2, dtype=F32) / ROPE_DIM)
    ang = jnp.arange(L, dtype=F32)[:, None] * inv[None, :]
    cos, sin = jnp.cos(ang), jnp.sin(ang)
    ones = jnp.ones((L, HEAD_DIM - ROPE_DIM), F32)
    zeros = jnp.zeros((L, HEAD_DIM - ROPE_HALF), F32)
    cos_t = jnp.concatenate([cos, cos, ones], axis=1)
    sin_lo = jnp.concatenate([-sin, zeros], axis=1)
    sin_hi = jnp.concatenate([jnp.zeros((L, ROPE_HALF), F32), sin, zeros[:, ROPE_HALF:]], axis=1)
    return cos_t, sin_lo, sin_hi


def _filter_feats(L):
    t = jnp.linspace(0.0, 1.0, L, dtype=F32)[:, None]
    bands = jnp.linspace(1e-4, FILTER_BANDS - 1, FILTER_BANDS, dtype=F32)[None, :]
    ang = (2.0 * math.pi / L) * jnp.arange(L, dtype=F32)[:, None] * bands
    feats = jnp.concatenate([t, jnp.cos(ang), -jnp.sin(ang)], axis=-1)
    feats = jnp.pad(feats, ((0, 0), (0, FEAT_PAD - feats.shape[1])))
    return feats, t


def kernel(x, norm_g, w_in, conv_w, conv_b, filt_w1, filt_b1, filt_w2, filt_b2, filt_w3, filt_b3,
           filt_w4, filt_freq, hyena_bias, attn_sink, w_hyena_out, w_attn_out, w_out, final_norm):
    B, L, D = x.shape
    depth = norm_g.shape[0]
    hw = hyena_bias.shape[2]
    aw = N_HEADS * HEAD_DIM
    kvw = N_KV_HEADS * HEAD_DIM
    sizes = (3 * hw, hw, aw, kvw, kvw, aw, D, D)
    cols = [0]
    for s in sizes:
        cols.append(cols[-1] + s)
    col_zhy, col_q, col_k, col_v, col_zat, col_gh, col_ga = cols[1:8]

    cmat, smat = _dft_mats(L // NB)
    rope_tabs = _rope_tabs(L)
    feats, tcol = _filter_feats(L)
    deltas = jnp.abs(jnp.linspace(MIN_DECAY, MAX_DECAY, hw, dtype=F32))[None, :]

    w_in_b, who_b, wao_b, wout_b = (w.astype(BF16) for w in (w_in, w_hyena_out, w_attn_out, w_out))
    row = lambda a: a[:, None, :]
    w1p = jnp.pad(filt_w1, ((0, 0), (0, FEAT_PAD - filt_w1.shape[1]), (0, 0)))
    xf = x.reshape(B * L, D)
    for l in range(depth):
        proj = _norm_proj(xf, row(norm_g), w_in_b, l)
        proj3 = proj.reshape(B, L, -1)
        kr, ki, kn = _filters(feats, tcol, w1p, row(filt_b1), filt_w2, row(filt_b2), filt_w3, row(filt_b3),
                              row(filt_freq), filt_w4, deltas, cmat, smat, l, NB)
        y_hy = _hyena(proj3, conv_w, row(conv_b), hyena_bias, cmat, smat, kr, ki, kn, l, NB)
        y_at = _attention(proj3, attn_sink, rope_tabs, col_q, col_k, col_v, col_zat, l)
        xf = _merge_out(xf, proj, y_hy.reshape(B * L, hw), y_at.reshape(B * L, aw),
                        who_b, wao_b, wout_b, final_norm, col_gh, col_ga, l, final=(l == depth - 1))
    return xf.reshape(B, L, D)
```

```python
import functools
import math

import jax
import jax.numpy as jnp
from jax import lax
from jax.experimental import pallas as pl
from jax.experimental.pallas import tpu as pltpu

F32 = jnp.float32
BF16 = jnp.bfloat16

HEAD_DIM = 128
N_HEADS = 8
N_KV_HEADS = 2
GROUP = N_HEADS // N_KV_HEADS
WINDOW = 128
QSUB = 128
ROPE_THETA = 500000.0
ROPE_DIM = HEAD_DIM // 4
ROPE_HALF = ROPE_DIM // 2
EPS = 1e-6
FILTER_BANDS = 16
FEAT_PAD = 128
DECAY_TARGET = 1e-2
MIN_DECAY = math.log(DECAY_TARGET) / 0.3
MAX_DECAY = math.log(DECAY_TARGET) / 1.5

NB = 4
VMEM_LIMIT = 56 * 1024 * 1024


def _params(sem, vmem=VMEM_LIMIT):
    return pltpu.CompilerParams(dimension_semantics=sem, vmem_limit_bytes=vmem)


def _const_spec(shape):
    return pl.BlockSpec(shape, lambda *_: (0,) * len(shape), pipeline_mode=pl.Buffered(1))


def _alt_sign(rows):
    return jnp.where((lax.broadcasted_iota(jnp.int32, (rows, 1), 0) & 1) == 0, 1.0, -1.0)


def _norm_proj_kernel(x_ref, g_ref, w_ref, o_ref, h_ref):
    @pl.when(pl.program_id(1) == 0)
    def _():
        x = x_ref[...]
        ms = jnp.mean(x * x, axis=-1, keepdims=True)
        h_ref[...] = (x * lax.rsqrt(ms + EPS) * g_ref[...]).astype(BF16)

    o_ref[...] = jnp.dot(h_ref[...], w_ref[...], preferred_element_type=F32)


def _norm_proj(x2d, g_all, w_all, l, tm=1024, tn=1536):
    m, d = x2d.shape
    n = w_all.shape[2]
    return pl.pallas_call(
        _norm_proj_kernel,
        out_shape=jax.ShapeDtypeStruct((m, n), F32),
        grid=(m // tm, n // tn),
        in_specs=[
            pl.BlockSpec((tm, d), lambda i, j: (i, 0)),
            pl.BlockSpec((None, 1, d), lambda i, j: (l, 0, 0)),
            pl.BlockSpec((None, d, tn), lambda i, j: (l, 0, j)),
        ],
        out_specs=pl.BlockSpec((tm, tn), lambda i, j: (i, j)),
        scratch_shapes=[pltpu.VMEM((tm, d), BF16)],
        compiler_params=_params(("parallel", "arbitrary")),
        name="norm_proj",
    )(x2d, g_all, w_all)


def _filter_kernel(feats_ref, t_ref, w1_ref, b1_ref, w2_ref, b2_ref, w3_ref, b3_ref, fr_ref,
                   w4f_ref, w4b_ref, dl_ref, c_ref, s_ref,
                   kr_ref, ki_ref, kn_ref,
                   hdn_ref, xf_ref, xb_ref, cf_ref, sf_ref, cb_ref, sb_ref, *, nb, row_chunk):
    L = feats_ref.shape[0]
    b = L // nb
    tc = dl_ref.shape[1]
    hp = lax.Precision.HIGHEST
    nchunks = b // row_chunk
    alt = _alt_sign(row_chunk)

    @pl.when((pl.program_id(0) == 0) & (pl.program_id(1) == 0))
    def _():
        fr = fr_ref[...]
        h = jnp.sin(fr * (jnp.dot(feats_ref[...], w1_ref[...], precision=hp,
                                  preferred_element_type=F32) + b1_ref[...]))
        h = jnp.sin(fr * (jnp.dot(h, w2_ref[...], precision=hp,
                                  preferred_element_type=F32) + b2_ref[...]))
        h = jnp.sin(fr * (jnp.dot(h, w3_ref[...], precision=hp,
                                  preferred_element_type=F32) + b3_ref[...]))
        hdn_ref[...] = h

    def taps(rows):
        h = hdn_ref[rows, :]
        decay = jnp.exp(-t_ref[rows, :] * dl_ref[...])
        fwd = jnp.dot(h, w4f_ref[...], precision=hp, preferred_element_type=F32) * decay
        bwd = jnp.dot(h, w4b_ref[...], precision=hp, preferred_element_type=F32) * decay
        return fwd, bwd

    zero_row = jnp.zeros((1, tc), F32)
    f0, b0, af, ab = [], [], [], []
    for q in range(nb):
        head_f, head_b = taps(pl.ds(q * b, 8))
        f0.append(head_f[0:1, :])
        b0.append(head_b[0:1, :])

        def tap_chunk(i, carry, q=q):
            l0 = pl.multiple_of(i * row_chunk, row_chunk)
            fwd, bwd = taps(pl.ds(pl.multiple_of(q * b + l0, row_chunk), row_chunk))
            first = (lax.broadcasted_iota(jnp.int32, (row_chunk, 1), 0) + l0) == 0
            fz = jnp.where(first, 0.0, fwd)
            bz = jnp.where(first, 0.0, bwd)
            xf_ref[q, pl.ds(l0, row_chunk), :] = fz.astype(BF16)
            xb_ref[q, pl.ds(l0, row_chunk), :] = bz.astype(BF16)
            return (carry[0] + jnp.sum(fz * alt, axis=0, keepdims=True),
                    carry[1] + jnp.sum(bz * alt, axis=0, keepdims=True))

        a_f, a_b = lax.fori_loop(0, nchunks, tap_chunk, (zero_row, zero_row))
        af.append(a_f)
        ab.append(a_b)

        def spec_chunk(i, carry, q=q):
            r = pl.ds(pl.multiple_of(i * row_chunk, row_chunk), row_chunk)
            cf_ref[q, r, :] = jnp.dot(c_ref[r, :], xf_ref[q], preferred_element_type=F32)
            sf_ref[q, r, :] = jnp.dot(s_ref[r, :], xf_ref[q], preferred_element_type=F32)
            cb_ref[q, r, :] = jnp.dot(c_ref[r, :], xb_ref[q], preferred_element_type=F32)
            sb_ref[q, r, :] = jnp.dot(s_ref[r, :], xb_ref[q], preferred_element_type=F32)
            return carry

        lax.fori_loop(0, nchunks, spec_chunk, 0)

    inv_n = 1.0 / (2 * b)
    for d in range(-(nb - 1), nb):
        slot = d + nb - 1
        e = -d
        if d >= 1:
            kn = af[d] + f0[d] + af[d - 1]
        elif d == 0:
            kn = af[0] + f0[0] + ab[0]
        else:
            kn = b0[e] + ab[e] + ab[e - 1]
        kn_ref[slot] = kn * inv_n

        def combine(i, carry, d=d, e=e, slot=slot):
            r0 = pl.multiple_of(i * row_chunk, row_chunk)
            r = pl.ds(r0, row_chunk)
            first = (lax.broadcasted_iota(jnp.int32, (row_chunk, 1), 0) + r0) == 0
            wgt = jnp.where(first, inv_n, 2.0 * inv_n)
            if d >= 1:
                kr = cf_ref[d, r, :] + f0[d] + alt * cf_ref[d - 1, r, :]
                ki = -sf_ref[d, r, :] - alt * sf_ref[d - 1, r, :]
            elif d == 0:
                kr = cf_ref[0, r, :] + f0[0] + cb_ref[0, r, :]
                ki = sb_ref[0, r, :] - sf_ref[0, r, :]
            else:
                kr = b0[e] + cb_ref[e, r, :] + alt * cb_ref[e - 1, r, :]
                ki = sb_ref[e, r, :] + alt * sb_ref[e - 1, r, :]
            kr_ref[slot, r, :] = kr * wgt
            ki_ref[slot, r, :] = ki * wgt
            return carry

        lax.fori_loop(0, nchunks, combine, 0)


def _filters(feats, tcol, w1p, b1, w2, b2, w3, b3, freq, w4, deltas, cmat, smat, l, nb, tc=256,
             row_chunk=512):
    L = feats.shape[0]
    b = L // nb
    nd = 2 * nb - 1
    width = deltas.shape[1]
    nct = width // tc
    hid = w2.shape[1]
    small = lambda a: pl.BlockSpec(a.shape, lambda o, c: (0,) * a.ndim)
    layer = lambda a: pl.BlockSpec((None,) + a.shape[1:], lambda o, c: (l,) + (0,) * (a.ndim - 1))
    kern = functools.partial(_filter_kernel, nb=nb, row_chunk=row_chunk)
    kspec = pl.BlockSpec((None, nd, b, tc), lambda o, c: (o, 0, 0, c))
    blk = lambda dt: pltpu.VMEM((nb, b, tc), dt)
    return pl.pallas_call(
        kern,
        out_shape=(jax.ShapeDtypeStruct((2, nd, b, width), F32),
                   jax.ShapeDtypeStruct((2, nd, b, width), F32),
                   jax.ShapeDtypeStruct((2, nd, 1, width), F32)),
        grid=(2, nct),
        in_specs=[small(feats), small(tcol), layer(w1p), layer(b1), layer(w2), layer(b2),
                  layer(w3), layer(b3), layer(freq),
                  pl.BlockSpec((None, hid, tc), lambda o, c: (l, 0, o * 2 * nct + c)),
                  pl.BlockSpec((None, hid, tc), lambda o, c: (l, 0, o * 2 * nct + nct + c)),
                  pl.BlockSpec((1, tc), lambda o, c: (0, c)),
                  _const_spec((b, b)), _const_spec((b, b))],
        out_specs=(kspec, kspec, pl.BlockSpec((None, nd, 1, tc), lambda o, c: (o, 0, 0, c))),
        scratch_shapes=[pltpu.VMEM((L, hid), F32), blk(BF16), blk(BF16),
                        blk(F32), blk(F32), blk(F32), blk(F32)],
        compiler_params=_params(("arbitrary", "arbitrary")),
        name="hyena_filters",
    )(feats, tcol, w1p, b1, w2, b2, w3, b3, freq, w4, w4, deltas, cmat, smat)


def _sconv_chunk(u_ref, w_ref, b_ref, i, rows, nchunks):
    r0 = i * rows
    tc = u_ref.shape[1]
    u = u_ref[r0:r0 + rows, :]
    zero = jnp.zeros((1, tc), F32)
    up = u_ref[r0 - 8:r0, :][7:8, :] if i > 0 else zero
    dn = u_ref[r0 + rows:r0 + rows + 8, :][0:1, :] if i < nchunks - 1 else zero
    row = lax.broadcasted_iota(jnp.int32, (rows, 1), 0)
    prev = jnp.where(row == 0, up, pltpu.roll(u, 1, 0))
    nxt = jnp.where(row == rows - 1, dn, pltpu.roll(u, rows - 1, 0))
    return b_ref[...] + prev * w_ref[0:1, :] + u * w_ref[1:2, :] + nxt * w_ref[2:3, :]


def _hyena_kernel(v_ref, x1_ref, x2_ref, zg_ref, wv_ref, wx1_ref, wx2_ref, bv_ref, bx1_ref, bx2_ref,
                  hb_ref, c_ref, s_ref, kr_ref, ki_ref, kn_ref, o_ref,
                  u_ref, ub_ref, a_ref, bn_ref, *, nb, seq_chunk, freq_chunk):
    L, tc = u_ref.shape
    b = L // nb
    per_block = b // seq_chunk
    nseq = L // seq_chunk
    alt = _alt_sign(seq_chunk)

    def lanes(j):
        return slice(j * tc, (j + 1) * tc)

    def put_signal(chunk, val, nyq):
        j, local = divmod(chunk, per_block)
        rows = slice(chunk * seq_chunk, (chunk + 1) * seq_chunk)
        u_ref[rows, :] = val
        ub_ref[local * seq_chunk:(local + 1) * seq_chunk, lanes(j)] = val.astype(BF16)
        nyq[j] = nyq[j] + jnp.sum(val * alt, axis=0, keepdims=True)

    def forward(order):
        for fc in range(b // freq_chunk):
            r = slice(fc * freq_chunk, (fc + 1) * freq_chunk)
            ur = jnp.dot(c_ref[r, :], ub_ref[...], preferred_element_type=F32)
            us = jnp.dot(s_ref[r, :], ub_ref[...], preferred_element_type=F32)
            for i in range(nb):
                acc_a = acc_b = None
                for j in range(nb):
                    kr = kr_ref[order, i - j + nb - 1, r, :]
                    ki = ki_ref[order, i - j + nb - 1, r, :]
                    urj, usj = ur[:, lanes(j)], us[:, lanes(j)]
                    ta = urj * kr + usj * ki
                    tb = usj * kr - urj * ki
                    acc_a = ta if acc_a is None else acc_a + ta
                    acc_b = tb if acc_b is None else acc_b + tb
                a_ref[r, lanes(i)] = acc_a.astype(BF16)
                bn_ref[r, lanes(i)] = acc_b.astype(BF16)

    def inverse(order, nyq, finish):
        nyq_term = []
        for i in range(nb):
            acc = None
            for j in range(nb):
                t = nyq[j] * kn_ref[order, i - j + nb - 1]
                acc = t if acc is None else acc + t
            nyq_term.append(acc)
        for tcn in range(per_block):
            r = slice(tcn * seq_chunk, (tcn + 1) * seq_chunk)
            y2 = jnp.dot(c_ref[r, :], a_ref[...], preferred_element_type=F32)
            y2 = y2 + jnp.dot(s_ref[r, :], bn_ref[...], preferred_element_type=F32)
            for i in range(nb):
                chunk = i * per_block + tcn
                rows = slice(chunk * seq_chunk, (chunk + 1) * seq_chunk)
                y = y2[:, lanes(i)] + alt * nyq_term[i]
                finish(chunk, rows, y + u_ref[rows, :] * hb_ref[order:order + 1, :])

    zero_row = jnp.zeros((1, tc), F32)
    nyq0 = [zero_row] * nb
    for chunk in range(nseq):
        put_signal(chunk, _sconv_chunk(v_ref, wv_ref, bv_ref, chunk, seq_chunk, nseq), nyq0)

    forward(0)

    nyq1 = [zero_row] * nb

    def finish_z(chunk, rows, y):
        put_signal(chunk, _sconv_chunk(x1_ref, wx1_ref, bx1_ref, chunk, seq_chunk, nseq) * y, nyq1)

    inverse(0, nyq0, finish_z)
    forward(1)

    def finish_out(chunk, rows, y):
        y = _sconv_chunk(x2_ref, wx2_ref, bx2_ref, chunk, seq_chunk, nseq) * y
        zg = zg_ref[rows, :]
        o_ref[rows, :] = (y * (zg * jax.nn.sigmoid(zg))).astype(o_ref.dtype)

    inverse(1, nyq1, finish_out)


def _hyena(proj3, conv_w, conv_b, hyena_bias, cmat, smat, kr, ki, kn, l, nb, tc=256,
           seq_chunk=512, freq_chunk=512):
    B, L, _ = proj3.shape
    b = L // nb
    nd = 2 * nb - 1
    width = hyena_bias.shape[2]
    nct = width // tc
    sig = lambda part: pl.BlockSpec((None, L, tc), lambda j, bb: (bb, 0, part * nct + j))
    cw = lambda part: pl.BlockSpec((None, 3, tc), lambda j, bb: (l, 0, part * nct + j))
    cb = lambda part: pl.BlockSpec((None, 1, tc), lambda j, bb: (l, 0, part * nct + j))
    kspec = pl.BlockSpec((2, nd, b, tc), lambda j, bb: (0, 0, 0, j), pipeline_mode=pl.Buffered(1))
    kern = functools.partial(_hyena_kernel, nb=nb, seq_chunk=seq_chunk, freq_chunk=freq_chunk)
    return pl.pallas_call(
        kern,
        out_shape=jax.ShapeDtypeStruct((B, L, width), BF16),
        grid=(nct, B),
        in_specs=[sig(0), sig(1), sig(2), sig(3),
                  cw(0), cw(1), cw(2), cb(0), cb(1), cb(2),
                  pl.BlockSpec((None, 2, tc), lambda j, bb: (l, 0, j)),
                  _const_spec((b, b)), _const_spec((b, b)),
                  kspec, kspec,
                  pl.BlockSpec((2, nd, 1, tc), lambda j, bb: (0, 0, 0, j))],
        out_specs=pl.BlockSpec((None, L, tc), lambda j, bb: (bb, 0, j)),
        scratch_shapes=[pltpu.VMEM((L, tc), F32), pltpu.VMEM((b, nb * tc), BF16),
                        pltpu.VMEM((b, nb * tc), BF16), pltpu.VMEM((b, nb * tc), BF16)],
        compiler_params=_params(("parallel", "arbitrary")),
        name="hyena_mixer",
    )(proj3, proj3, proj3, proj3, conv_w, conv_w, conv_w, conv_b, conv_b, conv_b,
      hyena_bias, cmat, smat, kr, ki, kn)


def _rope(x, cos, sin_lo, sin_hi):
    return (x * cos + pltpu.roll(x, HEAD_DIM - ROPE_HALF, 1) * sin_lo
            + pltpu.roll(x, ROPE_HALF, 1) * sin_hi)


def _attn_kernel(sink_ref, q_ref, k_ref, v_ref, zg_ref, cos_ref, slo_ref, shi_ref, o_ref,
                 kb_ref, vt_ref, *, layer):
    L = k_ref.shape[0]
    qb = q_ref.shape[0]
    kw = QSUB + 2 * WINDOW
    gq = GROUP * QSUB
    kvh = pl.program_id(1)
    n = pl.program_id(2)

    @pl.when(n == 0)
    def _():
        kb_ref[...] = _rope(k_ref[...], cos_ref[...], slo_ref[...], shi_ref[...]).astype(BF16)
        vt_ref[...] = v_ref[...].T.astype(BF16)

    log2e = math.log2(math.e)
    lane_head = lax.broadcasted_iota(jnp.int32, (1, gq), 1) // QSUB
    sk2 = jnp.zeros((1, gq), F32)
    for g in range(GROUP):
        sk2 = jnp.where(lane_head == g, sink_ref[layer, kvh * GROUP + g] * log2e, sk2)

    for sb in range(qb // QSUB):
        rows = slice(sb * QSUB, (sb + 1) * QSUB)
        q0 = pl.multiple_of(n * qb + sb * QSUB, QSUB)
        start = pl.multiple_of(jnp.clip(q0 - WINDOW, 0, L - kw), WINDOW)
        win = pl.ds(start, kw)
        tab = pl.ds(q0, QSUB)
        cq, slq, shq = cos_ref[tab, :], slo_ref[tab, :], shi_ref[tab, :]
        q = jnp.concatenate(
            [_rope(q_ref[rows, g * HEAD_DIM:(g + 1) * HEAD_DIM], cq, slq, shq).astype(BF16)
             for g in range(GROUP)], axis=0)
        raw = lax.dot_general(kb_ref[win, :], q, (((1,), (1,)), ((), ())),
                              preferred_element_type=F32)
        kpos = start + lax.broadcasted_iota(jnp.int32, (kw, 1), 0)
        qpos = q0 + lax.broadcasted_iota(jnp.int32, (1, QSUB), 1)
        cap = jnp.where(jnp.abs(kpos - qpos) <= WINDOW, jnp.inf, -jnp.inf)
        raw = jnp.minimum(raw, jnp.concatenate([cap] * GROUP, axis=1))
        c = (HEAD_DIM ** -0.5) * log2e
        m2 = jnp.maximum(jnp.max(raw, axis=0, keepdims=True) * c, sk2)
        p = jnp.exp2(raw * c - m2)
        denom = jnp.sum(p, axis=0, keepdims=True) + jnp.exp2(sk2 - m2)
        ot = jnp.dot(vt_ref[:, win], p.astype(BF16), preferred_element_type=F32) / denom
        for g in range(GROUP):
            cols = slice(g * HEAD_DIM, (g + 1) * HEAD_DIM)
            o = ot[:, g * QSUB:(g + 1) * QSUB].T
            zg = zg_ref[rows, cols]
            o_ref[rows, cols] = (o * (zg * jax.nn.sigmoid(zg))).astype(o_ref.dtype)


def _attention(proj3, sink, rope_tabs, col_q, col_k, col_v, col_zg, l, qb=512):
    B, L, _ = proj3.shape
    gw = GROUP * HEAD_DIM
    cos_t, sin_lo, sin_hi = rope_tabs
    tab = pl.BlockSpec((L, HEAD_DIM), lambda b, h, n: (0, 0))
    return pl.pallas_call(
        functools.partial(_attn_kernel, layer=l),
        out_shape=jax.ShapeDtypeStruct((B, L, N_HEADS * HEAD_DIM), BF16),
        grid=(B, N_KV_HEADS, L // qb),
        in_specs=[pl.BlockSpec(memory_space=pltpu.SMEM),
                  pl.BlockSpec((None, qb, gw), lambda b, h, n: (b, n, col_q // gw + h)),
                  pl.BlockSpec((None, L, HEAD_DIM), lambda b, h, n: (b, 0, col_k // HEAD_DIM + h)),
                  pl.BlockSpec((None, L, HEAD_DIM), lambda b, h, n: (b, 0, col_v // HEAD_DIM + h)),
                  pl.BlockSpec((None, qb, gw), lambda b, h, n: (b, n, col_zg // gw + h)),
                  tab, tab, tab],
        out_specs=pl.BlockSpec((None, qb, gw), lambda b, h, n: (b, n, h)),
        scratch_shapes=[pltpu.VMEM((L, HEAD_DIM), BF16), pltpu.VMEM((HEAD_DIM, L), BF16)],
        compiler_params=_params(("parallel", "parallel", "arbitrary")),
        name="window_attention",
    )(sink, proj3, proj3, proj3, proj3, cos_t, sin_lo, sin_hi)


def _merge_out_kernel(*refs, n_gate_blocks, final):
    x_ref, yh_ref, ya_ref = refs[0:3]
    gh_refs = refs[3:3 + n_gate_blocks]
    ga_refs = refs[3 + n_gate_blocks:3 + 2 * n_gate_blocks]
    who_ref, wao_ref, wout_ref, fg_ref, o_ref, m_ref = refs[3 + 2 * n_gate_blocks:]
    gw = gh_refs[0].shape[1]
    yh = yh_ref[...]
    ya = ya_ref[...]
    for c in range(n_gate_blocks):
        cols = slice(c * gw, (c + 1) * gw)
        ph = jnp.dot(yh, who_ref[:, cols], preferred_element_type=F32)
        pa = jnp.dot(ya, wao_ref[:, cols], preferred_element_type=F32)
        merged = jax.nn.sigmoid(gh_refs[c][...]) * ph + jax.nn.sigmoid(ga_refs[c][...]) * pa
        m_ref[:, cols] = merged.astype(BF16)
    out = x_ref[...] + jnp.dot(m_ref[...], wout_ref[...], preferred_element_type=F32)
    if final:
        ms = jnp.mean(out * out, axis=-1, keepdims=True)
        out = out * lax.rsqrt(ms + EPS) * fg_ref[...]
    o_ref[...] = out


def _merge_out(x2d, proj, yh, ya, who, wao, wout, final_g, col_gh, col_ga, l, final, tm=256, gw=512):
    m, d = x2d.shape
    width = yh.shape[1]
    ngb = d // gw
    gspec = lambda col0, c: pl.BlockSpec((tm, gw), lambda i: (i, col0 // gw + c))
    wspec = lambda rows: pl.BlockSpec((None, rows, d), lambda i: (l, 0, 0), pipeline_mode=pl.Buffered(1))
    kern = functools.partial(_merge_out_kernel, n_gate_blocks=ngb, final=final)
    return pl.pallas_call(
        kern,
        out_shape=jax.ShapeDtypeStruct((m, d), F32),
        grid=(m // tm,),
        in_specs=[pl.BlockSpec((tm, d), lambda i: (i, 0)),
                  pl.BlockSpec((tm, width), lambda i: (i, 0)),
                  pl.BlockSpec((tm, width), lambda i: (i, 0))]
                 + [gspec(col_gh, c) for c in range(ngb)]
                 + [gspec(col_ga, c) for c in range(ngb)]
                 + [wspec(width), wspec(width), wspec(d),
                    pl.BlockSpec((1, d), lambda i: (0, 0))],
        out_specs=pl.BlockSpec((tm, d), lambda i: (i, 0)),
        scratch_shapes=[pltpu.VMEM((tm, d), BF16)],
        compiler_params=_params(("parallel",)),
        name="merge_out",
    )(x2d, yh, ya, *([proj] * (2 * ngb)), who, wao, wout, final_g.reshape(1, d))


def _dft_mats(b):
    idx = jnp.arange(b, dtype=jnp.int32)
    k = (idx[:, None] * idx[None, :]) % (2 * b)
    ang = k.astype(F32) * (math.pi / b)
    return jnp.cos(ang).astype(BF16), jnp.sin(ang).astype(BF16)


def _rope_tabs(L):
    inv = ROPE_THETA ** (-jnp.arange(0, ROPE_DIM, 2, dtype=F32) / ROPE_DIM)
    ang = jnp.arange(L, dtype=F32)[:, None] * inv[None, :]
    cos, sin = jnp.cos(ang), jnp.sin(ang)
    ones = jnp.ones((L, HEAD_DIM - ROPE_DIM), F32)
    zeros = jnp.zeros((L, HEAD_DIM - ROPE_HALF), F32)
    cos_t = jnp.concatenate([cos, cos, ones], axis=1)
    sin_lo = jnp.concatenate([-sin, zeros], axis=1)
    sin_hi = jnp.concatenate([jnp.zeros((L, ROPE_HALF), F32), sin, zeros[:, ROPE_HALF:]], axis=1)
    return cos_t, sin_lo, sin_hi


def _filter_feats(L):
    t = jnp.linspace(0.0, 1.0, L, dtype=F32)[:, None]
    bands = jnp.linspace(1e-4, FILTER_BANDS - 1, FILTER_BANDS, dtype=F32)[None, :]
    ang = (2.0 * math.pi / L) * jnp.arange(L, dtype=F32)[:, None] * bands
    feats = jnp.concatenate([t, jnp.cos(ang), -jnp.sin(ang)], axis=-1)
    feats = jnp.pad(feats, ((0, 0), (0, FEAT_PAD - feats.shape[1])))
    return feats, t


def kernel(x, norm_g, w_in, conv_w, conv_b, filt_w1, filt_b1, filt_w2, filt_b2, filt_w3, filt_b3,
           filt_w4, filt_freq, hyena_bias, attn_sink, w_hyena_out, w_attn_out, w_out, final_norm):
    B, L, D = x.shape
    depth = norm_g.shape[0]
    hw = hyena_bias.shape[2]
    aw = N_HEADS * HEAD_DIM
    kvw = N_KV_HEADS * HEAD_DIM
    sizes = (3 * hw, hw, aw, kvw, kvw, aw, D, D)
    cols = [0]
    for s in sizes:
        cols.append(cols[-1] + s)
    col_zhy, col_q, col_k, col_v, col_zat, col_gh, col_ga = cols[1:8]

    cmat, smat = _dft_mats(L // NB)
    rope_tabs = _rope_tabs(L)
    feats, tcol = _filter_feats(L)
    deltas = jnp.abs(jnp.linspace(MIN_DECAY, MAX_DECAY, hw, dtype=F32))[None, :]

    w_in_b, who_b, wao_b, wout_b = (w.astype(BF16) for w in (w_in, w_hyena_out, w_attn_out, w_out))
    row = lambda a: a[:, None, :]
    w1p = jnp.pad(filt_w1, ((0, 0), (0, FEAT_PAD - filt_w1.shape[1]), (0, 0)))
    xf = x.reshape(B * L, D)
    for l in range(depth):
        proj = _norm_proj(xf, row(norm_g), w_in_b, l)
        proj3 = proj.reshape(B, L, -1)
        kr, ki, kn = _filters(feats, tcol, w1p, row(filt_b1), filt_w2, row(filt_b2), filt_w3, row(filt_b3),
                              row(filt_freq), filt_w4, deltas, cmat, smat, l, NB)
        y_hy = _hyena(proj3, conv_w, row(conv_b), hyena_bias, cmat, smat, kr, ki, kn, l, NB)
        y_at = _attention(proj3, attn_sink, rope_tabs, col_q, col_k, col_v, col_zat, l)
        xf = _merge_out(xf, proj, y_hy.reshape(B * L, hw), y_at.reshape(B * L, aw),
                        who_b, wao_b, wout_b, final_norm, col_gh, col_ga, l, final=(l == depth - 1))
    return xf.reshape(B, L, D)
```

```python
import functools
import math

import jax
import jax.numpy as jnp
from jax import lax
from jax.experimental import pallas as pl
from jax.experimental.pallas import tpu as pltpu

F32 = jnp.float32
BF16 = jnp.bfloat16

HEAD_DIM = 128
N_HEADS = 8
N_KV_HEADS = 2
GROUP = N_HEADS // N_KV_HEADS
WINDOW = 128
QSUB = 128
ROPE_THETA = 500000.0
ROPE_DIM = HEAD_DIM // 4
ROPE_HALF = ROPE_DIM // 2
EPS = 1e-6
FILTER_BANDS = 16
FEAT_PAD = 128
DECAY_TARGET = 1e-2
MIN_DECAY = math.log(DECAY_TARGET) / 0.3
MAX_DECAY = math.log(DECAY_TARGET) / 1.5

NB = 4
VMEM_LIMIT = 56 * 1024 * 1024


def _params(sem, vmem=VMEM_LIMIT):
    return pltpu.CompilerParams(dimension_semantics=sem, vmem_limit_bytes=vmem)


def _const_spec(shape):
    return pl.BlockSpec(shape, lambda *_: (0,) * len(shape), pipeline_mode=pl.Buffered(1))


def _alt_sign(rows):
    return jnp.where((lax.broadcasted_iota(jnp.int32, (rows, 1), 0) & 1) == 0, 1.0, -1.0)


def _norm_proj_kernel(x_ref, g_ref, w_ref, o_ref, h_ref):
    @pl.when(pl.program_id(1) == 0)
    def _():
        x = x_ref[...]
        ms = jnp.mean(x * x, axis=-1, keepdims=True)
        h_ref[...] = (x * lax.rsqrt(ms + EPS) * g_ref[...]).astype(BF16)

    o_ref[...] = jnp.dot(h_ref[...], w_ref[...], preferred_element_type=F32)


def _norm_proj(x2d, g_all, w_all, l, tm=1024, tn=1536):
    m, d = x2d.shape
    n = w_all.shape[2]
    return pl.pallas_call(
        _norm_proj_kernel,
        out_shape=jax.ShapeDtypeStruct((m, n), F32),
        grid=(m // tm, n // tn),
        in_specs=[
            pl.BlockSpec((tm, d), lambda i, j: (i, 0)),
            pl.BlockSpec((None, 1, d), lambda i, j: (l, 0, 0)),
            pl.BlockSpec((None, d, tn), lambda i, j: (l, 0, j)),
        ],
        out_specs=pl.BlockSpec((tm, tn), lambda i, j: (i, j)),
        scratch_shapes=[pltpu.VMEM((tm, d), BF16)],
        compiler_params=_params(("parallel", "arbitrary")),
        name="norm_proj",
    )(x2d, g_all, w_all)


def _filter_kernel(feats_ref, t_ref, w1_ref, b1_ref, w2_ref, b2_ref, w3_ref, b3_ref, fr_ref,
                   w4f_ref, w4b_ref, dl_ref, c_ref, s_ref,
                   kr_ref, ki_ref, kn_ref,
                   hdn_ref, xf_ref, xb_ref, cf_ref, sf_ref, cb_ref, sb_ref, *, nb, row_chunk):
    L = feats_ref.shape[0]
    b = L // nb
    tc = dl_ref.shape[1]
    hp = lax.Precision.HIGHEST
    nchunks = b // row_chunk
    alt = _alt_sign(row_chunk)

    @pl.when((pl.program_id(0) == 0) & (pl.program_id(1) == 0))
    def _():
        fr = fr_ref[...]
        h = jnp.sin(fr * (jnp.dot(feats_ref[...], w1_ref[...], precision=hp,
                                  preferred_element_type=F32) + b1_ref[...]))
        h = jnp.sin(fr * (jnp.dot(h, w2_ref[...], precision=hp,
                                  preferred_element_type=F32) + b2_ref[...]))
        h = jnp.sin(fr * (jnp.dot(h, w3_ref[...], precision=hp,
                                  preferred_element_type=F32) + b3_ref[...]))
        hdn_ref[...] = h

    def taps(rows):
        h = hdn_ref[rows, :]
        decay = jnp.exp(-t_ref[rows, :] * dl_ref[...])
        fwd = jnp.dot(h, w4f_ref[...], precision=hp, preferred_element_type=F32) * decay
        bwd = jnp.dot(h, w4b_ref[...], precision=hp, preferred_element_type=F32) * decay
        return fwd, bwd

    zero_row = jnp.zeros((1, tc), F32)
    f0, b0, af, ab = [], [], [], []
    for q in range(nb):
        head_f, head_b = taps(pl.ds(q * b, 8))
        f0.append(head_f[0:1, :])
        b0.append(head_b[0:1, :])

        def tap_chunk(i, carry, q=q):
            l0 = pl.multiple_of(i * row_chunk, row_chunk)
            fwd, bwd = taps(pl.ds(pl.multiple_of(q * b + l0, row_chunk), row_chunk))
            first = (lax.broadcasted_iota(jnp.int32, (row_chunk, 1), 0) + l0) == 0
            fz = jnp.where(first, 0.0, fwd)
            bz = jnp.where(first, 0.0, bwd)
            xf_ref[q, pl.ds(l0, row_chunk), :] = fz.astype(BF16)
            xb_ref[q, pl.ds(l0, row_chunk), :] = bz.astype(BF16)
            return (carry[0] + jnp.sum(fz * alt, axis=0, keepdims=True),
                    carry[1] + jnp.sum(bz * alt, axis=0, keepdims=True))

        a_f, a_b = lax.fori_loop(0, nchunks, tap_chunk, (zero_row, zero_row))
        af.append(a_f)
        ab.append(a_b)

        def spec_chunk(i, carry, q=q):
            r = pl.ds(pl.multiple_of(i * row_chunk, row_chunk), row_chunk)
            cf_ref[q, r, :] = jnp.dot(c_ref[r, :], xf_ref[q], preferred_element_type=F32)
            sf_ref[q, r, :] = jnp.dot(s_ref[r, :], xf_ref[q], preferred_element_type=F32)
            cb_ref[q, r, :] = jnp.dot(c_ref[r, :], xb_ref[q], preferred_element_type=F32)
            sb_ref[q, r, :] = jnp.dot(s_ref[r, :], xb_ref[q], preferred_element_type=F32)
            return carry

        lax.fori_loop(0, nchunks, spec_chunk, 0)

    inv_n = 1.0 / (2 * b)
    for d in range(-(nb - 1), nb):
        slot = d + nb - 1
        e = -d
        if d >= 1:
            kn = af[d] + f0[d] + af[d - 1]
        elif d == 0:
            kn = af[0] + f0[0] + ab[0]
        else:
            kn = b0[e] + ab[e] + ab[e - 1]
        kn_ref[slot] = kn * inv_n

        def combine(i, carry, d=d, e=e, slot=slot):
            r0 = pl.multiple_of(i * row_chunk, row_chunk)
            r = pl.ds(r0, row_chunk)
            first = (lax.broadcasted_iota(jnp.int32, (row_chunk, 1), 0) + r0) == 0
            wgt = jnp.where(first, inv_n, 2.0 * inv_n)
            if d >= 1:
                kr = cf_ref[d, r, :] + f0[d] + alt * cf_ref[d - 1, r, :]
                ki = -sf_ref[d, r, :] - alt * sf_ref[d - 1, r, :]
            elif d == 0:
                kr = cf_ref[0, r, :] + f0[0] + cb_ref[0, r, :]
                ki = sb_ref[0, r, :] - sf_ref[0, r, :]
            else:
                kr = b0[e] + cb_ref[e, r, :] + alt * cb_ref[e - 1, r, :]
                ki = sb_ref[e, r, :] + alt * sb_ref[e - 1, r, :]
            kr_ref[slot, r, :] = (kr * wgt).astype(kr_ref.dtype)
            ki_ref[slot, r, :] = (ki * wgt).astype(ki_ref.dtype)
            return carry

        lax.fori_loop(0, nchunks, combine, 0)


def _filters(feats, tcol, w1p, b1, w2, b2, w3, b3, freq, w4, deltas, cmat, smat, l, nb, tc=256,
             row_chunk=512):
    L = feats.shape[0]
    b = L // nb
    nd = 2 * nb - 1
    width = deltas.shape[1]
    nct = width // tc
    hid = w2.shape[1]
    small = lambda a: pl.BlockSpec(a.shape, lambda o, c: (0,) * a.ndim)
    layer = lambda a: pl.BlockSpec((None,) + a.shape[1:], lambda o, c: (l,) + (0,) * (a.ndim - 1))
    kern = functools.partial(_filter_kernel, nb=nb, row_chunk=row_chunk)
    kspec = pl.BlockSpec((None, nd, b, tc), lambda o, c: (o, 0, 0, c))
    blk = lambda dt: pltpu.VMEM((nb, b, tc), dt)
    return pl.pallas_call(
        kern,
        out_shape=(jax.ShapeDtypeStruct((2, nd, b, width), BF16),
                   jax.ShapeDtypeStruct((2, nd, b, width), BF16),
                   jax.ShapeDtypeStruct((2, nd, 1, width), F32)),
        grid=(2, nct),
        in_specs=[small(feats), small(tcol), layer(w1p), layer(b1), layer(w2), layer(b2),
                  layer(w3), layer(b3), layer(freq),
                  pl.BlockSpec((None, hid, tc), lambda o, c: (l, 0, o * 2 * nct + c)),
                  pl.BlockSpec((None, hid, tc), lambda o, c: (l, 0, o * 2 * nct + nct + c)),
                  pl.BlockSpec((1, tc), lambda o, c: (0, c)),
                  _const_spec((b, b)), _const_spec((b, b))],
        out_specs=(kspec, kspec, pl.BlockSpec((None, nd, 1, tc), lambda o, c: (o, 0, 0, c))),
        scratch_shapes=[pltpu.VMEM((L, hid), F32), blk(BF16), blk(BF16),
                        blk(F32), blk(F32), blk(F32), blk(F32)],
        compiler_params=_params(("arbitrary", "arbitrary")),
        name="hyena_filters",
    )(feats, tcol, w1p, b1, w2, b2, w3, b3, freq, w4, w4, deltas, cmat, smat)


def _sconv_chunk(u_ref, w_ref, b_ref, i, rows, nchunks):
    r0 = i * rows
    tc = u_ref.shape[1]
    u = u_ref[r0:r0 + rows, :]
    zero = jnp.zeros((1, tc), F32)
    up = u_ref[r0 - 8:r0, :][7:8, :] if i > 0 else zero
    dn = u_ref[r0 + rows:r0 + rows + 8, :][0:1, :] if i < nchunks - 1 else zero
    row = lax.broadcasted_iota(jnp.int32, (rows, 1), 0)
    prev = jnp.where(row == 0, up, pltpu.roll(u, 1, 0))
    nxt = jnp.where(row == rows - 1, dn, pltpu.roll(u, rows - 1, 0))
    return b_ref[...] + prev * w_ref[0:1, :] + u * w_ref[1:2, :] + nxt * w_ref[2:3, :]


def _hyena_kernel(v_ref, x1_ref, x2_ref, zg_ref, wv_ref, wx1_ref, wx2_ref, bv_ref, bx1_ref, bx2_ref,
                  hb_ref, c_ref, s_ref, kr_ref, ki_ref, kn_ref, o_ref,
                  u_ref, ub_ref, a_ref, bn_ref, *, nb, seq_chunk, freq_chunk):
    L, tc = u_ref.shape
    b = L // nb
    per_block = b // seq_chunk
    nseq = L // seq_chunk
    alt = _alt_sign(seq_chunk)

    def lanes(j):
        return slice(j * tc, (j + 1) * tc)

    def put_signal(chunk, val, nyq):
        j, local = divmod(chunk, per_block)
        rows = slice(chunk * seq_chunk, (chunk + 1) * seq_chunk)
        u_ref[rows, :] = val
        ub_ref[local * seq_chunk:(local + 1) * seq_chunk, lanes(j)] = val.astype(BF16)
        nyq[j] = nyq[j] + jnp.sum(val * alt, axis=0, keepdims=True)

    def forward(order):
        for fc in range(b // freq_chunk):
            r = slice(fc * freq_chunk, (fc + 1) * freq_chunk)
            ur = jnp.dot(c_ref[r, :], ub_ref[...], preferred_element_type=F32).astype(BF16)
            us = jnp.dot(s_ref[r, :], ub_ref[...], preferred_element_type=F32).astype(BF16)
            for i in range(nb):
                acc_a = acc_b = None
                for j in range(nb):
                    kr = kr_ref[order, i - j + nb - 1, r, :]
                    ki = ki_ref[order, i - j + nb - 1, r, :]
                    urj, usj = ur[:, lanes(j)], us[:, lanes(j)]
                    ta = urj * kr + usj * ki
                    tb = usj * kr - urj * ki
                    acc_a = ta if acc_a is None else acc_a + ta
                    acc_b = tb if acc_b is None else acc_b + tb
                a_ref[r, lanes(i)] = acc_a
                bn_ref[r, lanes(i)] = acc_b

    def inverse(order, nyq, finish):
        nyq_term = []
        for i in range(nb):
            acc = None
            for j in range(nb):
                t = nyq[j] * kn_ref[order, i - j + nb - 1]
                acc = t if acc is None else acc + t
            nyq_term.append(acc)
        for tcn in range(per_block):
            r = slice(tcn * seq_chunk, (tcn + 1) * seq_chunk)
            y2 = jnp.dot(c_ref[r, :], a_ref[...], preferred_element_type=F32)
            y2 = y2 + jnp.dot(s_ref[r, :], bn_ref[...], preferred_element_type=F32)
            for i in range(nb):
                chunk = i * per_block + tcn
                rows = slice(chunk * seq_chunk, (chunk + 1) * seq_chunk)
                y = y2[:, lanes(i)] + alt * nyq_term[i]
                finish(chunk, rows, y + u_ref[rows, :] * hb_ref[order:order + 1, :])

    zero_row = jnp.zeros((1, tc), F32)
    nyq0 = [zero_row] * nb
    for chunk in range(nseq):
        put_signal(chunk, _sconv_chunk(v_ref, wv_ref, bv_ref, chunk, seq_chunk, nseq), nyq0)

    forward(0)

    nyq1 = [zero_row] * nb

    def finish_z(chunk, rows, y):
        put_signal(chunk, _sconv_chunk(x1_ref, wx1_ref, bx1_ref, chunk, seq_chunk, nseq) * y, nyq1)

    inverse(0, nyq0, finish_z)
    forward(1)

    def finish_out(chunk, rows, y):
        y = _sconv_chunk(x2_ref, wx2_ref, bx2_ref, chunk, seq_chunk, nseq) * y
        zg = zg_ref[rows, :]
        o_ref[rows, :] = (y * (zg * jax.nn.sigmoid(zg))).astype(o_ref.dtype)

    inverse(1, nyq1, finish_out)


def _hyena(proj3, conv_w, conv_b, hyena_bias, cmat, smat, kr, ki, kn, l, nb, tc=256,
           seq_chunk=512, freq_chunk=512):
    B, L, _ = proj3.shape
    b = L // nb
    nd = 2 * nb - 1
    width = hyena_bias.shape[2]
    nct = width // tc
    sig = lambda part: pl.BlockSpec((None, L, tc), lambda j, bb: (bb, 0, part * nct + j))
    cw = lambda part: pl.BlockSpec((None, 3, tc), lambda j, bb: (l, 0, part * nct + j))
    cb = lambda part: pl.BlockSpec((None, 1, tc), lambda j, bb: (l, 0, part * nct + j))
    kspec = pl.BlockSpec((2, nd, b, tc), lambda j, bb: (0, 0, 0, j))
    kern = functools.partial(_hyena_kernel, nb=nb, seq_chunk=seq_chunk, freq_chunk=freq_chunk)
    return pl.pallas_call(
        kern,
        out_shape=jax.ShapeDtypeStruct((B, L, width), BF16),
        grid=(nct, B),
        in_specs=[sig(0), sig(1), sig(2), sig(3),
                  cw(0), cw(1), cw(2), cb(0), cb(1), cb(2),
                  pl.BlockSpec((None, 2, tc), lambda j, bb: (l, 0, j)),
                  _const_spec((b, b)), _const_spec((b, b)),
                  kspec, kspec,
                  pl.BlockSpec((2, nd, 1, tc), lambda j, bb: (0, 0, 0, j))],
        out_specs=pl.BlockSpec((None, L, tc), lambda j, bb: (bb, 0, j)),
        scratch_shapes=[pltpu.VMEM((L, tc), F32), pltpu.VMEM((b, nb * tc), BF16),
                        pltpu.VMEM((b, nb * tc), BF16), pltpu.VMEM((b, nb * tc), BF16)],
        compiler_params=_params(("parallel", "arbitrary")),
        name="hyena_mixer",
    )(proj3, proj3, proj3, proj3, conv_w, conv_w, conv_w, conv_b, conv_b, conv_b,
      hyena_bias, cmat, smat, kr, ki, kn)


def _rope(x, cos, sin_lo, sin_hi):
    return (x * cos + pltpu.roll(x, HEAD_DIM - ROPE_HALF, 1) * sin_lo
            + pltpu.roll(x, ROPE_HALF, 1) * sin_hi)


def _attn_kernel(sink_ref, q_ref, k_ref, v_ref, zg_ref, cos_ref, slo_ref, shi_ref, o_ref,
                 kb_ref, vt_ref, *, layer):
    L = k_ref.shape[0]
    qb = q_ref.shape[0]
    kw = QSUB + 2 * WINDOW
    gq = GROUP * QSUB
    kvh = pl.program_id(1)
    n = pl.program_id(2)

    @pl.when(n == 0)
    def _():
        kb_ref[...] = _rope(k_ref[...], cos_ref[...], slo_ref[...], shi_ref[...]).astype(BF16)
        vt_ref[...] = v_ref[...].T.astype(BF16)

    log2e = math.log2(math.e)
    lane_head = lax.broadcasted_iota(jnp.int32, (1, gq), 1) // QSUB
    sk2 = jnp.zeros((1, gq), F32)
    for g in range(GROUP):
        sk2 = jnp.where(lane_head == g, sink_ref[layer, kvh * GROUP + g] * log2e, sk2)

    for sb in range(qb // QSUB):
        rows = slice(sb * QSUB, (sb + 1) * QSUB)
        q0 = pl.multiple_of(n * qb + sb * QSUB, QSUB)
        start = pl.multiple_of(jnp.clip(q0 - WINDOW, 0, L - kw), WINDOW)
        win = pl.ds(start, kw)
        tab = pl.ds(q0, QSUB)
        cq, slq, shq = cos_ref[tab, :], slo_ref[tab, :], shi_ref[tab, :]
        q = jnp.concatenate(
            [_rope(q_ref[rows, g * HEAD_DIM:(g + 1) * HEAD_DIM], cq, slq, shq).astype(BF16)
             for g in range(GROUP)], axis=0)
        raw = lax.dot_general(kb_ref[win, :], q, (((1,), (1,)), ((), ())),
                              preferred_element_type=F32)
        kpos = start + lax.broadcasted_iota(jnp.int32, (kw, 1), 0)
        qpos = q0 + lax.broadcasted_iota(jnp.int32, (1, QSUB), 1)
        cap = jnp.where(jnp.abs(kpos - qpos) <= WINDOW, jnp.inf, -jnp.inf)
        raw = jnp.minimum(raw, jnp.concatenate([cap] * GROUP, axis=1))
        c = (HEAD_DIM ** -0.5) * log2e
        m2 = jnp.maximum(jnp.max(raw, axis=0, keepdims=True) * c, sk2)
        p = jnp.exp2(raw * c - m2)
        denom = jnp.sum(p, axis=0, keepdims=True) + jnp.exp2(sk2 - m2)
        ot = jnp.dot(vt_ref[:, win], p.astype(BF16), preferred_element_type=F32) / denom
        for g in range(GROUP):
            cols = slice(g * HEAD_DIM, (g + 1) * HEAD_DIM)
            o = ot[:, g * QSUB:(g + 1) * QSUB].T
            zg = zg_ref[rows, cols]
            o_ref[rows, cols] = (o * (zg * jax.nn.sigmoid(zg))).astype(o_ref.dtype)


def _attention(proj3, sink, rope_tabs, col_q, col_k, col_v, col_zg, l, qb=512):
    B, L, _ = proj3.shape
    gw = GROUP * HEAD_DIM
    cos_t, sin_lo, sin_hi = rope_tabs
    tab = pl.BlockSpec((L, HEAD_DIM), lambda b, h, n: (0, 0))
    return pl.pallas_call(
        functools.partial(_attn_kernel, layer=l),
        out_shape=jax.ShapeDtypeStruct((B, L, N_HEADS * HEAD_DIM), BF16),
        grid=(B, N_KV_HEADS, L // qb),
        in_specs=[pl.BlockSpec(memory_space=pltpu.SMEM),
                  pl.BlockSpec((None, qb, gw), lambda b, h, n: (b, n, col_q // gw + h)),
                  pl.BlockSpec((None, L, HEAD_DIM), lambda b, h, n: (b, 0, col_k // HEAD_DIM + h)),
                  pl.BlockSpec((None, L, HEAD_DIM), lambda b, h, n: (b, 0, col_v // HEAD_DIM + h)),
                  pl.BlockSpec((None, qb, gw), lambda b, h, n: (b, n, col_zg // gw + h)),
                  tab, tab, tab],
        out_specs=pl.BlockSpec((None, qb, gw), lambda b, h, n: (b, n, h)),
        scratch_shapes=[pltpu.VMEM((L, HEAD_DIM), BF16), pltpu.VMEM((HEAD_DIM, L), BF16)],
        compiler_params=_params(("parallel", "parallel", "arbitrary")),
        name="window_attention",
    )(sink, proj3, proj3, proj3, proj3, cos_t, sin_lo, sin_hi)


def _merge_out_kernel(*refs, n_gate_blocks, final):
    x_ref, yh_ref, ya_ref = refs[0:3]
    gh_refs = refs[3:3 + n_gate_blocks]
    ga_refs = refs[3 + n_gate_blocks:3 + 2 * n_gate_blocks]
    who_ref, wao_ref, wout_ref, fg_ref, o_ref, m_ref = refs[3 + 2 * n_gate_blocks:]
    gw = gh_refs[0].shape[1]
    yh = yh_ref[...]
    ya = ya_ref[...]
    for c in range(n_gate_blocks):
        cols = slice(c * gw, (c + 1) * gw)
        ph = jnp.dot(yh, who_ref[:, cols], preferred_element_type=F32)
        pa = jnp.dot(ya, wao_ref[:, cols], preferred_element_type=F32)
        merged = jax.nn.sigmoid(gh_refs[c][...]) * ph + jax.nn.sigmoid(ga_refs[c][...]) * pa
        m_ref[:, cols] = merged.astype(BF16)
    out = x_ref[...] + jnp.dot(m_ref[...], wout_ref[...], preferred_element_type=F32)
    if final:
        ms = jnp.mean(out * out, axis=-1, keepdims=True)
        out = out * lax.rsqrt(ms + EPS) * fg_ref[...]
    o_ref[...] = out


def _merge_out(x2d, proj, yh, ya, who, wao, wout, final_g, col_gh, col_ga, l, final, tm=256, gw=512):
    m, d = x2d.shape
    width = yh.shape[1]
    ngb = d // gw
    gspec = lambda col0, c: pl.BlockSpec((tm, gw), lambda i: (i, col0 // gw + c))
    wspec = lambda rows: pl.BlockSpec((None, rows, d), lambda i: (l, 0, 0), pipeline_mode=pl.Buffered(1))
    kern = functools.partial(_merge_out_kernel, n_gate_blocks=ngb, final=final)
    return pl.pallas_call(
        kern,
        out_shape=jax.ShapeDtypeStruct((m, d), F32),
        grid=(m // tm,),
        in_specs=[pl.BlockSpec((tm, d), lambda i: (i, 0)),
                  pl.BlockSpec((tm, width), lambda i: (i, 0)),
                  pl.BlockSpec((tm, width), lambda i: (i, 0))]
                 + [gspec(col_gh, c) for c in range(ngb)]
                 + [gspec(col_ga, c) for c in range(ngb)]
                 + [wspec(width), wspec(width), wspec(d),
                    pl.BlockSpec((1, d), lambda i: (0, 0))],
        out_specs=pl.BlockSpec((tm, d), lambda i: (i, 0)),
        scratch_shapes=[pltpu.VMEM((tm, d), BF16)],
        compiler_params=_params(("parallel",)),
        name="merge_out",
    )(x2d, yh, ya, *([proj] * (2 * ngb)), who, wao, wout, final_g.reshape(1, d))


def _dft_mats(b):
    idx = jnp.arange(b, dtype=jnp.int32)
    k = (idx[:, None] * idx[None, :]) % (2 * b)
    ang = k.astype(F32) * (math.pi / b)
    return jnp.cos(ang).astype(BF16), jnp.sin(ang).astype(BF16)


def _rope_tabs(L):
    inv = ROPE_THETA ** (-jnp.arange(0, ROPE_DIM, 2, dtype=F32) / ROPE_DIM)
    ang = jnp.arange(L, dtype=F32)[:, None] * inv[None, :]
    cos, sin = jnp.cos(ang), jnp.sin(ang)
    ones = jnp.ones((L, HEAD_DIM - ROPE_DIM), F32)
    zeros = jnp.zeros((L, HEAD_DIM - ROPE_HALF), F32)
    cos_t = jnp.concatenate([cos, cos, ones], axis=1)
    sin_lo = jnp.concatenate([-sin, zeros], axis=1)
    sin_hi = jnp.concatenate([jnp.zeros((L, ROPE_HALF), F32), sin, zeros[:, ROPE_HALF:]], axis=1)
    return cos_t, sin_lo, sin_hi


def _filter_feats(L):
    t = jnp.linspace(0.0, 1.0, L, dtype=F32)[:, None]
    bands = jnp.linspace(1e-4, FILTER_BANDS - 1, FILTER_BANDS, dtype=F32)[None, :]
    ang = (2.0 * math.pi / L) * jnp.arange(L, dtype=F32)[:, None] * bands
    feats = jnp.concatenate([t, jnp.cos(ang), -jnp.sin(ang)], axis=-1)
    feats = jnp.pad(feats, ((0, 0), (0, FEAT_PAD - feats.shape[1])))
    return feats, t


def kernel(x, norm_g, w_in, conv_w, conv_b, filt_w1, filt_b1, filt_w2, filt_b2, filt_w3, filt_b3,
           filt_w4, filt_freq, hyena_bias, attn_sink, w_hyena_out, w_attn_out, w_out, final_norm):
    B, L, D = x.shape
    depth = norm_g.shape[0]
    hw = hyena_bias.shape[2]
    aw = N_HEADS * HEAD_DIM
    kvw = N_KV_HEADS * HEAD_DIM
    sizes = (3 * hw, hw, aw, kvw, kvw, aw, D, D)
    cols = [0]
    for s in sizes:
        cols.append(cols[-1] + s)
    col_zhy, col_q, col_k, col_v, col_zat, col_gh, col_ga = cols[1:8]

    cmat, smat = _dft_mats(L // NB)
    rope_tabs = _rope_tabs(L)
    feats, tcol = _filter_feats(L)
    deltas = jnp.abs(jnp.linspace(MIN_DECAY, MAX_DECAY, hw, dtype=F32))[None, :]

    w_in_b, who_b, wao_b, wout_b = (w.astype(BF16) for w in (w_in, w_hyena_out, w_attn_out, w_out))
    row = lambda a: a[:, None, :]
    w1p = jnp.pad(filt_w1, ((0, 0), (0, FEAT_PAD - filt_w1.shape[1]), (0, 0)))
    xf = x.reshape(B * L, D)
    for l in range(depth):
        proj = _norm_proj(xf, row(norm_g), w_in_b, l)
        proj3 = proj.reshape(B, L, -1)
        kr, ki, kn = _filters(feats, tcol, w1p, row(filt_b1), filt_w2, row(filt_b2), filt_w3, row(filt_b3),
                              row(filt_freq), filt_w4, deltas, cmat, smat, l, NB)
        y_hy = _hyena(proj3, conv_w, row(conv_b), hyena_bias, cmat, smat, kr, ki, kn, l, NB)
        y_at = _attention(proj3, attn_sink, rope_tabs, col_q, col_k, col_v, col_zat, l)
        xf = _merge_out(xf, proj, y_hy.reshape(B * L, hw), y_at.reshape(B * L, aw),
                        who_b, wao_b, wout_b, final_norm, col_gh, col_ga, l, final=(l == depth - 1))
    return xf.reshape(B, L, D)
```

```python
import functools
import math

import jax
import jax.numpy as jnp
from jax import lax
from jax.experimental import pallas as pl
from jax.experimental.pallas import tpu as pltpu

F32 = jnp.float32
BF16 = jnp.bfloat16

HEAD_DIM = 128
N_HEADS = 8
N_KV_HEADS = 2
GROUP = N_HEADS // N_KV_HEADS
WINDOW = 128
QSUB = 128
ROPE_THETA = 500000.0
ROPE_DIM = HEAD_DIM // 4
ROPE_HALF = ROPE_DIM // 2
EPS = 1e-6
FILTER_BANDS = 16
FEAT_PAD = 128
DECAY_TARGET = 1e-2
MIN_DECAY = math.log(DECAY_TARGET) / 0.3
MAX_DECAY = math.log(DECAY_TARGET) / 1.5

NB = 4
BF16_ROWS = 16
VMEM_LIMIT = 56 * 1024 * 1024


def _params(sem, vmem=VMEM_LIMIT):
    return pltpu.CompilerParams(dimension_semantics=sem, vmem_limit_bytes=vmem)


def _const_spec(shape):
    return pl.BlockSpec(shape, lambda *_: (0,) * len(shape), pipeline_mode=pl.Buffered(1))


def _alt_sign(rows):
    return jnp.where((lax.broadcasted_iota(jnp.int32, (rows, 1), 0) & 1) == 0, 1.0, -1.0)


def _norm_proj_kernel(x_ref, g_ref, w_ref, o_ref, h_ref):
    @pl.when(pl.program_id(1) == 0)
    def _():
        x = x_ref[...]
        ms = jnp.mean(x * x, axis=-1, keepdims=True)
        h_ref[...] = (x * lax.rsqrt(ms + EPS) * g_ref[...]).astype(BF16)

    o_ref[...] = jnp.dot(h_ref[...], w_ref[...], preferred_element_type=F32)


def _norm_proj(x2d, g_all, w_all, l, tm=1024, tn=1536):
    m, d = x2d.shape
    n = w_all.shape[2]
    return pl.pallas_call(
        _norm_proj_kernel,
        out_shape=jax.ShapeDtypeStruct((m, n), F32),
        grid=(m // tm, n // tn),
        in_specs=[
            pl.BlockSpec((tm, d), lambda i, j: (i, 0)),
            pl.BlockSpec((None, 1, d), lambda i, j: (l, 0, 0)),
            pl.BlockSpec((None, d, tn), lambda i, j: (l, 0, j)),
        ],
        out_specs=pl.BlockSpec((tm, tn), lambda i, j: (i, j)),
        scratch_shapes=[pltpu.VMEM((tm, d), BF16)],
        compiler_params=_params(("parallel", "arbitrary")),
        name="norm_proj",
    )(x2d, g_all, w_all)


def _filter_kernel(feats_ref, t_ref, w1_ref, b1_ref, w2_ref, b2_ref, w3_ref, b3_ref, fr_ref,
                   w4f_ref, w4b_ref, dl_ref, c_ref, s_ref,
                   kr_ref, ki_ref, kn_ref,
                   hdn_ref, xf_ref, xb_ref, cf_ref, sf_ref, cb_ref, sb_ref, *, nb, row_chunk):
    L = feats_ref.shape[0]
    b = L // nb
    tc = dl_ref.shape[1]
    hp = lax.Precision.HIGHEST
    nchunks = b // row_chunk
    alt = _alt_sign(row_chunk)

    @pl.when((pl.program_id(0) == 0) & (pl.program_id(1) == 0))
    def _():
        fr = fr_ref[...]
        h = jnp.sin(fr * (jnp.dot(feats_ref[...], w1_ref[...], precision=hp,
                                  preferred_element_type=F32) + b1_ref[...]))
        h = jnp.sin(fr * (jnp.dot(h, w2_ref[...], precision=hp,
                                  preferred_element_type=F32) + b2_ref[...]))
        h = jnp.sin(fr * (jnp.dot(h, w3_ref[...], precision=hp,
                                  preferred_element_type=F32) + b3_ref[...]))
        hdn_ref[...] = h

    def taps(rows):
        h = hdn_ref[rows, :]
        decay = jnp.exp(-t_ref[rows, :] * dl_ref[...])
        fwd = jnp.dot(h, w4f_ref[...], precision=hp, preferred_element_type=F32) * decay
        bwd = jnp.dot(h, w4b_ref[...], precision=hp, preferred_element_type=F32) * decay
        return fwd, bwd

    zero_row = jnp.zeros((1, tc), F32)
    f0, b0, af, ab = [], [], [], []
    for q in range(nb):
        head_f, head_b = taps(pl.ds(q * b, 8))
        f0.append(head_f[0:1, :])
        b0.append(head_b[0:1, :])

        def tap_chunk(i, carry, q=q):
            l0 = pl.multiple_of(i * row_chunk, row_chunk)
            fwd, bwd = taps(pl.ds(pl.multiple_of(q * b + l0, row_chunk), row_chunk))
            first = (lax.broadcasted_iota(jnp.int32, (row_chunk, 1), 0) + l0) == 0
            fz = jnp.where(first, 0.0, fwd)
            bz = jnp.where(first, 0.0, bwd)
            xf_ref[q, pl.ds(l0, row_chunk), :] = fz.astype(BF16)
            xb_ref[q, pl.ds(l0, row_chunk), :] = bz.astype(BF16)
            return (carry[0] + jnp.sum(fz * alt, axis=0, keepdims=True),
                    carry[1] + jnp.sum(bz * alt, axis=0, keepdims=True))

        a_f, a_b = lax.fori_loop(0, nchunks, tap_chunk, (zero_row, zero_row))
        af.append(a_f)
        ab.append(a_b)

        def spec_chunk(i, carry, q=q):
            r = pl.ds(pl.multiple_of(i * row_chunk, row_chunk), row_chunk)
            cf_ref[q, r, :] = jnp.dot(c_ref[r, :], xf_ref[q], preferred_element_type=F32)
            sf_ref[q, r, :] = jnp.dot(s_ref[r, :], xf_ref[q], preferred_element_type=F32)
            cb_ref[q, r, :] = jnp.dot(c_ref[r, :], xb_ref[q], preferred_element_type=F32)
            sb_ref[q, r, :] = jnp.dot(s_ref[r, :], xb_ref[q], preferred_element_type=F32)
            return carry

        lax.fori_loop(0, nchunks, spec_chunk, 0)

    inv_n = 1.0 / (2 * b)
    for d in range(-(nb - 1), nb):
        slot = d + nb - 1
        e = -d
        if d >= 1:
            kn = af[d] + f0[d] + af[d - 1]
        elif d == 0:
            kn = af[0] + f0[0] + ab[0]
        else:
            kn = b0[e] + ab[e] + ab[e - 1]
        kn_ref[slot] = kn * inv_n

        def combine(i, carry, d=d, e=e, slot=slot):
            r0 = pl.multiple_of(i * row_chunk, row_chunk)
            r = pl.ds(r0, row_chunk)
            first = (lax.broadcasted_iota(jnp.int32, (row_chunk, 1), 0) + r0) == 0
            wgt = jnp.where(first, inv_n, 2.0 * inv_n)
            if d >= 1:
                kr = cf_ref[d, r, :] + f0[d] + alt * cf_ref[d - 1, r, :]
                ki = -sf_ref[d, r, :] - alt * sf_ref[d - 1, r, :]
            elif d == 0:
                kr = cf_ref[0, r, :] + f0[0] + cb_ref[0, r, :]
                ki = sb_ref[0, r, :] - sf_ref[0, r, :]
            else:
                kr = b0[e] + cb_ref[e, r, :] + alt * cb_ref[e - 1, r, :]
                ki = sb_ref[e, r, :] + alt * sb_ref[e - 1, r, :]
            kr_ref[slot, r, :] = (kr * wgt).astype(kr_ref.dtype)
            ki_ref[slot, r, :] = (ki * wgt).astype(ki_ref.dtype)
            return carry

        lax.fori_loop(0, nchunks, combine, 0)


def _filters(feats, tcol, w1p, b1, w2, b2, w3, b3, freq, w4, deltas, cmat, smat, l, nb, tc=256,
             row_chunk=512):
    L = feats.shape[0]
    b = L // nb
    nd = 2 * nb - 1
    width = deltas.shape[1]
    nct = width // tc
    hid = w2.shape[1]
    small = lambda a: pl.BlockSpec(a.shape, lambda o, c: (0,) * a.ndim)
    layer = lambda a: pl.BlockSpec((None,) + a.shape[1:], lambda o, c: (l,) + (0,) * (a.ndim - 1))
    kern = functools.partial(_filter_kernel, nb=nb, row_chunk=row_chunk)
    kspec = pl.BlockSpec((None, nd, b, tc), lambda o, c: (o, 0, 0, c))
    blk = lambda dt: pltpu.VMEM((nb, b, tc), dt)
    return pl.pallas_call(
        kern,
        out_shape=(jax.ShapeDtypeStruct((2, nd, b, width), BF16),
                   jax.ShapeDtypeStruct((2, nd, b, width), BF16),
                   jax.ShapeDtypeStruct((2, nd, 1, width), F32)),
        grid=(2, nct),
        in_specs=[small(feats), small(tcol), layer(w1p), layer(b1), layer(w2), layer(b2),
                  layer(w3), layer(b3), layer(freq),
                  pl.BlockSpec((None, hid, tc), lambda o, c: (l, 0, o * 2 * nct + c)),
                  pl.BlockSpec((None, hid, tc), lambda o, c: (l, 0, o * 2 * nct + nct + c)),
                  pl.BlockSpec((1, tc), lambda o, c: (0, c)),
                  _const_spec((b, b)), _const_spec((b, b))],
        out_specs=(kspec, kspec, pl.BlockSpec((None, nd, 1, tc), lambda o, c: (o, 0, 0, c))),
        scratch_shapes=[pltpu.VMEM((L, hid), F32), blk(BF16), blk(BF16),
                        blk(F32), blk(F32), blk(F32), blk(F32)],
        compiler_params=_params(("arbitrary", "arbitrary")),
        name="hyena_filters",
    )(feats, tcol, w1p, b1, w2, b2, w3, b3, freq, w4, w4, deltas, cmat, smat)


def _sconv_chunk(u_ref, w_ref, b_ref, i, rows, nchunks):
    r0 = i * rows
    tc = u_ref.shape[1]
    u = u_ref[r0:r0 + rows, :]
    zero = jnp.zeros((1, tc), F32)
    up = u_ref[r0 - 8:r0, :][7:8, :] if i > 0 else zero
    dn = u_ref[r0 + rows:r0 + rows + 8, :][0:1, :] if i < nchunks - 1 else zero
    row = lax.broadcasted_iota(jnp.int32, (rows, 1), 0)
    prev = jnp.where(row == 0, up, pltpu.roll(u, 1, 0))
    nxt = jnp.where(row == rows - 1, dn, pltpu.roll(u, rows - 1, 0))
    return b_ref[...] + prev * w_ref[0:1, :] + u * w_ref[1:2, :] + nxt * w_ref[2:3, :]


def _hyena_kernel(v_ref, x1_ref, x2_ref, zg_ref, wv_ref, wx1_ref, wx2_ref, bv_ref, bx1_ref, bx2_ref,
                  hb_ref, c_ref, sf_ref, si_ref, kr_ref, ki_ref, kn_ref, o_ref,
                  u_ref, ub_ref, a_ref, bn_ref, *, nb, seq_chunk, freq_chunk):
    L, tc = u_ref.shape
    b = L // nb
    per_block = b // seq_chunk
    nseq = L // seq_chunk

    def lanes(j):
        return slice(j * tc, (j + 1) * tc)

    def put_signal(chunk, val):
        j, local = divmod(chunk, per_block)
        rows = slice(chunk * seq_chunk, (chunk + 1) * seq_chunk)
        u_ref[rows, :] = val
        ub_ref[local * seq_chunk:(local + 1) * seq_chunk, lanes(j)] = val.astype(BF16)

    def forward(order):
        for fc in range(b // freq_chunk):
            r = slice(fc * freq_chunk, (fc + 1) * freq_chunk)
            ur = jnp.dot(c_ref[r, :], ub_ref[...], preferred_element_type=F32).astype(BF16)
            us = jnp.dot(sf_ref[r, :], ub_ref[...], preferred_element_type=F32).astype(BF16)
            for i in range(nb):
                acc_a = acc_b = None
                for j in range(nb):
                    kr = kr_ref[order, i - j + nb - 1, r, :]
                    ki = ki_ref[order, i - j + nb - 1, r, :]
                    urj, usj = ur[:, lanes(j)], us[:, lanes(j)]
                    ta = urj * kr + usj * ki
                    tb = usj * kr - urj * ki
                    acc_a = ta if acc_a is None else acc_a + ta
                    acc_b = tb if acc_b is None else acc_b + tb
                if fc == 0:
                    nyq = None
                    for j in range(nb):
                        t = us[0:1, lanes(j)].astype(F32) * kn_ref[order, i - j + nb - 1]
                        nyq = t if nyq is None else nyq + t
                    first = lax.broadcasted_iota(jnp.int32, (BF16_ROWS, 1), 0) == 0
                    top = jnp.where(first, nyq.astype(BF16), acc_b[:BF16_ROWS])
                    acc_b = jnp.concatenate([top, acc_b[BF16_ROWS:]], axis=0)
                a_ref[r, lanes(i)] = acc_a
                bn_ref[r, lanes(i)] = acc_b

    def inverse(order, finish):
        for tcn in range(per_block):
            r = slice(tcn * seq_chunk, (tcn + 1) * seq_chunk)
            y2 = jnp.dot(c_ref[r, :], a_ref[...], preferred_element_type=F32)
            y2 = y2 + jnp.dot(si_ref[r, :], bn_ref[...], preferred_element_type=F32)
            for i in range(nb):
                chunk = i * per_block + tcn
                rows = slice(chunk * seq_chunk, (chunk + 1) * seq_chunk)
                finish(chunk, rows, y2[:, lanes(i)] + u_ref[rows, :] * hb_ref[order:order + 1, :])

    for chunk in range(nseq):
        put_signal(chunk, _sconv_chunk(v_ref, wv_ref, bv_ref, chunk, seq_chunk, nseq))

    forward(0)

    def finish_z(chunk, rows, y):
        put_signal(chunk, _sconv_chunk(x1_ref, wx1_ref, bx1_ref, chunk, seq_chunk, nseq) * y)

    inverse(0, finish_z)
    forward(1)

    def finish_out(chunk, rows, y):
        y = _sconv_chunk(x2_ref, wx2_ref, bx2_ref, chunk, seq_chunk, nseq) * y
        zg = zg_ref[rows, :]
        o_ref[rows, :] = (y * (zg * jax.nn.sigmoid(zg))).astype(o_ref.dtype)

    inverse(1, finish_out)


def _hyena(proj3, conv_w, conv_b, hyena_bias, cmat, smat_fwd, smat_inv, kr, ki, kn, l, nb, tc=256,
           seq_chunk=512, freq_chunk=512):
    B, L, _ = proj3.shape
    b = L // nb
    nd = 2 * nb - 1
    width = hyena_bias.shape[2]
    nct = width // tc
    sig = lambda part: pl.BlockSpec((None, L, tc), lambda j, bb: (bb, 0, part * nct + j))
    cw = lambda part: pl.BlockSpec((None, 3, tc), lambda j, bb: (l, 0, part * nct + j))
    cb = lambda part: pl.BlockSpec((None, 1, tc), lambda j, bb: (l, 0, part * nct + j))
    kspec = pl.BlockSpec((2, nd, b, tc), lambda j, bb: (0, 0, 0, j))
    kern = functools.partial(_hyena_kernel, nb=nb, seq_chunk=seq_chunk, freq_chunk=freq_chunk)
    return pl.pallas_call(
        kern,
        out_shape=jax.ShapeDtypeStruct((B, L, width), BF16),
        grid=(nct, B),
        in_specs=[sig(0), sig(1), sig(2), sig(3),
                  cw(0), cw(1), cw(2), cb(0), cb(1), cb(2),
                  pl.BlockSpec((None, 2, tc), lambda j, bb: (l, 0, j)),
                  _const_spec((b, b)), _const_spec((b, b)), _const_spec((b, b)),
                  kspec, kspec,
                  pl.BlockSpec((2, nd, 1, tc), lambda j, bb: (0, 0, 0, j))],
        out_specs=pl.BlockSpec((None, L, tc), lambda j, bb: (bb, 0, j)),
        scratch_shapes=[pltpu.VMEM((L, tc), F32), pltpu.VMEM((b, nb * tc), BF16),
                        pltpu.VMEM((b, nb * tc), BF16), pltpu.VMEM((b, nb * tc), BF16)],
        compiler_params=_params(("parallel", "arbitrary")),
        name="hyena_mixer",
    )(proj3, proj3, proj3, proj3, conv_w, conv_w, conv_w, conv_b, conv_b, conv_b,
      hyena_bias, cmat, smat_fwd, smat_inv, kr, ki, kn)


def _rope(x, cos, sin_lo, sin_hi):
    return (x * cos + pltpu.roll(x, HEAD_DIM - ROPE_HALF, 1) * sin_lo
            + pltpu.roll(x, ROPE_HALF, 1) * sin_hi)


def _attn_kernel(sink_ref, q_ref, k_ref, v_ref, zg_ref, cos_ref, slo_ref, shi_ref, o_ref,
                 kb_ref, vt_ref, *, layer):
    L = k_ref.shape[0]
    qb = q_ref.shape[0]
    kw = QSUB + 2 * WINDOW
    gq = GROUP * QSUB
    kvh = pl.program_id(1)
    n = pl.program_id(2)

    @pl.when(n == 0)
    def _():
        kb_ref[...] = _rope(k_ref[...], cos_ref[...], slo_ref[...], shi_ref[...]).astype(BF16)
        vt_ref[...] = v_ref[...].T.astype(BF16)

    log2e = math.log2(math.e)
    lane_head = lax.broadcasted_iota(jnp.int32, (1, gq), 1) // QSUB
    sk2 = jnp.zeros((1, gq), F32)
    for g in range(GROUP):
        sk2 = jnp.where(lane_head == g, sink_ref[layer, kvh * GROUP + g] * log2e, sk2)

    for sb in range(qb // QSUB):
        rows = slice(sb * QSUB, (sb + 1) * QSUB)
        q0 = pl.multiple_of(n * qb + sb * QSUB, QSUB)
        start = pl.multiple_of(jnp.clip(q0 - WINDOW, 0, L - kw), WINDOW)
        win = pl.ds(start, kw)
        tab = pl.ds(q0, QSUB)
        cq, slq, shq = cos_ref[tab, :], slo_ref[tab, :], shi_ref[tab, :]
        q = jnp.concatenate(
            [_rope(q_ref[rows, g * HEAD_DIM:(g + 1) * HEAD_DIM], cq, slq, shq).astype(BF16)
             for g in range(GROUP)], axis=0)
        raw = lax.dot_general(kb_ref[win, :], q, (((1,), (1,)), ((), ())),
                              preferred_element_type=F32)
        kpos = start + lax.broadcasted_iota(jnp.int32, (kw, 1), 0)
        qpos = q0 + lax.broadcasted_iota(jnp.int32, (1, QSUB), 1)
        cap = jnp.where(jnp.abs(kpos - qpos) <= WINDOW, jnp.inf, -jnp.inf)
        raw = jnp.minimum(raw, jnp.concatenate([cap] * GROUP, axis=1))
        c = (HEAD_DIM ** -0.5) * log2e
        m2 = jnp.maximum(jnp.max(raw, axis=0, keepdims=True) * c, sk2)
        p = jnp.exp2(raw * c - m2)
        denom = jnp.sum(p, axis=0, keepdims=True) + jnp.exp2(sk2 - m2)
        ot = jnp.dot(vt_ref[:, win], p.astype(BF16), preferred_element_type=F32) / denom
        for g in range(GROUP):
            cols = slice(g * HEAD_DIM, (g + 1) * HEAD_DIM)
            o = ot[:, g * QSUB:(g + 1) * QSUB].T
            zg = zg_ref[rows, cols]
            o_ref[rows, cols] = (o * (zg * jax.nn.sigmoid(zg))).astype(o_ref.dtype)


def _attention(proj3, sink, rope_tabs, col_q, col_k, col_v, col_zg, l, qb=1024):
    B, L, _ = proj3.shape
    gw = GROUP * HEAD_DIM
    cos_t, sin_lo, sin_hi = rope_tabs
    tab = pl.BlockSpec((L, HEAD_DIM), lambda b, h, n: (0, 0))
    return pl.pallas_call(
        functools.partial(_attn_kernel, layer=l),
        out_shape=jax.ShapeDtypeStruct((B, L, N_HEADS * HEAD_DIM), BF16),
        grid=(B, N_KV_HEADS, L // qb),
        in_specs=[pl.BlockSpec(memory_space=pltpu.SMEM),
                  pl.BlockSpec((None, qb, gw), lambda b, h, n: (b, n, col_q // gw + h)),
                  pl.BlockSpec((None, L, HEAD_DIM), lambda b, h, n: (b, 0, col_k // HEAD_DIM + h)),
                  pl.BlockSpec((None, L, HEAD_DIM), lambda b, h, n: (b, 0, col_v // HEAD_DIM + h)),
                  pl.BlockSpec((None, qb, gw), lambda b, h, n: (b, n, col_zg // gw + h)),
                  tab, tab, tab],
        out_specs=pl.BlockSpec((None, qb, gw), lambda b, h, n: (b, n, h)),
        scratch_shapes=[pltpu.VMEM((L, HEAD_DIM), BF16), pltpu.VMEM((HEAD_DIM, L), BF16)],
        compiler_params=_params(("parallel", "parallel", "arbitrary")),
        name="window_attention",
    )(sink, proj3, proj3, proj3, proj3, cos_t, sin_lo, sin_hi)


def _merge_out_kernel(*refs, n_gate_blocks, final):
    x_ref, yh_ref, ya_ref = refs[0:3]
    gh_refs = refs[3:3 + n_gate_blocks]
    ga_refs = refs[3 + n_gate_blocks:3 + 2 * n_gate_blocks]
    who_ref, wao_ref, wout_ref, fg_ref, o_ref, m_ref = refs[3 + 2 * n_gate_blocks:]
    gw = gh_refs[0].shape[1]
    yh = yh_ref[...]
    ya = ya_ref[...]
    for c in range(n_gate_blocks):
        cols = slice(c * gw, (c + 1) * gw)
        ph = jnp.dot(yh, who_ref[:, cols], preferred_element_type=F32)
        pa = jnp.dot(ya, wao_ref[:, cols], preferred_element_type=F32)
        merged = jax.nn.sigmoid(gh_refs[c][...]) * ph + jax.nn.sigmoid(ga_refs[c][...]) * pa
        m_ref[:, cols] = merged.astype(BF16)
    out = x_ref[...] + jnp.dot(m_ref[...], wout_ref[...], preferred_element_type=F32)
    if final:
        ms = jnp.mean(out * out, axis=-1, keepdims=True)
        out = out * lax.rsqrt(ms + EPS) * fg_ref[...]
    o_ref[...] = out


def _merge_out(x2d, proj, yh, ya, who, wao, wout, final_g, col_gh, col_ga, l, final, tm=256, gw=512):
    m, d = x2d.shape
    width = yh.shape[1]
    ngb = d // gw
    gspec = lambda col0, c: pl.BlockSpec((tm, gw), lambda i: (i, col0 // gw + c))
    wspec = lambda rows: pl.BlockSpec((None, rows, d), lambda i: (l, 0, 0), pipeline_mode=pl.Buffered(1))
    kern = functools.partial(_merge_out_kernel, n_gate_blocks=ngb, final=final)
    return pl.pallas_call(
        kern,
        out_shape=jax.ShapeDtypeStruct((m, d), F32),
        grid=(m // tm,),
        in_specs=[pl.BlockSpec((tm, d), lambda i: (i, 0)),
                  pl.BlockSpec((tm, width), lambda i: (i, 0)),
                  pl.BlockSpec((tm, width), lambda i: (i, 0))]
                 + [gspec(col_gh, c) for c in range(ngb)]
                 + [gspec(col_ga, c) for c in range(ngb)]
                 + [wspec(width), wspec(width), wspec(d),
                    pl.BlockSpec((1, d), lambda i: (0, 0))],
        out_specs=pl.BlockSpec((tm, d), lambda i: (i, 0)),
        scratch_shapes=[pltpu.VMEM((tm, d), BF16)],
        compiler_params=_params(("parallel",)),
        name="merge_out",
    )(x2d, yh, ya, *([proj] * (2 * ngb)), who, wao, wout, final_g.reshape(1, d))


def _dft_mats(b):
    idx = jnp.arange(b, dtype=jnp.int32)
    k = (idx[:, None] * idx[None, :]) % (2 * b)
    ang = k.astype(F32) * (math.pi / b)
    return jnp.cos(ang).astype(BF16), jnp.sin(ang).astype(BF16)


def _rope_tabs(L):
    inv = ROPE_THETA ** (-jnp.arange(0, ROPE_DIM, 2, dtype=F32) / ROPE_DIM)
    ang = jnp.arange(L, dtype=F32)[:, None] * inv[None, :]
    cos, sin = jnp.cos(ang), jnp.sin(ang)
    ones = jnp.ones((L, HEAD_DIM - ROPE_DIM), F32)
    zeros = jnp.zeros((L, HEAD_DIM - ROPE_HALF), F32)
    cos_t = jnp.concatenate([cos, cos, ones], axis=1)
    sin_lo = jnp.concatenate([-sin, zeros], axis=1)
    sin_hi = jnp.concatenate([jnp.zeros((L, ROPE_HALF), F32), sin, zeros[:, ROPE_HALF:]], axis=1)
    return cos_t, sin_lo, sin_hi


def _filter_feats(L):
    t = jnp.linspace(0.0, 1.0, L, dtype=F32)[:, None]
    bands = jnp.linspace(1e-4, FILTER_BANDS - 1, FILTER_BANDS, dtype=F32)[None, :]
    ang = (2.0 * math.pi / L) * jnp.arange(L, dtype=F32)[:, None] * bands
    feats = jnp.concatenate([t, jnp.cos(ang), -jnp.sin(ang)], axis=-1)
    feats = jnp.pad(feats, ((0, 0), (0, FEAT_PAD - feats.shape[1])))
    return feats, t


def kernel(x, norm_g, w_in, conv_w, conv_b, filt_w1, filt_b1, filt_w2, filt_b2, filt_w3, filt_b3,
           filt_w4, filt_freq, hyena_bias, attn_sink, w_hyena_out, w_attn_out, w_out, final_norm):
    B, L, D = x.shape
    depth = norm_g.shape[0]
    hw = hyena_bias.shape[2]
    aw = N_HEADS * HEAD_DIM
    kvw = N_KV_HEADS * HEAD_DIM
    sizes = (3 * hw, hw, aw, kvw, kvw, aw, D, D)
    cols = [0]
    for s in sizes:
        cols.append(cols[-1] + s)
    col_zhy, col_q, col_k, col_v, col_zat, col_gh, col_ga = cols[1:8]

    cmat, smat = _dft_mats(L // NB)
    nyq_cos = jnp.where(jnp.arange(L // NB) % 2 == 0, 1.0, -1.0).astype(BF16)
    smat_fwd = smat.at[0, :].set(nyq_cos)
    smat_inv = smat.at[:, 0].set(nyq_cos)
    rope_tabs = _rope_tabs(L)
    feats, tcol = _filter_feats(L)
    deltas = jnp.abs(jnp.linspace(MIN_DECAY, MAX_DECAY, hw, dtype=F32))[None, :]

    w_in_b, who_b, wao_b, wout_b = (w.astype(BF16) for w in (w_in, w_hyena_out, w_attn_out, w_out))
    row = lambda a: a[:, None, :]
    w1p = jnp.pad(filt_w1, ((0, 0), (0, FEAT_PAD - filt_w1.shape[1]), (0, 0)))
    xf = x.reshape(B * L, D)
    for l in range(depth):
        proj = _norm_proj(xf, row(norm_g), w_in_b, l)
        proj3 = proj.reshape(B, L, -1)
        kr, ki, kn = _filters(feats, tcol, w1p, row(filt_b1), filt_w2, row(filt_b2), filt_w3, row(filt_b3),
                              row(filt_freq), filt_w4, deltas, cmat, smat, l, NB)
        y_hy = _hyena(proj3, conv_w, row(conv_b), hyena_bias, cmat, smat_fwd, smat_inv, kr, ki, kn, l, NB)
        y_at = _attention(proj3, attn_sink, rope_tabs, col_q, col_k, col_v, col_zat, l)
        xf = _merge_out(xf, proj, y_hy.reshape(B * L, hw), y_at.reshape(B * L, aw),
                        who_b, wao_b, wout_b, final_norm, col_gh, col_ga, l, final=(l == depth - 1))
    return xf.reshape(B, L, D)
```

```python
import functools
import math

import jax
import jax.numpy as jnp
from jax import lax
from jax.experimental import pallas as pl
from jax.experimental.pallas import tpu as pltpu

F32 = jnp.float32
BF16 = jnp.bfloat16

HEAD_DIM = 128
N_HEADS = 8
N_KV_HEADS = 2
GROUP = N_HEADS // N_KV_HEADS
WINDOW = 128
QSUB = 128
ROPE_THETA = 500000.0
ROPE_DIM = HEAD_DIM // 4
ROPE_HALF = ROPE_DIM // 2
EPS = 1e-6
FILTER_BANDS = 16
FEAT_PAD = 128
DECAY_TARGET = 1e-2
MIN_DECAY = math.log(DECAY_TARGET) / 0.3
MAX_DECAY = math.log(DECAY_TARGET) / 1.5

NB = 4
BF16_ROWS = 16
VMEM_LIMIT = 56 * 1024 * 1024


def _params(sem, vmem=VMEM_LIMIT):
    return pltpu.CompilerParams(dimension_semantics=sem, vmem_limit_bytes=vmem)


def _const_spec(shape):
    return pl.BlockSpec(shape, lambda *_: (0,) * len(shape), pipeline_mode=pl.Buffered(1))


def _split_bf16(x):
    hi = x.astype(BF16)
    return hi, (x - hi.astype(F32)).astype(BF16)


def _alt_sign(rows):
    return jnp.where((lax.broadcasted_iota(jnp.int32, (rows, 1), 0) & 1) == 0, 1.0, -1.0)


def _norm_proj_kernel(x_ref, g_ref, w_ref, o_ref, h_ref):
    @pl.when(pl.program_id(1) == 0)
    def _():
        x = x_ref[...]
        ms = jnp.mean(x * x, axis=-1, keepdims=True)
        h_ref[...] = (x * lax.rsqrt(ms + EPS) * g_ref[...]).astype(BF16)

    o_ref[...] = jnp.dot(h_ref[...], w_ref[...], preferred_element_type=F32)


def _norm_proj(x2d, g_all, w_all, l, tm=1024, tn=1536):
    m, d = x2d.shape
    n = w_all.shape[2]
    return pl.pallas_call(
        _norm_proj_kernel,
        out_shape=jax.ShapeDtypeStruct((m, n), F32),
        grid=(m // tm, n // tn),
        in_specs=[
            pl.BlockSpec((tm, d), lambda i, j: (i, 0)),
            pl.BlockSpec((None, 1, d), lambda i, j: (l, 0, 0)),
            pl.BlockSpec((None, d, tn), lambda i, j: (l, 0, j)),
        ],
        out_specs=pl.BlockSpec((tm, tn), lambda i, j: (i, j)),
        scratch_shapes=[pltpu.VMEM((tm, d), BF16)],
        compiler_params=_params(("parallel", "arbitrary")),
        name="norm_proj",
    )(x2d, g_all, w_all)


def _filter_kernel(feats_ref, t_ref, w1_ref, b1_ref, w2_ref, b2_ref, w3_ref, b3_ref, fr_ref,
                   w4f_ref, w4b_ref, dl_ref, c_ref, s_ref,
                   kr_ref, ki_ref, kn_ref,
                   hhi_ref, hlo_ref, xf_ref, xb_ref, cf_ref, sf_ref, cb_ref, sb_ref, *, nb, row_chunk):
    L = feats_ref.shape[0]
    b = L // nb
    tc = dl_ref.shape[1]
    hp = lax.Precision.HIGHEST
    nchunks = b // row_chunk
    alt = _alt_sign(row_chunk)

    @pl.when((pl.program_id(0) == 0) & (pl.program_id(1) == 0))
    def _():
        half = L // 2
        hid = w2_ref.shape[0]
        twice = lambda a: jnp.concatenate([a, a], axis=1)

        def blockdiag(w):
            z = jnp.zeros_like(w)
            return jnp.concatenate([jnp.concatenate([w, z], axis=1), jnp.concatenate([z, w], axis=1)], axis=0)

        fr = twice(fr_ref[...])
        h = jnp.concatenate([feats_ref[0:half, :], feats_ref[half:, :]], axis=1)
        for w_ref, b_ref in ((w1_ref, b1_ref), (w2_ref, b2_ref), (w3_ref, b3_ref)):
            h = jnp.sin(fr * (jnp.dot(h, blockdiag(w_ref[...]), precision=hp,
                                      preferred_element_type=F32) + twice(b_ref[...])))
        hi, lo = _split_bf16(h)
        hhi_ref[0:half, :], hhi_ref[half:, :] = hi[:, :hid], hi[:, hid:]
        hlo_ref[0:half, :], hlo_ref[half:, :] = lo[:, :hid], lo[:, hid:]

    def dot_split(hh, hl, w_ref):
        wh, wl = _split_bf16(w_ref[...])
        return (jnp.dot(hh, wh, preferred_element_type=F32) + jnp.dot(hh, wl, preferred_element_type=F32)
                + jnp.dot(hl, wh, preferred_element_type=F32))

    def taps(rows):
        hh, hl = hhi_ref[rows, :], hlo_ref[rows, :]
        decay = jnp.exp(-t_ref[rows, :] * dl_ref[...])
        return dot_split(hh, hl, w4f_ref) * decay, dot_split(hh, hl, w4b_ref) * decay

    zero_row = jnp.zeros((1, tc), F32)
    f0, b0, af, ab = [], [], [], []
    for q in range(nb):
        head_f, head_b = taps(pl.ds(q * b, BF16_ROWS))
        f0.append(head_f[0:1, :])
        b0.append(head_b[0:1, :])

        def tap_chunk(i, carry, q=q):
            l0 = pl.multiple_of(i * row_chunk, row_chunk)
            fwd, bwd = taps(pl.ds(pl.multiple_of(q * b + l0, row_chunk), row_chunk))
            first = (lax.broadcasted_iota(jnp.int32, (row_chunk, 1), 0) + l0) == 0
            fz = jnp.where(first, 0.0, fwd)
            bz = jnp.where(first, 0.0, bwd)
            xf_ref[q, pl.ds(l0, row_chunk), :] = fz.astype(BF16)
            xb_ref[q, pl.ds(l0, row_chunk), :] = bz.astype(BF16)
            return (carry[0] + jnp.sum(fz * alt, axis=0, keepdims=True),
                    carry[1] + jnp.sum(bz * alt, axis=0, keepdims=True))

        a_f, a_b = lax.fori_loop(0, nchunks, tap_chunk, (zero_row, zero_row))
        af.append(a_f)
        ab.append(a_b)

        def spec_chunk(i, carry, q=q):
            r = pl.ds(pl.multiple_of(i * row_chunk, row_chunk), row_chunk)
            cf_ref[q, r, :] = jnp.dot(c_ref[r, :], xf_ref[q], preferred_element_type=F32)
            sf_ref[q, r, :] = jnp.dot(s_ref[r, :], xf_ref[q], preferred_element_type=F32)
            cb_ref[q, r, :] = jnp.dot(c_ref[r, :], xb_ref[q], preferred_element_type=F32)
            sb_ref[q, r, :] = jnp.dot(s_ref[r, :], xb_ref[q], preferred_element_type=F32)
            return carry

        lax.fori_loop(0, nchunks, spec_chunk, 0)

    inv_n = 1.0 / (2 * b)
    for d in range(-(nb - 1), nb):
        slot = d + nb - 1
        e = -d
        if d >= 1:
            kn = af[d] + f0[d] + af[d - 1]
        elif d == 0:
            kn = af[0] + f0[0] + ab[0]
        else:
            kn = b0[e] + ab[e] + ab[e - 1]
        kn_ref[slot] = kn * inv_n

        def combine(i, carry, d=d, e=e, slot=slot):
            r0 = pl.multiple_of(i * row_chunk, row_chunk)
            r = pl.ds(r0, row_chunk)
            first = (lax.broadcasted_iota(jnp.int32, (row_chunk, 1), 0) + r0) == 0
            wgt = jnp.where(first, inv_n, 2.0 * inv_n)
            if d >= 1:
                kr = cf_ref[d, r, :] + f0[d] + alt * cf_ref[d - 1, r, :]
                ki = -sf_ref[d, r, :] - alt * sf_ref[d - 1, r, :]
            elif d == 0:
                kr = cf_ref[0, r, :] + f0[0] + cb_ref[0, r, :]
                ki = sb_ref[0, r, :] - sf_ref[0, r, :]
            else:
                kr = b0[e] + cb_ref[e, r, :] + alt * cb_ref[e - 1, r, :]
                ki = sb_ref[e, r, :] + alt * sb_ref[e - 1, r, :]
            kr_ref[slot, r, :] = (kr * wgt).astype(kr_ref.dtype)
            ki_ref[slot, r, :] = (ki * wgt).astype(ki_ref.dtype)
            return carry

        lax.fori_loop(0, nchunks, combine, 0)


def _filters(feats, tcol, w1p, b1, w2, b2, w3, b3, freq, w4, deltas, cmat, smat, l, nb, tc=256,
             row_chunk=512):
    L = feats.shape[0]
    b = L // nb
    nd = 2 * nb - 1
    width = deltas.shape[1]
    nct = width // tc
    hid = w2.shape[1]
    small = lambda a: pl.BlockSpec(a.shape, lambda o, c: (0,) * a.ndim)
    layer = lambda a: pl.BlockSpec((None,) + a.shape[1:], lambda o, c: (l,) + (0,) * (a.ndim - 1))
    kern = functools.partial(_filter_kernel, nb=nb, row_chunk=row_chunk)
    kspec = pl.BlockSpec((None, nd, b, tc), lambda o, c: (o, 0, 0, c))
    blk = lambda dt: pltpu.VMEM((nb, b, tc), dt)
    return pl.pallas_call(
        kern,
        out_shape=(jax.ShapeDtypeStruct((2, nd, b, width), BF16),
                   jax.ShapeDtypeStruct((2, nd, b, width), BF16),
                   jax.ShapeDtypeStruct((2, nd, 1, width), F32)),
        grid=(2, nct),
        in_specs=[small(feats), small(tcol), layer(w1p), layer(b1), layer(w2), layer(b2),
                  layer(w3), layer(b3), layer(freq),
                  pl.BlockSpec((None, hid, tc), lambda o, c: (l, 0, o * 2 * nct + c)),
                  pl.BlockSpec((None, hid, tc), lambda o, c: (l, 0, o * 2 * nct + nct + c)),
                  pl.BlockSpec((1, tc), lambda o, c: (0, c)),
                  _const_spec((b, b)), _const_spec((b, b))],
        out_specs=(kspec, kspec, pl.BlockSpec((None, nd, 1, tc), lambda o, c: (o, 0, 0, c))),
        scratch_shapes=[pltpu.VMEM((L, hid), BF16), pltpu.VMEM((L, hid), BF16), blk(BF16), blk(BF16),
                        blk(F32), blk(F32), blk(F32), blk(F32)],
        compiler_params=_params(("arbitrary", "arbitrary")),
        name="hyena_filters",
    )(feats, tcol, w1p, b1, w2, b2, w3, b3, freq, w4, w4, deltas, cmat, smat)


def _sconv_chunk(u_ref, w_ref, b_ref, i, rows, nchunks):
    r0 = i * rows
    tc = u_ref.shape[1]
    u = u_ref[r0:r0 + rows, :]
    zero = jnp.zeros((1, tc), F32)
    up = u_ref[r0 - 8:r0, :][7:8, :] if i > 0 else zero
    dn = u_ref[r0 + rows:r0 + rows + 8, :][0:1, :] if i < nchunks - 1 else zero
    row = lax.broadcasted_iota(jnp.int32, (rows, 1), 0)
    prev = jnp.where(row == 0, up, pltpu.roll(u, 1, 0))
    nxt = jnp.where(row == rows - 1, dn, pltpu.roll(u, rows - 1, 0))
    return b_ref[...] + prev * w_ref[0:1, :] + u * w_ref[1:2, :] + nxt * w_ref[2:3, :]


def _hyena_kernel(v_ref, x1_ref, x2_ref, zg_ref, wv_ref, wx1_ref, wx2_ref, bv_ref, bx1_ref, bx2_ref,
                  hb_ref, c_ref, sf_ref, si_ref, kr_ref, ki_ref, kn_ref, o_ref,
                  u_ref, ub_ref, a_ref, bn_ref, *, nb, seq_chunk, freq_chunk):
    L, tc = u_ref.shape
    b = L // nb
    per_block = b // seq_chunk
    nseq = L // seq_chunk

    def lanes(j):
        return slice(j * tc, (j + 1) * tc)

    def put_signal(chunk, val):
        j, local = divmod(chunk, per_block)
        rows = slice(chunk * seq_chunk, (chunk + 1) * seq_chunk)
        u_ref[rows, :] = val
        ub_ref[local * seq_chunk:(local + 1) * seq_chunk, lanes(j)] = val.astype(BF16)

    def forward(order):
        for fc in range(b // freq_chunk):
            r = slice(fc * freq_chunk, (fc + 1) * freq_chunk)
            ur = jnp.dot(c_ref[r, :], ub_ref[...], preferred_element_type=F32).astype(BF16)
            us = jnp.dot(sf_ref[r, :], ub_ref[...], preferred_element_type=F32).astype(BF16)
            for i in range(nb):
                acc_a = acc_b = None
                for j in range(nb):
                    kr = kr_ref[order, i - j + nb - 1, r, :]
                    ki = ki_ref[order, i - j + nb - 1, r, :]
                    urj, usj = ur[:, lanes(j)], us[:, lanes(j)]
                    ta = urj * kr + usj * ki
                    tb = usj * kr - urj * ki
                    acc_a = ta if acc_a is None else acc_a + ta
                    acc_b = tb if acc_b is None else acc_b + tb
                if fc == 0:
                    nyq = None
                    for j in range(nb):
                        t = us[0:1, lanes(j)].astype(F32) * kn_ref[order, i - j + nb - 1]
                        nyq = t if nyq is None else nyq + t
                    first = lax.broadcasted_iota(jnp.int32, (BF16_ROWS, 1), 0) == 0
                    top = jnp.where(first, nyq.astype(BF16), acc_b[:BF16_ROWS])
                    acc_b = jnp.concatenate([top, acc_b[BF16_ROWS:]], axis=0)
                a_ref[r, lanes(i)] = acc_a
                bn_ref[r, lanes(i)] = acc_b

    def inverse(order, finish):
        for tcn in range(per_block):
            r = slice(tcn * seq_chunk, (tcn + 1) * seq_chunk)
            y2 = jnp.dot(c_ref[r, :], a_ref[...], preferred_element_type=F32)
            y2 = y2 + jnp.dot(si_ref[r, :], bn_ref[...], preferred_element_type=F32)
            for i in range(nb):
                chunk = i * per_block + tcn
                rows = slice(chunk * seq_chunk, (chunk + 1) * seq_chunk)
                finish(chunk, rows, y2[:, lanes(i)] + u_ref[rows, :] * hb_ref[order:order + 1, :])

    for chunk in range(nseq):
        put_signal(chunk, _sconv_chunk(v_ref, wv_ref, bv_ref, chunk, seq_chunk, nseq))

    forward(0)

    def finish_z(chunk, rows, y):
        put_signal(chunk, _sconv_chunk(x1_ref, wx1_ref, bx1_ref, chunk, seq_chunk, nseq) * y)

    inverse(0, finish_z)
    forward(1)

    def finish_out(chunk, rows, y):
        y = _sconv_chunk(x2_ref, wx2_ref, bx2_ref, chunk, seq_chunk, nseq) * y
        zg = zg_ref[rows, :]
        o_ref[rows, :] = (y * (zg * jax.nn.sigmoid(zg))).astype(o_ref.dtype)

    inverse(1, finish_out)


def _hyena(proj3, conv_w, conv_b, hyena_bias, cmat, smat_fwd, smat_inv, kr, ki, kn, l, nb, tc=256,
           seq_chunk=512, freq_chunk=512):
    B, L, _ = proj3.shape
    b = L // nb
    nd = 2 * nb - 1
    width = hyena_bias.shape[2]
    nct = width // tc
    sig = lambda part: pl.BlockSpec((None, L, tc), lambda j, bb: (bb, 0, part * nct + j))
    cw = lambda part: pl.BlockSpec((None, 3, tc), lambda j, bb: (l, 0, part * nct + j))
    cb = lambda part: pl.BlockSpec((None, 1, tc), lambda j, bb: (l, 0, part * nct + j))
    kspec = pl.BlockSpec((2, nd, b, tc), lambda j, bb: (0, 0, 0, j))
    kern = functools.partial(_hyena_kernel, nb=nb, seq_chunk=seq_chunk, freq_chunk=freq_chunk)
    return pl.pallas_call(
        kern,
        out_shape=jax.ShapeDtypeStruct((B, L, width), BF16),
        grid=(nct, B),
        in_specs=[sig(0), sig(1), sig(2), sig(3),
                  cw(0), cw(1), cw(2), cb(0), cb(1), cb(2),
                  pl.BlockSpec((None, 2, tc), lambda j, bb: (l, 0, j)),
                  _const_spec((b, b)), _const_spec((b, b)), _const_spec((b, b)),
                  kspec, kspec,
                  pl.BlockSpec((2, nd, 1, tc), lambda j, bb: (0, 0, 0, j))],
        out_specs=pl.BlockSpec((None, L, tc), lambda j, bb: (bb, 0, j)),
        scratch_shapes=[pltpu.VMEM((L, tc), F32), pltpu.VMEM((b, nb * tc), BF16),
                        pltpu.VMEM((b, nb * tc), BF16), pltpu.VMEM((b, nb * tc), BF16)],
        compiler_params=_params(("parallel", "arbitrary")),
        name="hyena_mixer",
    )(proj3, proj3, proj3, proj3, conv_w, conv_w, conv_w, conv_b, conv_b, conv_b,
      hyena_bias, cmat, smat_fwd, smat_inv, kr, ki, kn)


def _rope(x, cos, sin_lo, sin_hi):
    return (x * cos + pltpu.roll(x, HEAD_DIM - ROPE_HALF, 1) * sin_lo
            + pltpu.roll(x, ROPE_HALF, 1) * sin_hi)


def _attn_kernel(sink_ref, q_ref, k_ref, v_ref, zg_ref, cos_ref, slo_ref, shi_ref, o_ref,
                 kb_ref, vt_ref, *, layer):
    L = k_ref.shape[0]
    qb = q_ref.shape[0]
    kw = QSUB + 2 * WINDOW
    gq = GROUP * QSUB
    kvh = pl.program_id(1)
    n = pl.program_id(2)

    @pl.when(n == 0)
    def _():
        kb_ref[...] = _rope(k_ref[...], cos_ref[...], slo_ref[...], shi_ref[...]).astype(BF16)
        vt_ref[...] = v_ref[...].T.astype(BF16)

    log2e = math.log2(math.e)
    lane_head = lax.broadcasted_iota(jnp.int32, (1, gq), 1) // QSUB
    sk2 = jnp.zeros((1, gq), F32)
    for g in range(GROUP):
        sk2 = jnp.where(lane_head == g, sink_ref[layer, kvh * GROUP + g] * log2e, sk2)

    for sb in range(qb // QSUB):
        rows = slice(sb * QSUB, (sb + 1) * QSUB)
        q0 = pl.multiple_of(n * qb + sb * QSUB, QSUB)
        start = pl.multiple_of(jnp.clip(q0 - WINDOW, 0, L - kw), WINDOW)
        win = pl.ds(start, kw)
        tab = pl.ds(q0, QSUB)
        cq, slq, shq = cos_ref[tab, :], slo_ref[tab, :], shi_ref[tab, :]
        q = jnp.concatenate(
            [_rope(q_ref[rows, g * HEAD_DIM:(g + 1) * HEAD_DIM], cq, slq, shq).astype(BF16)
             for g in range(GROUP)], axis=0)
        raw = lax.dot_general(kb_ref[win, :], q, (((1,), (1,)), ((), ())),
                              preferred_element_type=F32)
        kpos = start + lax.broadcasted_iota(jnp.int32, (kw, 1), 0)
        qpos = q0 + lax.broadcasted_iota(jnp.int32, (1, QSUB), 1)
        cap = jnp.where(jnp.abs(kpos - qpos) <= WINDOW, jnp.inf, -jnp.inf)
        raw = jnp.minimum(raw, jnp.concatenate([cap] * GROUP, axis=1))
        c = (HEAD_DIM ** -0.5) * log2e
        m2 = jnp.maximum(jnp.max(raw, axis=0, keepdims=True) * c, sk2)
        p = jnp.exp2(raw * c - m2)
        denom = jnp.sum(p, axis=0, keepdims=True) + jnp.exp2(sk2 - m2)
        ot = jnp.dot(vt_ref[:, win], p.astype(BF16), preferred_element_type=F32) / denom
        for g in range(GROUP):
            cols = slice(g * HEAD_DIM, (g + 1) * HEAD_DIM)
            o = ot[:, g * QSUB:(g + 1) * QSUB].T
            zg = zg_ref[rows, cols]
            o_ref[rows, cols] = (o * (zg * jax.nn.sigmoid(zg))).astype(o_ref.dtype)


def _attention(proj3, sink, rope_tabs, col_q, col_k, col_v, col_zg, l, qb=1024):
    B, L, _ = proj3.shape
    gw = GROUP * HEAD_DIM
    cos_t, sin_lo, sin_hi = rope_tabs
    tab = pl.BlockSpec((L, HEAD_DIM), lambda b, h, n: (0, 0))
    return pl.pallas_call(
        functools.partial(_attn_kernel, layer=l),
        out_shape=jax.ShapeDtypeStruct((B, L, N_HEADS * HEAD_DIM), BF16),
        grid=(B, N_KV_HEADS, L // qb),
        in_specs=[pl.BlockSpec(memory_space=pltpu.SMEM),
                  pl.BlockSpec((None, qb, gw), lambda b, h, n: (b, n, col_q // gw + h)),
                  pl.BlockSpec((None, L, HEAD_DIM), lambda b, h, n: (b, 0, col_k // HEAD_DIM + h)),
                  pl.BlockSpec((None, L, HEAD_DIM), lambda b, h, n: (b, 0, col_v // HEAD_DIM + h)),
                  pl.BlockSpec((None, qb, gw), lambda b, h, n: (b, n, col_zg // gw + h)),
                  tab, tab, tab],
        out_specs=pl.BlockSpec((None, qb, gw), lambda b, h, n: (b, n, h)),
        scratch_shapes=[pltpu.VMEM((L, HEAD_DIM), BF16), pltpu.VMEM((HEAD_DIM, L), BF16)],
        compiler_params=_params(("parallel", "parallel", "arbitrary")),
        name="window_attention",
    )(sink, proj3, proj3, proj3, proj3, cos_t, sin_lo, sin_hi)


def _merge_out_kernel(*refs, n_gate_blocks, final):
    x_ref, yh_ref, ya_ref = refs[0:3]
    gh_refs = refs[3:3 + n_gate_blocks]
    ga_refs = refs[3 + n_gate_blocks:3 + 2 * n_gate_blocks]
    who_ref, wao_ref, wout_ref, fg_ref, o_ref, m_ref = refs[3 + 2 * n_gate_blocks:]
    gw = gh_refs[0].shape[1]
    yh = yh_ref[...]
    ya = ya_ref[...]
    for c in range(n_gate_blocks):
        cols = slice(c * gw, (c + 1) * gw)
        ph = jnp.dot(yh, who_ref[:, cols], preferred_element_type=F32)
        pa = jnp.dot(ya, wao_ref[:, cols], preferred_element_type=F32)
        merged = jax.nn.sigmoid(gh_refs[c][...]) * ph + jax.nn.sigmoid(ga_refs[c][...]) * pa
        m_ref[:, cols] = merged.astype(BF16)
    out = x_ref[...] + jnp.dot(m_ref[...], wout_ref[...], preferred_element_type=F32)
    if final:
        ms = jnp.mean(out * out, axis=-1, keepdims=True)
        out = out * lax.rsqrt(ms + EPS) * fg_ref[...]
    o_ref[...] = out


def _merge_out(x2d, proj, yh, ya, who, wao, wout, final_g, col_gh, col_ga, l, final, tm=256, gw=512):
    m, d = x2d.shape
    width = yh.shape[1]
    ngb = d // gw
    gspec = lambda col0, c: pl.BlockSpec((tm, gw), lambda i: (i, col0 // gw + c))
    wspec = lambda rows: pl.BlockSpec((None, rows, d), lambda i: (l, 0, 0), pipeline_mode=pl.Buffered(1))
    kern = functools.partial(_merge_out_kernel, n_gate_blocks=ngb, final=final)
    return pl.pallas_call(
        kern,
        out_shape=jax.ShapeDtypeStruct((m, d), F32),
        grid=(m // tm,),
        in_specs=[pl.BlockSpec((tm, d), lambda i: (i, 0)),
                  pl.BlockSpec((tm, width), lambda i: (i, 0)),
                  pl.BlockSpec((tm, width), lambda i: (i, 0))]
                 + [gspec(col_gh, c) for c in range(ngb)]
                 + [gspec(col_ga, c) for c in range(ngb)]
                 + [wspec(width), wspec(width), wspec(d),
                    pl.BlockSpec((1, d), lambda i: (0, 0))],
        out_specs=pl.BlockSpec((tm, d), lambda i: (i, 0)),
        scratch_shapes=[pltpu.VMEM((tm, d), BF16)],
        compiler_params=_params(("parallel",)),
        name="merge_out",
    )(x2d, yh, ya, *([proj] * (2 * ngb)), who, wao, wout, final_g.reshape(1, d))


def _dft_mats(b):
    idx = jnp.arange(b, dtype=jnp.int32)
    k = (idx[:, None] * idx[None, :]) % (2 * b)
    ang = k.astype(F32) * (math.pi / b)
    return jnp.cos(ang).astype(BF16), jnp.sin(ang).astype(BF16)


def _rope_tabs(L):
    inv = ROPE_THETA ** (-jnp.arange(0, ROPE_DIM, 2, dtype=F32) / ROPE_DIM)
    ang = jnp.arange(L, dtype=F32)[:, None] * inv[None, :]
    cos, sin = jnp.cos(ang), jnp.sin(ang)
    ones = jnp.ones((L, HEAD_DIM - ROPE_DIM), F32)
    zeros = jnp.zeros((L, HEAD_DIM - ROPE_HALF), F32)
    cos_t = jnp.concatenate([cos, cos, ones], axis=1)
    sin_lo = jnp.concatenate([-sin, zeros], axis=1)
    sin_hi = jnp.concatenate([jnp.zeros((L, ROPE_HALF), F32), sin, zeros[:, ROPE_HALF:]], axis=1)
    return cos_t, sin_lo, sin_hi


def _filter_feats(L):
    t = jnp.linspace(0.0, 1.0, L, dtype=F32)[:, None]
    bands = jnp.linspace(1e-4, FILTER_BANDS - 1, FILTER_BANDS, dtype=F32)[None, :]
    ang = (2.0 * math.pi / L) * jnp.arange(L, dtype=F32)[:, None] * bands
    feats = jnp.concatenate([t, jnp.cos(ang), -jnp.sin(ang)], axis=-1)
    feats = jnp.pad(feats, ((0, 0), (0, FEAT_PAD - feats.shape[1])))
    return feats, t


def kernel(x, norm_g, w_in, conv_w, conv_b, filt_w1, filt_b1, filt_w2, filt_b2, filt_w3, filt_b3,
           filt_w4, filt_freq, hyena_bias, attn_sink, w_hyena_out, w_attn_out, w_out, final_norm):
    B, L, D = x.shape
    depth = norm_g.shape[0]
    hw = hyena_bias.shape[2]
    aw = N_HEADS * HEAD_DIM
    kvw = N_KV_HEADS * HEAD_DIM
    sizes = (3 * hw, hw, aw, kvw, kvw, aw, D, D)
    cols = [0]
    for s in sizes:
        cols.append(cols[-1] + s)
    col_zhy, col_q, col_k, col_v, col_zat, col_gh, col_ga = cols[1:8]

    cmat, smat = _dft_mats(L // NB)
    nyq_cos = jnp.where(jnp.arange(L // NB) % 2 == 0, 1.0, -1.0).astype(BF16)
    smat_fwd = smat.at[0, :].set(nyq_cos)
    smat_inv = smat.at[:, 0].set(nyq_cos)
    rope_tabs = _rope_tabs(L)
    feats, tcol = _filter_feats(L)
    deltas = jnp.abs(jnp.linspace(MIN_DECAY, MAX_DECAY, hw, dtype=F32))[None, :]

    w_in_b, who_b, wao_b, wout_b = (w.astype(BF16) for w in (w_in, w_hyena_out, w_attn_out, w_out))
    row = lambda a: a[:, None, :]
    w1p = jnp.pad(filt_w1, ((0, 0), (0, FEAT_PAD - filt_w1.shape[1]), (0, 0)))
    xf = x.reshape(B * L, D)
    for l in range(depth):
        proj = _norm_proj(xf, row(norm_g), w_in_b, l)
        proj3 = proj.reshape(B, L, -1)
        kr, ki, kn = _filters(feats, tcol, w1p, row(filt_b1), filt_w2, row(filt_b2), filt_w3, row(filt_b3),
                              row(filt_freq), filt_w4, deltas, cmat, smat, l, NB)
        y_hy = _hyena(proj3, conv_w, row(conv_b), hyena_bias, cmat, smat_fwd, smat_inv, kr, ki, kn, l, NB)
        y_at = _attention(proj3, attn_sink, rope_tabs, col_q, col_k, col_v, col_zat, l)
        xf = _merge_out(xf, proj, y_hy.reshape(B * L, hw), y_at.reshape(B * L, aw),
                        who_b, wao_b, wout_b, final_norm, col_gh, col_ga, l, final=(l == depth - 1))
    return xf.reshape(B, L, D)
```

```python
import functools
import math

import jax
import jax.numpy as jnp
from jax import lax
from jax.experimental import pallas as pl
from jax.experimental.pallas import tpu as pltpu

F32 = jnp.float32
BF16 = jnp.bfloat16

HEAD_DIM = 128
N_HEADS = 8
N_KV_HEADS = 2
GROUP = N_HEADS // N_KV_HEADS
WINDOW = 128
QSUB = 128
ROPE_THETA = 500000.0
ROPE_DIM = HEAD_DIM // 4
ROPE_HALF = ROPE_DIM // 2
EPS = 1e-6
FILTER_BANDS = 16
FEAT_PAD = 128
DECAY_TARGET = 1e-2
MIN_DECAY = math.log(DECAY_TARGET) / 0.3
MAX_DECAY = math.log(DECAY_TARGET) / 1.5

NB = 4
BF16_ROWS = 16
VMEM_LIMIT = 56 * 1024 * 1024


def _params(sem, vmem=VMEM_LIMIT):
    return pltpu.CompilerParams(dimension_semantics=sem, vmem_limit_bytes=vmem)


def _const_spec(shape):
    return pl.BlockSpec(shape, lambda *_: (0,) * len(shape), pipeline_mode=pl.Buffered(1))


def _split_bf16(x):
    hi = x.astype(BF16)
    return hi, (x - hi.astype(F32)).astype(BF16)


def _alt_sign(rows):
    return jnp.where((lax.broadcasted_iota(jnp.int32, (rows, 1), 0) & 1) == 0, 1.0, -1.0)


def _norm_proj_kernel(x_ref, g_ref, w_ref, o_ref, h_ref):
    @pl.when(pl.program_id(1) == 0)
    def _():
        x = x_ref[...]
        ms = jnp.mean(x * x, axis=-1, keepdims=True)
        h_ref[...] = (x * lax.rsqrt(ms + EPS) * g_ref[...]).astype(BF16)

    o_ref[...] = jnp.dot(h_ref[...], w_ref[...], preferred_element_type=F32)


def _norm_proj(x2d, g_all, w_all, l, tm=1024, tn=1792):
    m, d = x2d.shape
    n = w_all.shape[2]
    return pl.pallas_call(
        _norm_proj_kernel,
        out_shape=jax.ShapeDtypeStruct((m, n), F32),
        grid=(m // tm, n // tn),
        in_specs=[
            pl.BlockSpec((tm, d), lambda i, j: (i, 0)),
            pl.BlockSpec((None, 1, d), lambda i, j: (l, 0, 0)),
            pl.BlockSpec((None, d, tn), lambda i, j: (l, 0, j)),
        ],
        out_specs=pl.BlockSpec((tm, tn), lambda i, j: (i, j)),
        scratch_shapes=[pltpu.VMEM((tm, d), BF16)],
        compiler_params=_params(("parallel", "arbitrary")),
        name="norm_proj",
    )(x2d, g_all, w_all)


def _filter_kernel(feats_ref, t_ref, w1_ref, b1_ref, w2_ref, b2_ref, w3_ref, b3_ref, fr_ref,
                   w4f_ref, w4b_ref, dl_ref, c_ref, s_ref,
                   kr_ref, ki_ref, kn_ref,
                   hhi_ref, hlo_ref, xf_ref, xb_ref, cf_ref, sf_ref, cb_ref, sb_ref, *, nb, row_chunk):
    L = feats_ref.shape[0]
    b = L // nb
    tc = dl_ref.shape[1]
    hp = lax.Precision.HIGHEST
    nchunks = b // row_chunk
    alt = _alt_sign(row_chunk)

    @pl.when((pl.program_id(0) == 0) & (pl.program_id(1) == 0))
    def _():
        half = L // 2
        hid = w2_ref.shape[0]
        twice = lambda a: jnp.concatenate([a, a], axis=1)

        def blockdiag(w):
            z = jnp.zeros_like(w)
            return jnp.concatenate([jnp.concatenate([w, z], axis=1), jnp.concatenate([z, w], axis=1)], axis=0)

        fr = twice(fr_ref[...])
        h = jnp.concatenate([feats_ref[0:half, :], feats_ref[half:, :]], axis=1)
        for w_ref, b_ref in ((w1_ref, b1_ref), (w2_ref, b2_ref), (w3_ref, b3_ref)):
            h = jnp.sin(fr * (jnp.dot(h, blockdiag(w_ref[...]), precision=hp,
                                      preferred_element_type=F32) + twice(b_ref[...])))
        hi, lo = _split_bf16(h)
        hhi_ref[0:half, :], hhi_ref[half:, :] = hi[:, :hid], hi[:, hid:]
        hlo_ref[0:half, :], hlo_ref[half:, :] = lo[:, :hid], lo[:, hid:]

    def dot_split(hh, hl, w_ref):
        wh, wl = _split_bf16(w_ref[...])
        return (jnp.dot(hh, wh, preferred_element_type=F32) + jnp.dot(hh, wl, preferred_element_type=F32)
                + jnp.dot(hl, wh, preferred_element_type=F32))

    def taps(rows):
        hh, hl = hhi_ref[rows, :], hlo_ref[rows, :]
        decay = jnp.exp(-t_ref[rows, :] * dl_ref[...])
        return dot_split(hh, hl, w4f_ref) * decay, dot_split(hh, hl, w4b_ref) * decay

    zero_row = jnp.zeros((1, tc), F32)
    f0, b0, af, ab = [], [], [], []
    for q in range(nb):
        head_f, head_b = taps(pl.ds(q * b, BF16_ROWS))
        f0.append(head_f[0:1, :])
        b0.append(head_b[0:1, :])

        def tap_chunk(i, carry, q=q):
            l0 = pl.multiple_of(i * row_chunk, row_chunk)
            fwd, bwd = taps(pl.ds(pl.multiple_of(q * b + l0, row_chunk), row_chunk))
            first = (lax.broadcasted_iota(jnp.int32, (row_chunk, 1), 0) + l0) == 0
            fz = jnp.where(first, 0.0, fwd)
            bz = jnp.where(first, 0.0, bwd)
            xf_ref[q, pl.ds(l0, row_chunk), :] = fz.astype(BF16)
            xb_ref[q, pl.ds(l0, row_chunk), :] = bz.astype(BF16)
            return (carry[0] + jnp.sum(fz * alt, axis=0, keepdims=True),
                    carry[1] + jnp.sum(bz * alt, axis=0, keepdims=True))

        a_f, a_b = lax.fori_loop(0, nchunks, tap_chunk, (zero_row, zero_row))
        af.append(a_f)
        ab.append(a_b)

        def spec_chunk(i, carry, q=q):
            r = pl.ds(pl.multiple_of(i * row_chunk, row_chunk), row_chunk)
            cf_ref[q, r, :] = jnp.dot(c_ref[r, :], xf_ref[q], preferred_element_type=F32)
            sf_ref[q, r, :] = jnp.dot(s_ref[r, :], xf_ref[q], preferred_element_type=F32)
            cb_ref[q, r, :] = jnp.dot(c_ref[r, :], xb_ref[q], preferred_element_type=F32)
            sb_ref[q, r, :] = jnp.dot(s_ref[r, :], xb_ref[q], preferred_element_type=F32)
            return carry

        lax.fori_loop(0, nchunks, spec_chunk, 0)

    inv_n = 1.0 / (2 * b)
    for d in range(-(nb - 1), nb):
        slot = d + nb - 1
        e = -d
        if d >= 1:
            kn = af[d] + f0[d] + af[d - 1]
        elif d == 0:
            kn = af[0] + f0[0] + ab[0]
        else:
            kn = b0[e] + ab[e] + ab[e - 1]
        kn_ref[slot] = kn * inv_n

        def combine(i, carry, d=d, e=e, slot=slot):
            r0 = pl.multiple_of(i * row_chunk, row_chunk)
            r = pl.ds(r0, row_chunk)
            first = (lax.broadcasted_iota(jnp.int32, (row_chunk, 1), 0) + r0) == 0
            wgt = jnp.where(first, inv_n, 2.0 * inv_n)
            if d >= 1:
                kr = cf_ref[d, r, :] + f0[d] + alt * cf_ref[d - 1, r, :]
                ki = -sf_ref[d, r, :] - alt * sf_ref[d - 1, r, :]
            elif d == 0:
                kr = cf_ref[0, r, :] + f0[0] + cb_ref[0, r, :]
                ki = sb_ref[0, r, :] - sf_ref[0, r, :]
            else:
                kr = b0[e] + cb_ref[e, r, :] + alt * cb_ref[e - 1, r, :]
                ki = sb_ref[e, r, :] + alt * sb_ref[e - 1, r, :]
            kr_ref[slot, r, :] = (kr * wgt).astype(kr_ref.dtype)
            ki_ref[slot, r, :] = (ki * wgt).astype(ki_ref.dtype)
            return carry

        lax.fori_loop(0, nchunks, combine, 0)


def _filters(feats, tcol, w1p, b1, w2, b2, w3, b3, freq, w4, deltas, cmat, smat, l, nb, tc=256,
             row_chunk=512):
    L = feats.shape[0]
    b = L // nb
    nd = 2 * nb - 1
    width = deltas.shape[1]
    nct = width // tc
    hid = w2.shape[1]
    small = lambda a: pl.BlockSpec(a.shape, lambda o, c: (0,) * a.ndim)
    layer = lambda a: pl.BlockSpec((None,) + a.shape[1:], lambda o, c: (l,) + (0,) * (a.ndim - 1))
    kern = functools.partial(_filter_kernel, nb=nb, row_chunk=row_chunk)
    kspec = pl.BlockSpec((None, nd, b, tc), lambda o, c: (o, 0, 0, c))
    blk = lambda dt: pltpu.VMEM((nb, b, tc), dt)
    return pl.pallas_call(
        kern,
        out_shape=(jax.ShapeDtypeStruct((2, nd, b, width), BF16),
                   jax.ShapeDtypeStruct((2, nd, b, width), BF16),
                   jax.ShapeDtypeStruct((2, nd, 1, width), F32)),
        grid=(2, nct),
        in_specs=[small(feats), small(tcol), layer(w1p), layer(b1), layer(w2), layer(b2),
                  layer(w3), layer(b3), layer(freq),
                  pl.BlockSpec((None, hid, tc), lambda o, c: (l, 0, o * 2 * nct + c)),
                  pl.BlockSpec((None, hid, tc), lambda o, c: (l, 0, o * 2 * nct + nct + c)),
                  pl.BlockSpec((1, tc), lambda o, c: (0, c)),
                  _const_spec((b, b)), _const_spec((b, b))],
        out_specs=(kspec, kspec, pl.BlockSpec((None, nd, 1, tc), lambda o, c: (o, 0, 0, c))),
        scratch_shapes=[pltpu.VMEM((L, hid), BF16), pltpu.VMEM((L, hid), BF16), blk(BF16), blk(BF16),
                        blk(F32), blk(F32), blk(F32), blk(F32)],
        compiler_params=_params(("arbitrary", "arbitrary")),
        name="hyena_filters",
    )(feats, tcol, w1p, b1, w2, b2, w3, b3, freq, w4, w4, deltas, cmat, smat)


def _sconv_chunk(u_ref, w_ref, b_ref, i, rows, nchunks):
    r0 = i * rows
    tc = u_ref.shape[1]
    u = u_ref[r0:r0 + rows, :]
    zero = jnp.zeros((1, tc), F32)
    up = u_ref[r0 - 8:r0, :][7:8, :] if i > 0 else zero
    dn = u_ref[r0 + rows:r0 + rows + 8, :][0:1, :] if i < nchunks - 1 else zero
    row = lax.broadcasted_iota(jnp.int32, (rows, 1), 0)
    prev = jnp.where(row == 0, up, pltpu.roll(u, 1, 0))
    nxt = jnp.where(row == rows - 1, dn, pltpu.roll(u, rows - 1, 0))
    return b_ref[...] + prev * w_ref[0:1, :] + u * w_ref[1:2, :] + nxt * w_ref[2:3, :]


def _hyena_kernel(v_ref, x1_ref, x2_ref, zg_ref, wv_ref, wx1_ref, wx2_ref, bv_ref, bx1_ref, bx2_ref,
                  hb_ref, c_ref, sf_ref, si_ref, kr_ref, ki_ref, kn_ref, o_ref,
                  u_ref, ub_ref, a_ref, bn_ref, *, nb, seq_chunk, freq_chunk):
    L, tc = u_ref.shape
    b = L // nb
    per_block = b // seq_chunk
    nseq = L // seq_chunk

    def lanes(j):
        return slice(j * tc, (j + 1) * tc)

    def put_signal(chunk, val):
        j, local = divmod(chunk, per_block)
        rows = slice(chunk * seq_chunk, (chunk + 1) * seq_chunk)
        u_ref[rows, :] = val
        ub_ref[local * seq_chunk:(local + 1) * seq_chunk, lanes(j)] = val.astype(BF16)

    def forward(order):
        for fc in range(b // freq_chunk):
            r = slice(fc * freq_chunk, (fc + 1) * freq_chunk)
            ur = jnp.dot(c_ref[r, :], ub_ref[...], preferred_element_type=F32).astype(BF16)
            us = jnp.dot(sf_ref[r, :], ub_ref[...], preferred_element_type=F32).astype(BF16)
            for i in range(nb):
                acc_a = acc_b = None
                for j in range(nb):
                    kr = kr_ref[order, i - j + nb - 1, r, :]
                    ki = ki_ref[order, i - j + nb - 1, r, :]
                    urj, usj = ur[:, lanes(j)], us[:, lanes(j)]
                    ta = urj * kr + usj * ki
                    tb = usj * kr - urj * ki
                    acc_a = ta if acc_a is None else acc_a + ta
                    acc_b = tb if acc_b is None else acc_b + tb
                if fc == 0:
                    nyq = None
                    for j in range(nb):
                        t = us[0:1, lanes(j)].astype(F32) * kn_ref[order, i - j + nb - 1]
                        nyq = t if nyq is None else nyq + t
                    first = lax.broadcasted_iota(jnp.int32, (BF16_ROWS, 1), 0) == 0
                    top = jnp.where(first, nyq.astype(BF16), acc_b[:BF16_ROWS])
                    acc_b = jnp.concatenate([top, acc_b[BF16_ROWS:]], axis=0)
                a_ref[r, lanes(i)] = acc_a
                bn_ref[r, lanes(i)] = acc_b

    def inverse(order, finish):
        for tcn in range(per_block):
            r = slice(tcn * seq_chunk, (tcn + 1) * seq_chunk)
            y2 = jnp.dot(c_ref[r, :], a_ref[...], preferred_element_type=F32)
            y2 = y2 + jnp.dot(si_ref[r, :], bn_ref[...], preferred_element_type=F32)
            for i in range(nb):
                chunk = i * per_block + tcn
                rows = slice(chunk * seq_chunk, (chunk + 1) * seq_chunk)
                finish(chunk, rows, y2[:, lanes(i)] + u_ref[rows, :] * hb_ref[order:order + 1, :])

    for chunk in range(nseq):
        put_signal(chunk, _sconv_chunk(v_ref, wv_ref, bv_ref, chunk, seq_chunk, nseq))

    forward(0)

    def finish_z(chunk, rows, y):
        put_signal(chunk, _sconv_chunk(x1_ref, wx1_ref, bx1_ref, chunk, seq_chunk, nseq) * y)

    inverse(0, finish_z)
    forward(1)

    def finish_out(chunk, rows, y):
        y = _sconv_chunk(x2_ref, wx2_ref, bx2_ref, chunk, seq_chunk, nseq) * y
        zg = zg_ref[rows, :]
        o_ref[rows, :] = (y * (zg * jax.nn.sigmoid(zg))).astype(o_ref.dtype)

    inverse(1, finish_out)


def _hyena(proj3, conv_w, conv_b, hyena_bias, cmat, smat_fwd, smat_inv, kr, ki, kn, l, nb, tc=256,
           seq_chunk=512, freq_chunk=512):
    B, L, _ = proj3.shape
    b = L // nb
    nd = 2 * nb - 1
    width = hyena_bias.shape[2]
    nct = width // tc
    sig = lambda part: pl.BlockSpec((None, L, tc), lambda j, bb: (bb, 0, part * nct + j))
    cw = lambda part: pl.BlockSpec((None, 3, tc), lambda j, bb: (l, 0, part * nct + j))
    cb = lambda part: pl.BlockSpec((None, 1, tc), lambda j, bb: (l, 0, part * nct + j))
    kspec = pl.BlockSpec((2, nd, b, tc), lambda j, bb: (0, 0, 0, j))
    kern = functools.partial(_hyena_kernel, nb=nb, seq_chunk=seq_chunk, freq_chunk=freq_chunk)
    return pl.pallas_call(
        kern,
        out_shape=jax.ShapeDtypeStruct((B, L, width), BF16),
        grid=(nct, B),
        in_specs=[sig(0), sig(1), sig(2), sig(3),
                  cw(0), cw(1), cw(2), cb(0), cb(1), cb(2),
                  pl.BlockSpec((None, 2, tc), lambda j, bb: (l, 0, j)),
                  _const_spec((b, b)), _const_spec((b, b)), _const_spec((b, b)),
                  kspec, kspec,
                  pl.BlockSpec((2, nd, 1, tc), lambda j, bb: (0, 0, 0, j))],
        out_specs=pl.BlockSpec((None, L, tc), lambda j, bb: (bb, 0, j)),
        scratch_shapes=[pltpu.VMEM((L, tc), F32), pltpu.VMEM((b, nb * tc), BF16),
                        pltpu.VMEM((b, nb * tc), BF16), pltpu.VMEM((b, nb * tc), BF16)],
        compiler_params=_params(("parallel", "arbitrary")),
        name="hyena_mixer",
    )(proj3, proj3, proj3, proj3, conv_w, conv_w, conv_w, conv_b, conv_b, conv_b,
      hyena_bias, cmat, smat_fwd, smat_inv, kr, ki, kn)


def _rope(x, cos, sin_lo, sin_hi):
    return (x * cos + pltpu.roll(x, HEAD_DIM - ROPE_HALF, 1) * sin_lo
            + pltpu.roll(x, ROPE_HALF, 1) * sin_hi)


def _attn_kernel(sink_ref, q_ref, k_ref, v_ref, zg_ref, cos_ref, slo_ref, shi_ref, o_ref,
                 kb_ref, vt_ref, *, layer):
    L = k_ref.shape[0]
    qb = q_ref.shape[0]
    kw = QSUB + 2 * WINDOW
    gq = GROUP * QSUB
    kvh = pl.program_id(1)
    n = pl.program_id(2)

    @pl.when(n == 0)
    def _():
        kb_ref[...] = _rope(k_ref[...], cos_ref[...], slo_ref[...], shi_ref[...]).astype(BF16)
        vt_ref[...] = v_ref[...].T.astype(BF16)

    log2e = math.log2(math.e)
    lane_head = lax.broadcasted_iota(jnp.int32, (1, gq), 1) // QSUB
    sk2 = jnp.zeros((1, gq), F32)
    for g in range(GROUP):
        sk2 = jnp.where(lane_head == g, sink_ref[layer, kvh * GROUP + g] * log2e, sk2)

    for sb in range(qb // QSUB):
        rows = slice(sb * QSUB, (sb + 1) * QSUB)
        q0 = pl.multiple_of(n * qb + sb * QSUB, QSUB)
        start = pl.multiple_of(jnp.clip(q0 - WINDOW, 0, L - kw), WINDOW)
        win = pl.ds(start, kw)
        tab = pl.ds(q0, QSUB)
        cq, slq, shq = cos_ref[tab, :], slo_ref[tab, :], shi_ref[tab, :]
        q = jnp.concatenate(
            [_rope(q_ref[rows, g * HEAD_DIM:(g + 1) * HEAD_DIM], cq, slq, shq).astype(BF16)
             for g in range(GROUP)], axis=0)
        raw = lax.dot_general(kb_ref[win, :], q, (((1,), (1,)), ((), ())),
                              preferred_element_type=F32)
        kpos = start + lax.broadcasted_iota(jnp.int32, (kw, 1), 0)
        qpos = q0 + lax.broadcasted_iota(jnp.int32, (1, QSUB), 1)
        cap = jnp.where(jnp.abs(kpos - qpos) <= WINDOW, jnp.inf, -jnp.inf)
        raw = jnp.minimum(raw, jnp.concatenate([cap] * GROUP, axis=1))
        c = (HEAD_DIM ** -0.5) * log2e
        m2 = jnp.maximum(jnp.max(raw, axis=0, keepdims=True) * c, sk2)
        p = jnp.exp2(raw * c - m2)
        denom = jnp.sum(p, axis=0, keepdims=True) + jnp.exp2(sk2 - m2)
        ot = jnp.dot(vt_ref[:, win], p.astype(BF16), preferred_element_type=F32) / denom
        for g in range(GROUP):
            cols = slice(g * HEAD_DIM, (g + 1) * HEAD_DIM)
            o = ot[:, g * QSUB:(g + 1) * QSUB].T
            zg = zg_ref[rows, cols]
            o_ref[rows, cols] = (o * (zg * jax.nn.sigmoid(zg))).astype(o_ref.dtype)


def _attention(proj3, sink, rope_tabs, col_q, col_k, col_v, col_zg, l, qb=2048):
    B, L, _ = proj3.shape
    gw = GROUP * HEAD_DIM
    cos_t, sin_lo, sin_hi = rope_tabs
    tab = pl.BlockSpec((L, HEAD_DIM), lambda b, h, n: (0, 0))
    return pl.pallas_call(
        functools.partial(_attn_kernel, layer=l),
        out_shape=jax.ShapeDtypeStruct((B, L, N_HEADS * HEAD_DIM), BF16),
        grid=(B, N_KV_HEADS, L // qb),
        in_specs=[pl.BlockSpec(memory_space=pltpu.SMEM),
                  pl.BlockSpec((None, qb, gw), lambda b, h, n: (b, n, col_q // gw + h)),
                  pl.BlockSpec((None, L, HEAD_DIM), lambda b, h, n: (b, 0, col_k // HEAD_DIM + h)),
                  pl.BlockSpec((None, L, HEAD_DIM), lambda b, h, n: (b, 0, col_v // HEAD_DIM + h)),
                  pl.BlockSpec((None, qb, gw), lambda b, h, n: (b, n, col_zg // gw + h)),
                  tab, tab, tab],
        out_specs=pl.BlockSpec((None, qb, gw), lambda b, h, n: (b, n, h)),
        scratch_shapes=[pltpu.VMEM((L, HEAD_DIM), BF16), pltpu.VMEM((HEAD_DIM, L), BF16)],
        compiler_params=_params(("parallel", "parallel", "arbitrary")),
        name="window_attention",
    )(sink, proj3, proj3, proj3, proj3, cos_t, sin_lo, sin_hi)


def _merge_out_kernel(*refs, n_gate_blocks, final):
    x_ref, yh_ref, ya_ref = refs[0:3]
    gh_refs = refs[3:3 + n_gate_blocks]
    ga_refs = refs[3 + n_gate_blocks:3 + 2 * n_gate_blocks]
    who_ref, wao_ref, wout_ref, fg_ref, o_ref, m_ref = refs[3 + 2 * n_gate_blocks:]
    gw = gh_refs[0].shape[1]
    yh = yh_ref[...]
    ya = ya_ref[...]
    for c in range(n_gate_blocks):
        cols = slice(c * gw, (c + 1) * gw)
        ph = jnp.dot(yh, who_ref[:, cols], preferred_element_type=F32)
        pa = jnp.dot(ya, wao_ref[:, cols], preferred_element_type=F32)
        merged = jax.nn.sigmoid(gh_refs[c][...]) * ph + jax.nn.sigmoid(ga_refs[c][...]) * pa
        m_ref[:, cols] = merged.astype(BF16)
    out = x_ref[...] + jnp.dot(m_ref[...], wout_ref[...], preferred_element_type=F32)
    if final:
        ms = jnp.mean(out * out, axis=-1, keepdims=True)
        out = out * lax.rsqrt(ms + EPS) * fg_ref[...]
    o_ref[...] = out


def _merge_out(x2d, proj, yh, ya, who, wao, wout, final_g, col_gh, col_ga, l, final, tm=256, gw=512):
    m, d = x2d.shape
    width = yh.shape[1]
    ngb = d // gw
    gspec = lambda col0, c: pl.BlockSpec((tm, gw), lambda i: (i, col0 // gw + c))
    wspec = lambda rows: pl.BlockSpec((None, rows, d), lambda i: (l, 0, 0), pipeline_mode=pl.Buffered(1))
    kern = functools.partial(_merge_out_kernel, n_gate_blocks=ngb, final=final)
    return pl.pallas_call(
        kern,
        out_shape=jax.ShapeDtypeStruct((m, d), F32),
        grid=(m // tm,),
        in_specs=[pl.BlockSpec((tm, d), lambda i: (i, 0)),
                  pl.BlockSpec((tm, width), lambda i: (i, 0)),
                  pl.BlockSpec((tm, width), lambda i: (i, 0))]
                 + [gspec(col_gh, c) for c in range(ngb)]
                 + [gspec(col_ga, c) for c in range(ngb)]
                 + [wspec(width), wspec(width), wspec(d),
                    pl.BlockSpec((1, d), lambda i: (0, 0))],
        out_specs=pl.BlockSpec((tm, d), lambda i: (i, 0)),
        scratch_shapes=[pltpu.VMEM((tm, d), BF16)],
        compiler_params=_params(("parallel",)),
        name="merge_out",
    )(x2d, yh, ya, *([proj] * (2 * ngb)), who, wao, wout, final_g.reshape(1, d))


def _dft_mats(b):
    idx = jnp.arange(b, dtype=jnp.int32)
    k = (idx[:, None] * idx[None, :]) % (2 * b)
    ang = k.astype(F32) * (math.pi / b)
    return jnp.cos(ang).astype(BF16), jnp.sin(ang).astype(BF16)


def _rope_tabs(L):
    inv = ROPE_THETA ** (-jnp.arange(0, ROPE_DIM, 2, dtype=F32) / ROPE_DIM)
    ang = jnp.arange(L, dtype=F32)[:, None] * inv[None, :]
    cos, sin = jnp.cos(ang), jnp.sin(ang)
    ones = jnp.ones((L, HEAD_DIM - ROPE_DIM), F32)
    zeros = jnp.zeros((L, HEAD_DIM - ROPE_HALF), F32)
    cos_t = jnp.concatenate([cos, cos, ones], axis=1)
    sin_lo = jnp.concatenate([-sin, zeros], axis=1)
    sin_hi = jnp.concatenate([jnp.zeros((L, ROPE_HALF), F32), sin, zeros[:, ROPE_HALF:]], axis=1)
    return cos_t, sin_lo, sin_hi


def _filter_feats(L):
    t = jnp.linspace(0.0, 1.0, L, dtype=F32)[:, None]
    bands = jnp.linspace(1e-4, FILTER_BANDS - 1, FILTER_BANDS, dtype=F32)[None, :]
    ang = (2.0 * math.pi / L) * jnp.arange(L, dtype=F32)[:, None] * bands
    feats = jnp.concatenate([t, jnp.cos(ang), -jnp.sin(ang)], axis=-1)
    feats = jnp.pad(feats, ((0, 0), (0, FEAT_PAD - feats.shape[1])))
    return feats, t


def kernel(x, norm_g, w_in, conv_w, conv_b, filt_w1, filt_b1, filt_w2, filt_b2, filt_w3, filt_b3,
           filt_w4, filt_freq, hyena_bias, attn_sink, w_hyena_out, w_attn_out, w_out, final_norm):
    B, L, D = x.shape
    depth = norm_g.shape[0]
    hw = hyena_bias.shape[2]
    aw = N_HEADS * HEAD_DIM
    kvw = N_KV_HEADS * HEAD_DIM
    sizes = (3 * hw, hw, aw, kvw, kvw, aw, D, D)
    cols = [0]
    for s in sizes:
        cols.append(cols[-1] + s)
    col_zhy, col_q, col_k, col_v, col_zat, col_gh, col_ga = cols[1:8]

    cmat, smat = _dft_mats(L // NB)
    nyq_cos = jnp.where(jnp.arange(L // NB) % 2 == 0, 1.0, -1.0).astype(BF16)
    smat_fwd = smat.at[0, :].set(nyq_cos)
    smat_inv = smat.at[:, 0].set(nyq_cos)
    rope_tabs = _rope_tabs(L)
    feats, tcol = _filter_feats(L)
    deltas = jnp.abs(jnp.linspace(MIN_DECAY, MAX_DECAY, hw, dtype=F32))[None, :]

    w_in_b, who_b, wao_b, wout_b = (w.astype(BF16) for w in (w_in, w_hyena_out, w_attn_out, w_out))
    row = lambda a: a[:, None, :]
    w1p = jnp.pad(filt_w1, ((0, 0), (0, FEAT_PAD - filt_w1.shape[1]), (0, 0)))
    xf = x.reshape(B * L, D)
    for l in range(depth):
        proj = _norm_proj(xf, row(norm_g), w_in_b, l)
        proj3 = proj.reshape(B, L, -1)
        kr, ki, kn = _filters(feats, tcol, w1p, row(filt_b1), filt_w2, row(filt_b2), filt_w3, row(filt_b3),
                              row(filt_freq), filt_w4, deltas, cmat, smat, l, NB)
        y_hy = _hyena(proj3, conv_w, row(conv_b), hyena_bias, cmat, smat_fwd, smat_inv, kr, ki, kn, l, NB)
        y_at = _attention(proj3, attn_sink, rope_tabs, col_q, col_k, col_v, col_zat, l)
        xf = _merge_out(xf, proj, y_hy.reshape(B * L, hw), y_at.reshape(B * L, aw),
                        who_b, wao_b, wout_b, final_norm, col_gh, col_ga, l, final=(l == depth - 1))
    return xf.reshape(B, L, D)
```

```python
import functools
import math

import jax
import jax.numpy as jnp
from jax import lax
from jax.experimental import pallas as pl
from jax.experimental.pallas import tpu as pltpu

F32 = jnp.float32
BF16 = jnp.bfloat16

HEAD_DIM = 128
N_HEADS = 8
N_KV_HEADS = 2
GROUP = N_HEADS // N_KV_HEADS
WINDOW = 128
QSUB = 128
ROPE_THETA = 500000.0
ROPE_DIM = HEAD_DIM // 4
ROPE_HALF = ROPE_DIM // 2
EPS = 1e-6
FILTER_BANDS = 16
FEAT_PAD = 128
DECAY_TARGET = 1e-2
MIN_DECAY = math.log(DECAY_TARGET) / 0.3
MAX_DECAY = math.log(DECAY_TARGET) / 1.5

NB = 4
BF16_ROWS = 16
VMEM_LIMIT = 56 * 1024 * 1024


def _params(sem, vmem=VMEM_LIMIT):
    return pltpu.CompilerParams(dimension_semantics=sem, vmem_limit_bytes=vmem)


def _const_spec(shape):
    return pl.BlockSpec(shape, lambda *_: (0,) * len(shape), pipeline_mode=pl.Buffered(1))


def _split_bf16(x):
    hi = x.astype(BF16)
    return hi, (x - hi.astype(F32)).astype(BF16)


def _alt_sign(rows):
    return jnp.where((lax.broadcasted_iota(jnp.int32, (rows, 1), 0) & 1) == 0, 1.0, -1.0)


def _norm_proj_kernel(x_ref, g_ref, w_ref, o_ref, h_ref):
    @pl.when(pl.program_id(1) == 0)
    def _():
        x = x_ref[...]
        ms = jnp.mean(x * x, axis=-1, keepdims=True)
        h_ref[...] = (x * lax.rsqrt(ms + EPS) * g_ref[...]).astype(BF16)

    o_ref[...] = jnp.dot(h_ref[...], w_ref[...], preferred_element_type=F32)


def _norm_proj(x2d, g_all, w_all, l, tm=1024, tn=1792):
    m, d = x2d.shape
    n = w_all.shape[2]
    return pl.pallas_call(
        _norm_proj_kernel,
        out_shape=jax.ShapeDtypeStruct((m, n), F32),
        grid=(m // tm, n // tn),
        in_specs=[
            pl.BlockSpec((tm, d), lambda i, j: (i, 0)),
            pl.BlockSpec((None, 1, d), lambda i, j: (l, 0, 0)),
            pl.BlockSpec((None, d, tn), lambda i, j: (l, 0, j)),
        ],
        out_specs=pl.BlockSpec((tm, tn), lambda i, j: (i, j)),
        scratch_shapes=[pltpu.VMEM((tm, d), BF16)],
        compiler_params=_params(("parallel", "arbitrary")),
        name="norm_proj",
    )(x2d, g_all, w_all)


def _filter_kernel(feats_ref, t_ref, w1_ref, b1_ref, w2_ref, b2_ref, w3_ref, b3_ref, fr_ref,
                   w4f_ref, w4b_ref, dl_ref, c_ref, s_ref,
                   kr_ref, ki_ref, kn_ref,
                   hhi_ref, hlo_ref, xf_ref, xb_ref, cf_ref, sf_ref, cb_ref, sb_ref, *, nb, row_chunk):
    L = feats_ref.shape[0]
    b = L // nb
    tc = dl_ref.shape[1]
    hp = lax.Precision.HIGHEST
    nchunks = b // row_chunk
    alt = _alt_sign(row_chunk)

    @pl.when((pl.program_id(0) == 0) & (pl.program_id(1) == 0))
    def _():
        half = L // 2
        hid = w2_ref.shape[0]
        twice = lambda a: jnp.concatenate([a, a], axis=1)

        def blockdiag(w):
            z = jnp.zeros_like(w)
            return jnp.concatenate([jnp.concatenate([w, z], axis=1), jnp.concatenate([z, w], axis=1)], axis=0)

        fr = twice(fr_ref[...])
        h = jnp.concatenate([feats_ref[0:half, :], feats_ref[half:, :]], axis=1)
        for w_ref, b_ref in ((w1_ref, b1_ref), (w2_ref, b2_ref), (w3_ref, b3_ref)):
            h = jnp.sin(fr * (jnp.dot(h, blockdiag(w_ref[...]), precision=hp,
                                      preferred_element_type=F32) + twice(b_ref[...])))
        hi, lo = _split_bf16(h)
        hhi_ref[0:half, :], hhi_ref[half:, :] = hi[:, :hid], hi[:, hid:]
        hlo_ref[0:half, :], hlo_ref[half:, :] = lo[:, :hid], lo[:, hid:]

    def dot_split(hh, hl, w_ref):
        wh, wl = _split_bf16(w_ref[...])
        return (jnp.dot(hh, wh, preferred_element_type=F32) + jnp.dot(hh, wl, preferred_element_type=F32)
                + jnp.dot(hl, wh, preferred_element_type=F32))

    def taps(rows):
        hh, hl = hhi_ref[rows, :], hlo_ref[rows, :]
        decay = jnp.exp(-t_ref[rows, :] * dl_ref[...])
        return dot_split(hh, hl, w4f_ref) * decay, dot_split(hh, hl, w4b_ref) * decay

    zero_row = jnp.zeros((1, tc), F32)
    f0, b0, af, ab = [], [], [], []
    for q in range(nb):
        head_f, head_b = taps(pl.ds(q * b, BF16_ROWS))
        f0.append(head_f[0:1, :])
        b0.append(head_b[0:1, :])

        def tap_chunk(i, carry, q=q):
            l0 = pl.multiple_of(i * row_chunk, row_chunk)
            fwd, bwd = taps(pl.ds(pl.multiple_of(q * b + l0, row_chunk), row_chunk))
            first = (lax.broadcasted_iota(jnp.int32, (row_chunk, 1), 0) + l0) == 0
            fz = jnp.where(first, 0.0, fwd)
            bz = jnp.where(first, 0.0, bwd)
            xf_ref[q, pl.ds(l0, row_chunk), :] = fz.astype(BF16)
            xb_ref[q, pl.ds(l0, row_chunk), :] = bz.astype(BF16)
            return (carry[0] + jnp.sum(fz * alt, axis=0, keepdims=True),
                    carry[1] + jnp.sum(bz * alt, axis=0, keepdims=True))

        a_f, a_b = lax.fori_loop(0, nchunks, tap_chunk, (zero_row, zero_row))
        af.append(a_f)
        ab.append(a_b)

        def spec_chunk(i, carry, q=q):
            r = pl.ds(pl.multiple_of(i * row_chunk, row_chunk), row_chunk)
            cf_ref[q, r, :] = jnp.dot(c_ref[r, :], xf_ref[q], preferred_element_type=F32)
            sf_ref[q, r, :] = jnp.dot(s_ref[r, :], xf_ref[q], preferred_element_type=F32)
            cb_ref[q, r, :] = jnp.dot(c_ref[r, :], xb_ref[q], preferred_element_type=F32)
            sb_ref[q, r, :] = jnp.dot(s_ref[r, :], xb_ref[q], preferred_element_type=F32)
            return carry

        lax.fori_loop(0, nchunks, spec_chunk, 0)

    inv_n = 1.0 / (2 * b)
    for d in range(-(nb - 1), nb):
        slot = d + nb - 1
        e = -d
        if d >= 1:
            kn = af[d] + f0[d] + af[d - 1]
        elif d == 0:
            kn = af[0] + f0[0] + ab[0]
        else:
            kn = b0[e] + ab[e] + ab[e - 1]
        kn_ref[slot] = kn * inv_n

        def combine(i, carry, d=d, e=e, slot=slot):
            r0 = pl.multiple_of(i * row_chunk, row_chunk)
            r = pl.ds(r0, row_chunk)
            first = (lax.broadcasted_iota(jnp.int32, (row_chunk, 1), 0) + r0) == 0
            wgt = jnp.where(first, inv_n, 2.0 * inv_n)
            if d >= 1:
                kr = cf_ref[d, r, :] + f0[d] + alt * cf_ref[d - 1, r, :]
                ki = -sf_ref[d, r, :] - alt * sf_ref[d - 1, r, :]
            elif d == 0:
                kr = cf_ref[0, r, :] + f0[0] + cb_ref[0, r, :]
                ki = sb_ref[0, r, :] - sf_ref[0, r, :]
            else:
                kr = b0[e] + cb_ref[e, r, :] + alt * cb_ref[e - 1, r, :]
                ki = sb_ref[e, r, :] + alt * sb_ref[e - 1, r, :]
            kr_ref[slot, r, :] = (kr * wgt).astype(kr_ref.dtype)
            ki_ref[slot, r, :] = (ki * wgt).astype(ki_ref.dtype)
            return carry

        lax.fori_loop(0, nchunks, combine, 0)


def _filters(feats, tcol, w1p, b1, w2, b2, w3, b3, freq, w4, deltas, cmat, smat, l, nb, tc=256,
             row_chunk=512):
    L = feats.shape[0]
    b = L // nb
    nd = 2 * nb - 1
    width = deltas.shape[1]
    nct = width // tc
    hid = w2.shape[1]
    small = lambda a: pl.BlockSpec(a.shape, lambda o, c: (0,) * a.ndim)
    layer = lambda a: pl.BlockSpec((None,) + a.shape[1:], lambda o, c: (l,) + (0,) * (a.ndim - 1))
    kern = functools.partial(_filter_kernel, nb=nb, row_chunk=row_chunk)
    kspec = pl.BlockSpec((None, nd, b, tc), lambda o, c: (o, 0, 0, c))
    blk = lambda dt: pltpu.VMEM((nb, b, tc), dt)
    return pl.pallas_call(
        kern,
        out_shape=(jax.ShapeDtypeStruct((2, nd, b, width), BF16),
                   jax.ShapeDtypeStruct((2, nd, b, width), BF16),
                   jax.ShapeDtypeStruct((2, nd, 1, width), F32)),
        grid=(2, nct),
        in_specs=[small(feats), small(tcol), layer(w1p), layer(b1), layer(w2), layer(b2),
                  layer(w3), layer(b3), layer(freq),
                  pl.BlockSpec((None, hid, tc), lambda o, c: (l, 0, o * 2 * nct + c)),
                  pl.BlockSpec((None, hid, tc), lambda o, c: (l, 0, o * 2 * nct + nct + c)),
                  pl.BlockSpec((1, tc), lambda o, c: (0, c)),
                  _const_spec((b, b)), _const_spec((b, b))],
        out_specs=(kspec, kspec, pl.BlockSpec((None, nd, 1, tc), lambda o, c: (o, 0, 0, c))),
        scratch_shapes=[pltpu.VMEM((L, hid), BF16), pltpu.VMEM((L, hid), BF16), blk(BF16), blk(BF16),
                        blk(F32), blk(F32), blk(F32), blk(F32)],
        compiler_params=_params(("arbitrary", "arbitrary")),
        name="hyena_filters",
    )(feats, tcol, w1p, b1, w2, b2, w3, b3, freq, w4, w4, deltas, cmat, smat)


def _sconv_chunk(u_ref, w_ref, b_ref, i, rows, nchunks):
    r0 = i * rows
    tc = u_ref.shape[1]
    u = u_ref[r0:r0 + rows, :]
    zero = jnp.zeros((1, tc), F32)
    up = u_ref[r0 - 8:r0, :][7:8, :] if i > 0 else zero
    dn = u_ref[r0 + rows:r0 + rows + 8, :][0:1, :] if i < nchunks - 1 else zero
    row = lax.broadcasted_iota(jnp.int32, (rows, 1), 0)
    prev = jnp.where(row == 0, up, pltpu.roll(u, 1, 0))
    nxt = jnp.where(row == rows - 1, dn, pltpu.roll(u, rows - 1, 0))
    return b_ref[...] + prev * w_ref[0:1, :] + u * w_ref[1:2, :] + nxt * w_ref[2:3, :]


def _hyena_kernel(v_ref, x1_ref, x2_ref, zg_ref, wv_ref, wx1_ref, wx2_ref, bv_ref, bx1_ref, bx2_ref,
                  hb_ref, c_ref, sf_ref, si_ref, kr_ref, ki_ref, kn_ref, o_ref,
                  u_ref, ub_ref, a_ref, bn_ref, *, nb, seq_chunk, freq_chunk):
    L, tc = u_ref.shape
    b = L // nb
    per_block = b // seq_chunk
    nseq = L // seq_chunk

    def lanes(j):
        return slice(j * tc, (j + 1) * tc)

    def put_signal(chunk, val):
        j, local = divmod(chunk, per_block)
        rows = slice(chunk * seq_chunk, (chunk + 1) * seq_chunk)
        u_ref[rows, :] = val
        ub_ref[local * seq_chunk:(local + 1) * seq_chunk, lanes(j)] = val.astype(BF16)

    def forward(order):
        for fc in range(b // freq_chunk):
            r = slice(fc * freq_chunk, (fc + 1) * freq_chunk)
            ur = jnp.dot(c_ref[r, :], ub_ref[...], preferred_element_type=F32).astype(BF16)
            us = jnp.dot(sf_ref[r, :], ub_ref[...], preferred_element_type=F32).astype(BF16)
            for i in range(nb):
                acc_a = acc_b = None
                for j in range(nb):
                    kr = kr_ref[order, i - j + nb - 1, r, :]
                    ki = ki_ref[order, i - j + nb - 1, r, :]
                    urj, usj = ur[:, lanes(j)], us[:, lanes(j)]
                    ta = urj * kr + usj * ki
                    tb = usj * kr - urj * ki
                    acc_a = ta if acc_a is None else acc_a + ta
                    acc_b = tb if acc_b is None else acc_b + tb
                if fc == 0:
                    nyq = None
                    for j in range(nb):
                        t = us[0:1, lanes(j)].astype(F32) * kn_ref[order, i - j + nb - 1]
                        nyq = t if nyq is None else nyq + t
                    first = lax.broadcasted_iota(jnp.int32, (BF16_ROWS, 1), 0) == 0
                    top = jnp.where(first, nyq.astype(BF16), acc_b[:BF16_ROWS])
                    acc_b = jnp.concatenate([top, acc_b[BF16_ROWS:]], axis=0)
                a_ref[r, lanes(i)] = acc_a
                bn_ref[r, lanes(i)] = acc_b

    def inverse(order, finish):
        for tcn in range(per_block):
            r = slice(tcn * seq_chunk, (tcn + 1) * seq_chunk)
            y2 = jnp.dot(c_ref[r, :], a_ref[...], preferred_element_type=F32)
            y2 = y2 + jnp.dot(si_ref[r, :], bn_ref[...], preferred_element_type=F32)
            for i in range(nb):
                chunk = i * per_block + tcn
                rows = slice(chunk * seq_chunk, (chunk + 1) * seq_chunk)
                finish(chunk, rows, y2[:, lanes(i)] + u_ref[rows, :] * hb_ref[order:order + 1, :])

    for chunk in range(nseq):
        put_signal(chunk, _sconv_chunk(v_ref, wv_ref, bv_ref, chunk, seq_chunk, nseq))

    forward(0)

    def finish_z(chunk, rows, y):
        put_signal(chunk, _sconv_chunk(x1_ref, wx1_ref, bx1_ref, chunk, seq_chunk, nseq) * y)

    inverse(0, finish_z)
    forward(1)

    def finish_out(chunk, rows, y):
        y = _sconv_chunk(x2_ref, wx2_ref, bx2_ref, chunk, seq_chunk, nseq) * y
        zg = zg_ref[rows, :]
        o_ref[rows, :] = (y * (zg * jax.nn.sigmoid(zg))).astype(o_ref.dtype)

    inverse(1, finish_out)


def _hyena(proj3, conv_w, conv_b, hyena_bias, cmat, smat_fwd, smat_inv, kr, ki, kn, l, nb, tc=256,
           seq_chunk=512, freq_chunk=512):
    B, L, _ = proj3.shape
    b = L // nb
    nd = 2 * nb - 1
    width = hyena_bias.shape[2]
    nct = width // tc
    sig = lambda part: pl.BlockSpec((None, L, tc), lambda j, bb: (bb, 0, part * nct + j))
    cw = lambda part: pl.BlockSpec((None, 3, tc), lambda j, bb: (l, 0, part * nct + j))
    cb = lambda part: pl.BlockSpec((None, 1, tc), lambda j, bb: (l, 0, part * nct + j))
    kspec = pl.BlockSpec((2, nd, b, tc), lambda j, bb: (0, 0, 0, j))
    kern = functools.partial(_hyena_kernel, nb=nb, seq_chunk=seq_chunk, freq_chunk=freq_chunk)
    return pl.pallas_call(
        kern,
        out_shape=jax.ShapeDtypeStruct((B, L, width), BF16),
        grid=(nct, B),
        in_specs=[sig(0), sig(1), sig(2), sig(3),
                  cw(0), cw(1), cw(2), cb(0), cb(1), cb(2),
                  pl.BlockSpec((None, 2, tc), lambda j, bb: (l, 0, j)),
                  _const_spec((b, b)), _const_spec((b, b)), _const_spec((b, b)),
                  kspec, kspec,
                  pl.BlockSpec((2, nd, 1, tc), lambda j, bb: (0, 0, 0, j))],
        out_specs=pl.BlockSpec((None, L, tc), lambda j, bb: (bb, 0, j)),
        scratch_shapes=[pltpu.VMEM((L, tc), F32), pltpu.VMEM((b, nb * tc), BF16),
                        pltpu.VMEM((b, nb * tc), BF16), pltpu.VMEM((b, nb * tc), BF16)],
        compiler_params=_params(("parallel", "arbitrary")),
        name="hyena_mixer",
    )(proj3, proj3, proj3, proj3, conv_w, conv_w, conv_w, conv_b, conv_b, conv_b,
      hyena_bias, cmat, smat_fwd, smat_inv, kr, ki, kn)


def _rope(x, cos, sin_lo, sin_hi):
    return (x * cos + pltpu.roll(x, HEAD_DIM - ROPE_HALF, 1) * sin_lo
            + pltpu.roll(x, ROPE_HALF, 1) * sin_hi)


def _attn_kernel(sink_ref, q_ref, k_ref, v_ref, zg_ref, cos_ref, slo_ref, shi_ref, o_ref,
                 kb_ref, vt_ref, *, layer):
    L = k_ref.shape[0]
    qb = q_ref.shape[0]
    kw = QSUB + 2 * WINDOW
    gq = GROUP * QSUB
    kvh = pl.program_id(1)
    n = pl.program_id(2)

    @pl.when(n == 0)
    def _():
        kb_ref[...] = _rope(k_ref[...], cos_ref[...], slo_ref[...], shi_ref[...]).astype(BF16)
        vt_ref[...] = v_ref[...].T.astype(BF16)

    log2e = math.log2(math.e)
    lane_head = lax.broadcasted_iota(jnp.int32, (1, gq), 1) // QSUB
    sk2 = jnp.zeros((1, gq), F32)
    for g in range(GROUP):
        sk2 = jnp.where(lane_head == g, sink_ref[layer, kvh * GROUP + g] * log2e, sk2)

    for sb in range(qb // QSUB):
        rows = slice(sb * QSUB, (sb + 1) * QSUB)
        q0 = pl.multiple_of(n * qb + sb * QSUB, QSUB)
        start = pl.multiple_of(jnp.clip(q0 - WINDOW, 0, L - kw), WINDOW)
        win = pl.ds(start, kw)
        tab = pl.ds(q0, QSUB)
        cq, slq, shq = cos_ref[tab, :], slo_ref[tab, :], shi_ref[tab, :]
        q = jnp.concatenate(
            [_rope(q_ref[rows, g * HEAD_DIM:(g + 1) * HEAD_DIM], cq, slq, shq).astype(BF16)
             for g in range(GROUP)], axis=0)
        raw = lax.dot_general(kb_ref[win, :], q, (((1,), (1,)), ((), ())),
                              preferred_element_type=F32)
        kpos = start + lax.broadcasted_iota(jnp.int32, (kw, 1), 0)
        qpos = q0 + lax.broadcasted_iota(jnp.int32, (1, QSUB), 1)
        cap = jnp.where(jnp.abs(kpos - qpos) <= WINDOW, jnp.inf, -jnp.inf)
        raw = jnp.minimum(raw, jnp.concatenate([cap] * GROUP, axis=1))
        c = (HEAD_DIM ** -0.5) * log2e
        m2 = jnp.maximum(jnp.max(raw, axis=0, keepdims=True) * c, sk2)
        p = jnp.exp2(raw * c - m2)
        denom = jnp.sum(p, axis=0, keepdims=True) + jnp.exp2(sk2 - m2)
        ot = jnp.dot(vt_ref[:, win], p.astype(BF16), preferred_element_type=F32) / denom
        for g in range(GROUP):
            cols = slice(g * HEAD_DIM, (g + 1) * HEAD_DIM)
            o = ot[:, g * QSUB:(g + 1) * QSUB].T
            zg = zg_ref[rows, cols]
            o_ref[rows, cols] = (o * (zg * jax.nn.sigmoid(zg))).astype(o_ref.dtype)


def _attention(proj3, sink, rope_tabs, col_q, col_k, col_v, col_zg, l, qb=2048):
    B, L, _ = proj3.shape
    gw = GROUP * HEAD_DIM
    cos_t, sin_lo, sin_hi = rope_tabs
    tab = pl.BlockSpec((L, HEAD_DIM), lambda b, h, n: (0, 0))
    return pl.pallas_call(
        functools.partial(_attn_kernel, layer=l),
        out_shape=jax.ShapeDtypeStruct((B, L, N_HEADS * HEAD_DIM), BF16),
        grid=(B, N_KV_HEADS, L // qb),
        in_specs=[pl.BlockSpec(memory_space=pltpu.SMEM),
                  pl.BlockSpec((None, qb, gw), lambda b, h, n: (b, n, col_q // gw + h)),
                  pl.BlockSpec((None, L, HEAD_DIM), lambda b, h, n: (b, 0, col_k // HEAD_DIM + h)),
                  pl.BlockSpec((None, L, HEAD_DIM), lambda b, h, n: (b, 0, col_v // HEAD_DIM + h)),
                  pl.BlockSpec((None, qb, gw), lambda b, h, n: (b, n, col_zg // gw + h)),
                  tab, tab, tab],
        out_specs=pl.BlockSpec((None, qb, gw), lambda b, h, n: (b, n, h)),
        scratch_shapes=[pltpu.VMEM((L, HEAD_DIM), BF16), pltpu.VMEM((HEAD_DIM, L), BF16)],
        compiler_params=_params(("parallel", "parallel", "arbitrary")),
        name="window_attention",
    )(sink, proj3, proj3, proj3, proj3, cos_t, sin_lo, sin_hi)


def _merge_out_kernel(*refs, n_gate_blocks, final):
    x_ref, yh_ref, ya_ref = refs[0:3]
    gh_refs = refs[3:3 + n_gate_blocks]
    ga_refs = refs[3 + n_gate_blocks:3 + 2 * n_gate_blocks]
    who_ref, wao_ref, wout_ref, fg_ref, o_ref, m_ref = refs[3 + 2 * n_gate_blocks:]
    gw = gh_refs[0].shape[1]
    yh = yh_ref[...]
    ya = ya_ref[...]
    for c in range(n_gate_blocks):
        cols = slice(c * gw, (c + 1) * gw)
        ph = jnp.dot(yh, who_ref[:, cols], preferred_element_type=F32)
        pa = jnp.dot(ya, wao_ref[:, cols], preferred_element_type=F32)
        merged = jax.nn.sigmoid(gh_refs[c][...]) * ph + jax.nn.sigmoid(ga_refs[c][...]) * pa
        m_ref[:, cols] = merged.astype(BF16)
    out = x_ref[...] + jnp.dot(m_ref[...], wout_ref[...], preferred_element_type=F32)
    if final:
        ms = jnp.mean(out * out, axis=-1, keepdims=True)
        out = out * lax.rsqrt(ms + EPS) * fg_ref[...]
    o_ref[...] = out


def _merge_out(x2d, proj, yh, ya, who, wao, wout, final_g, col_gh, col_ga, l, final, tm=256, gw=512):
    m, d = x2d.shape
    width = yh.shape[1]
    ngb = d // gw
    gspec = lambda col0, c: pl.BlockSpec((tm, gw), lambda i: (i, col0 // gw + c))
    wspec = lambda rows: pl.BlockSpec((None, rows, d), lambda i: (l, 0, 0), pipeline_mode=pl.Buffered(1))
    kern = functools.partial(_merge_out_kernel, n_gate_blocks=ngb, final=final)
    return pl.pallas_call(
        kern,
        out_shape=jax.ShapeDtypeStruct((m, d), F32),
        grid=(m // tm,),
        in_specs=[pl.BlockSpec((tm, d), lambda i: (i, 0)),
                  pl.BlockSpec((tm, width), lambda i: (i, 0)),
                  pl.BlockSpec((tm, width), lambda i: (i, 0))]
                 + [gspec(col_gh, c) for c in range(ngb)]
                 + [gspec(col_ga, c) for c in range(ngb)]
                 + [wspec(width), wspec(width), wspec(d),
                    pl.BlockSpec((1, d), lambda i: (0, 0))],
        out_specs=pl.BlockSpec((tm, d), lambda i: (i, 0)),
        scratch_shapes=[pltpu.VMEM((tm, d), BF16)],
        compiler_params=pltpu.CompilerParams(
            dimension_semantics=("parallel",), vmem_limit_bytes=VMEM_LIMIT,
            allow_input_fusion=[False] * (3 + 2 * ngb) + [True, True, True, False]),
        name="merge_out",
    )(x2d, yh, ya, *([proj] * (2 * ngb)), who, wao, wout, final_g.reshape(1, d))


def _dft_mats(b):
    idx = jnp.arange(b, dtype=jnp.int32)
    k = (idx[:, None] * idx[None, :]) % (2 * b)
    ang = k.astype(F32) * (math.pi / b)
    return jnp.cos(ang).astype(BF16), jnp.sin(ang).astype(BF16)


def _rope_tabs(L):
    inv = ROPE_THETA ** (-jnp.arange(0, ROPE_DIM, 2, dtype=F32) / ROPE_DIM)
    ang = jnp.arange(L, dtype=F32)[:, None] * inv[None, :]
    cos, sin = jnp.cos(ang), jnp.sin(ang)
    ones = jnp.ones((L, HEAD_DIM - ROPE_DIM), F32)
    zeros = jnp.zeros((L, HEAD_DIM - ROPE_HALF), F32)
    cos_t = jnp.concatenate([cos, cos, ones], axis=1)
    sin_lo = jnp.concatenate([-sin, zeros], axis=1)
    sin_hi = jnp.concatenate([jnp.zeros((L, ROPE_HALF), F32), sin, zeros[:, ROPE_HALF:]], axis=1)
    return cos_t, sin_lo, sin_hi


def _filter_feats(L):
    t = jnp.linspace(0.0, 1.0, L, dtype=F32)[:, None]
    bands = jnp.linspace(1e-4, FILTER_BANDS - 1, FILTER_BANDS, dtype=F32)[None, :]
    ang = (2.0 * math.pi / L) * jnp.arange(L, dtype=F32)[:, None] * bands
    feats = jnp.concatenate([t, jnp.cos(ang), -jnp.sin(ang)], axis=-1)
    feats = jnp.pad(feats, ((0, 0), (0, FEAT_PAD - feats.shape[1])))
    return feats, t


def kernel(x, norm_g, w_in, conv_w, conv_b, filt_w1, filt_b1, filt_w2, filt_b2, filt_w3, filt_b3,
           filt_w4, filt_freq, hyena_bias, attn_sink, w_hyena_out, w_attn_out, w_out, final_norm):
    B, L, D = x.shape
    depth = norm_g.shape[0]
    hw = hyena_bias.shape[2]
    aw = N_HEADS * HEAD_DIM
    kvw = N_KV_HEADS * HEAD_DIM
    sizes = (3 * hw, hw, aw, kvw, kvw, aw, D, D)
    cols = [0]
    for s in sizes:
        cols.append(cols[-1] + s)
    col_zhy, col_q, col_k, col_v, col_zat, col_gh, col_ga = cols[1:8]

    cmat, smat = _dft_mats(L // NB)
    nyq_cos = jnp.where(jnp.arange(L // NB) % 2 == 0, 1.0, -1.0).astype(BF16)
    smat_fwd = smat.at[0, :].set(nyq_cos)
    smat_inv = smat.at[:, 0].set(nyq_cos)
    rope_tabs = _rope_tabs(L)
    feats, tcol = _filter_feats(L)
    deltas = jnp.abs(jnp.linspace(MIN_DECAY, MAX_DECAY, hw, dtype=F32))[None, :]

    w_in_b, who_b, wao_b, wout_b = (w.astype(BF16) for w in (w_in, w_hyena_out, w_attn_out, w_out))
    row = lambda a: a[:, None, :]
    w1p = jnp.pad(filt_w1, ((0, 0), (0, FEAT_PAD - filt_w1.shape[1]), (0, 0)))
    xf = x.reshape(B * L, D)
    for l in range(depth):
        proj = _norm_proj(xf, row(norm_g), w_in_b, l)
        proj3 = proj.reshape(B, L, -1)
        kr, ki, kn = _filters(feats, tcol, w1p, row(filt_b1), filt_w2, row(filt_b2), filt_w3, row(filt_b3),
                              row(filt_freq), filt_w4, deltas, cmat, smat, l, NB)
        y_hy = _hyena(proj3, conv_w, row(conv_b), hyena_bias, cmat, smat_fwd, smat_inv, kr, ki, kn, l, NB)
        y_at = _attention(proj3, attn_sink, rope_tabs, col_q, col_k, col_v, col_zat, l)
        xf = _merge_out(xf, proj, y_hy.reshape(B * L, hw), y_at.reshape(B * L, aw),
                        who_b, wao_b, wout_b, final_norm, col_gh, col_ga, l, final=(l == depth - 1))
    return xf.reshape(B, L, D)
```

```python
import functools
import math

import numpy as np
import jax
import jax.numpy as jnp
from jax import lax
from jax.experimental import pallas as pl
from jax.experimental.pallas import tpu as pltpu

F32 = jnp.float32
BF16 = jnp.bfloat16

HEAD_DIM = 128
N_HEADS = 8
N_KV_HEADS = 2
GROUP = N_HEADS // N_KV_HEADS
WINDOW = 128
QSUB = 128
ROPE_THETA = 500000.0
ROPE_DIM = HEAD_DIM // 4
ROPE_HALF = ROPE_DIM // 2
EPS = 1e-6
FILTER_BANDS = 16
FEAT_PAD = 128
DECAY_TARGET = 1e-2
MIN_DECAY = math.log(DECAY_TARGET) / 0.3
MAX_DECAY = math.log(DECAY_TARGET) / 1.5

NB = 4
BF16_ROWS = 16
VMEM_LIMIT = 56 * 1024 * 1024


def _params(sem, vmem=VMEM_LIMIT):
    return pltpu.CompilerParams(dimension_semantics=sem, vmem_limit_bytes=vmem)


def _const_spec(shape):
    return pl.BlockSpec(shape, lambda *_: (0,) * len(shape), pipeline_mode=pl.Buffered(1))


def _split_bf16(x):
    hi = x.astype(BF16)
    return hi, (x - hi.astype(F32)).astype(BF16)


def _alt_sign(rows):
    return jnp.where((lax.broadcasted_iota(jnp.int32, (rows, 1), 0) & 1) == 0, 1.0, -1.0)


def _norm_proj_kernel(x_ref, g_ref, w_ref, o_ref, h_ref):
    @pl.when(pl.program_id(1) == 0)
    def _():
        x = x_ref[...]
        ms = jnp.mean(x * x, axis=-1, keepdims=True)
        h_ref[...] = (x * lax.rsqrt(ms + EPS) * g_ref[...]).astype(BF16)

    o_ref[...] = jnp.dot(h_ref[...], w_ref[...], preferred_element_type=F32)


def _norm_proj(x2d, g_all, w_all, l, tm=1024, tn=1792):
    m, d = x2d.shape
    n = w_all.shape[2]
    return pl.pallas_call(
        _norm_proj_kernel,
        out_shape=jax.ShapeDtypeStruct((m, n), F32),
        grid=(m // tm, n // tn),
        in_specs=[
            pl.BlockSpec((tm, d), lambda i, j: (i, 0)),
            pl.BlockSpec((None, 1, d), lambda i, j: (l, 0, 0)),
            pl.BlockSpec((None, d, tn), lambda i, j: (l, 0, j)),
        ],
        out_specs=pl.BlockSpec((tm, tn), lambda i, j: (i, j)),
        scratch_shapes=[pltpu.VMEM((tm, d), BF16)],
        compiler_params=_params(("parallel", "arbitrary")),
        name="norm_proj",
    )(x2d, g_all, w_all)


def _filter_kernel(feats_ref, t_ref, w1_ref, b1_ref, w2_ref, b2_ref, w3_ref, b3_ref, fr_ref,
                   w4f_ref, w4b_ref, dl_ref, c_ref, s_ref,
                   kr_ref, ki_ref, kn_ref,
                   hhi_ref, hlo_ref, xf_ref, xb_ref, cf_ref, sf_ref, cb_ref, sb_ref, *, nb, row_chunk):
    L = feats_ref.shape[0]
    b = L // nb
    tc = dl_ref.shape[1]
    hp = lax.Precision.HIGHEST
    nchunks = b // row_chunk
    alt = _alt_sign(row_chunk)

    @pl.when((pl.program_id(0) == 0) & (pl.program_id(1) == 0))
    def _():
        half = L // 2
        hid = w2_ref.shape[0]
        twice = lambda a: jnp.concatenate([a, a], axis=1)

        def blockdiag(w):
            z = jnp.zeros_like(w)
            return jnp.concatenate([jnp.concatenate([w, z], axis=1), jnp.concatenate([z, w], axis=1)], axis=0)

        fr = twice(fr_ref[...])
        h = jnp.concatenate([feats_ref[0:half, :], feats_ref[half:, :]], axis=1)
        for w_ref, b_ref in ((w1_ref, b1_ref), (w2_ref, b2_ref), (w3_ref, b3_ref)):
            h = jnp.sin(fr * (jnp.dot(h, blockdiag(w_ref[...]), precision=hp,
                                      preferred_element_type=F32) + twice(b_ref[...])))
        hi, lo = _split_bf16(h)
        hhi_ref[0:half, :], hhi_ref[half:, :] = hi[:, :hid], hi[:, hid:]
        hlo_ref[0:half, :], hlo_ref[half:, :] = lo[:, :hid], lo[:, hid:]

    def dot_split(hh, hl, w_ref):
        wh, wl = _split_bf16(w_ref[...])
        return (jnp.dot(hh, wh, preferred_element_type=F32) + jnp.dot(hh, wl, preferred_element_type=F32)
                + jnp.dot(hl, wh, preferred_element_type=F32))

    def taps(rows):
        hh, hl = hhi_ref[rows, :], hlo_ref[rows, :]
        decay = jnp.exp(-t_ref[rows, :] * dl_ref[...])
        return dot_split(hh, hl, w4f_ref) * decay, dot_split(hh, hl, w4b_ref) * decay

    zero_row = jnp.zeros((1, tc), F32)
    f0, b0, af, ab = [], [], [], []
    for q in range(nb):
        head_f, head_b = taps(pl.ds(q * b, BF16_ROWS))
        f0.append(head_f[0:1, :])
        b0.append(head_b[0:1, :])

        def tap_chunk(i, carry, q=q):
            l0 = pl.multiple_of(i * row_chunk, row_chunk)
            fwd, bwd = taps(pl.ds(pl.multiple_of(q * b + l0, row_chunk), row_chunk))
            first = (lax.broadcasted_iota(jnp.int32, (row_chunk, 1), 0) + l0) == 0
            fz = jnp.where(first, 0.0, fwd)
            bz = jnp.where(first, 0.0, bwd)
            xf_ref[q, pl.ds(l0, row_chunk), :] = fz.astype(BF16)
            xb_ref[q, pl.ds(l0, row_chunk), :] = bz.astype(BF16)
            return (carry[0] + jnp.sum(fz * alt, axis=0, keepdims=True),
                    carry[1] + jnp.sum(bz * alt, axis=0, keepdims=True))

        a_f, a_b = lax.fori_loop(0, nchunks, tap_chunk, (zero_row, zero_row))
        af.append(a_f)
        ab.append(a_b)

        def spec_chunk(i, carry, q=q):
            r = pl.ds(pl.multiple_of(i * row_chunk, row_chunk), row_chunk)
            cf_ref[q, r, :] = jnp.dot(c_ref[r, :], xf_ref[q], preferred_element_type=F32)
            sf_ref[q, r, :] = jnp.dot(s_ref[r, :], xf_ref[q], preferred_element_type=F32)
            cb_ref[q, r, :] = jnp.dot(c_ref[r, :], xb_ref[q], preferred_element_type=F32)
            sb_ref[q, r, :] = jnp.dot(s_ref[r, :], xb_ref[q], preferred_element_type=F32)
            return carry

        lax.fori_loop(0, nchunks, spec_chunk, 0)

    inv_n = 1.0 / (2 * b)
    for d in range(-(nb - 1), nb):
        slot = d + nb - 1
        e = -d
        if d >= 1:
            kn = af[d] + f0[d] + af[d - 1]
        elif d == 0:
            kn = af[0] + f0[0] + ab[0]
        else:
            kn = b0[e] + ab[e] + ab[e - 1]
        kn_ref[slot] = kn * inv_n

        def combine(i, carry, d=d, e=e, slot=slot):
            r0 = pl.multiple_of(i * row_chunk, row_chunk)
            r = pl.ds(r0, row_chunk)
            first = (lax.broadcasted_iota(jnp.int32, (row_chunk, 1), 0) + r0) == 0
            wgt = jnp.where(first, inv_n, 2.0 * inv_n)
            if d >= 1:
                kr = cf_ref[d, r, :] + f0[d] + alt * cf_ref[d - 1, r, :]
                ki = -sf_ref[d, r, :] - alt * sf_ref[d - 1, r, :]
            elif d == 0:
                kr = cf_ref[0, r, :] + f0[0] + cb_ref[0, r, :]
                ki = sb_ref[0, r, :] - sf_ref[0, r, :]
            else:
                kr = b0[e] + cb_ref[e, r, :] + alt * cb_ref[e - 1, r, :]
                ki = sb_ref[e, r, :] + alt * sb_ref[e - 1, r, :]
            kr_ref[slot, r, :] = (kr * wgt).astype(kr_ref.dtype)
            ki_ref[slot, r, :] = (ki * wgt).astype(ki_ref.dtype)
            return carry

        lax.fori_loop(0, nchunks, combine, 0)


def _filters(feats, tcol, w1p, b1, w2, b2, w3, b3, freq, w4, deltas, cmat, smat, l, nb, tc=256,
             row_chunk=512):
    L = feats.shape[0]
    b = L // nb
    nd = 2 * nb - 1
    width = deltas.shape[1]
    nct = width // tc
    hid = w2.shape[1]
    small = lambda a: pl.BlockSpec(a.shape, lambda o, c: (0,) * a.ndim)
    layer = lambda a: pl.BlockSpec((None,) + a.shape[1:], lambda o, c: (l,) + (0,) * (a.ndim - 1))
    kern = functools.partial(_filter_kernel, nb=nb, row_chunk=row_chunk)
    kspec = pl.BlockSpec((None, nd, b, tc), lambda o, c: (o, 0, 0, c))
    blk = lambda dt: pltpu.VMEM((nb, b, tc), dt)
    return pl.pallas_call(
        kern,
        out_shape=(jax.ShapeDtypeStruct((2, nd, b, width), BF16),
                   jax.ShapeDtypeStruct((2, nd, b, width), BF16),
                   jax.ShapeDtypeStruct((2, nd, 1, width), F32)),
        grid=(2, nct),
        in_specs=[small(feats), small(tcol), layer(w1p), layer(b1), layer(w2), layer(b2),
                  layer(w3), layer(b3), layer(freq),
                  pl.BlockSpec((None, hid, tc), lambda o, c: (l, 0, o * 2 * nct + c)),
                  pl.BlockSpec((None, hid, tc), lambda o, c: (l, 0, o * 2 * nct + nct + c)),
                  pl.BlockSpec((1, tc), lambda o, c: (0, c)),
                  _const_spec((b, b)), _const_spec((b, b))],
        out_specs=(kspec, kspec, pl.BlockSpec((None, nd, 1, tc), lambda o, c: (o, 0, 0, c))),
        scratch_shapes=[pltpu.VMEM((L, hid), BF16), pltpu.VMEM((L, hid), BF16), blk(BF16), blk(BF16),
                        blk(F32), blk(F32), blk(F32), blk(F32)],
        compiler_params=_params(("arbitrary", "arbitrary")),
        name="hyena_filters",
    )(feats, tcol, w1p, b1, w2, b2, w3, b3, freq, w4, w4, deltas, cmat, smat)


def _sconv_chunk(u_ref, w_ref, b_ref, i, rows, nchunks):
    r0 = i * rows
    tc = u_ref.shape[1]
    u = u_ref[r0:r0 + rows, :]
    zero = jnp.zeros((1, tc), F32)
    up = u_ref[r0 - 8:r0, :][7:8, :] if i > 0 else zero
    dn = u_ref[r0 + rows:r0 + rows + 8, :][0:1, :] if i < nchunks - 1 else zero
    row = lax.broadcasted_iota(jnp.int32, (rows, 1), 0)
    prev = jnp.where(row == 0, up, pltpu.roll(u, 1, 0))
    nxt = jnp.where(row == rows - 1, dn, pltpu.roll(u, rows - 1, 0))
    return b_ref[...] + prev * w_ref[0:1, :] + u * w_ref[1:2, :] + nxt * w_ref[2:3, :]


def _hyena_kernel(v_ref, x1_ref, x2_ref, zg_ref, wv_ref, wx1_ref, wx2_ref, bv_ref, bx1_ref, bx2_ref,
                  hb_ref, c_ref, sf_ref, si_ref, kr_ref, ki_ref, kn_ref, o_ref,
                  u_ref, ub_ref, a_ref, bn_ref, *, nb, seq_chunk, freq_chunk):
    L, tc = u_ref.shape
    b = L // nb
    per_block = b // seq_chunk
    nseq = L // seq_chunk

    def lanes(j):
        return slice(j * tc, (j + 1) * tc)

    def put_signal(chunk, val):
        j, local = divmod(chunk, per_block)
        rows = slice(chunk * seq_chunk, (chunk + 1) * seq_chunk)
        u_ref[rows, :] = val
        ub_ref[local * seq_chunk:(local + 1) * seq_chunk, lanes(j)] = val.astype(BF16)

    def forward(order):
        for fc in range(b // freq_chunk):
            r = slice(fc * freq_chunk, (fc + 1) * freq_chunk)
            ur = jnp.dot(c_ref[r, :], ub_ref[...], preferred_element_type=F32).astype(BF16)
            us = jnp.dot(sf_ref[r, :], ub_ref[...], preferred_element_type=F32).astype(BF16)
            for i in range(nb):
                acc_a = acc_b = None
                for j in range(nb):
                    kr = kr_ref[order, i - j + nb - 1, r, :]
                    ki = ki_ref[order, i - j + nb - 1, r, :]
                    urj, usj = ur[:, lanes(j)], us[:, lanes(j)]
                    ta = urj * kr + usj * ki
                    tb = usj * kr - urj * ki
                    acc_a = ta if acc_a is None else acc_a + ta
                    acc_b = tb if acc_b is None else acc_b + tb
                if fc == 0:
                    nyq = None
                    for j in range(nb):
                        t = us[0:1, lanes(j)].astype(F32) * kn_ref[order, i - j + nb - 1]
                        nyq = t if nyq is None else nyq + t
                    first = lax.broadcasted_iota(jnp.int32, (BF16_ROWS, 1), 0) == 0
                    top = jnp.where(first, nyq.astype(BF16), acc_b[:BF16_ROWS])
                    acc_b = jnp.concatenate([top, acc_b[BF16_ROWS:]], axis=0)
                a_ref[r, lanes(i)] = acc_a
                bn_ref[r, lanes(i)] = acc_b

    def inverse(order, finish):
        for tcn in range(per_block):
            r = slice(tcn * seq_chunk, (tcn + 1) * seq_chunk)
            y2 = jnp.dot(c_ref[r, :], a_ref[...], preferred_element_type=F32)
            y2 = y2 + jnp.dot(si_ref[r, :], bn_ref[...], preferred_element_type=F32)
            for i in range(nb):
                chunk = i * per_block + tcn
                rows = slice(chunk * seq_chunk, (chunk + 1) * seq_chunk)
                finish(chunk, rows, y2[:, lanes(i)] + u_ref[rows, :] * hb_ref[order:order + 1, :])

    for chunk in range(nseq):
        put_signal(chunk, _sconv_chunk(v_ref, wv_ref, bv_ref, chunk, seq_chunk, nseq))

    forward(0)

    def finish_z(chunk, rows, y):
        put_signal(chunk, _sconv_chunk(x1_ref, wx1_ref, bx1_ref, chunk, seq_chunk, nseq) * y)

    inverse(0, finish_z)
    forward(1)

    def finish_out(chunk, rows, y):
        y = _sconv_chunk(x2_ref, wx2_ref, bx2_ref, chunk, seq_chunk, nseq) * y
        zg = zg_ref[rows, :]
        o_ref[rows, :] = (y * (zg * jax.nn.sigmoid(zg))).astype(o_ref.dtype)

    inverse(1, finish_out)


def _hyena(proj3, conv_w, conv_b, hyena_bias, cmat, smat_fwd, smat_inv, kr, ki, kn, l, nb, tc=256,
           seq_chunk=512, freq_chunk=512):
    B, L, _ = proj3.shape
    b = L // nb
    nd = 2 * nb - 1
    width = hyena_bias.shape[2]
    nct = width // tc
    sig = lambda part: pl.BlockSpec((None, L, tc), lambda j, bb: (bb, 0, part * nct + j))
    cw = lambda part: pl.BlockSpec((None, 3, tc), lambda j, bb: (l, 0, part * nct + j))
    cb = lambda part: pl.BlockSpec((None, 1, tc), lambda j, bb: (l, 0, part * nct + j))
    kspec = pl.BlockSpec((2, nd, b, tc), lambda j, bb: (0, 0, 0, j))
    kern = functools.partial(_hyena_kernel, nb=nb, seq_chunk=seq_chunk, freq_chunk=freq_chunk)
    return pl.pallas_call(
        kern,
        out_shape=jax.ShapeDtypeStruct((B, L, width), BF16),
        grid=(nct, B),
        in_specs=[sig(0), sig(1), sig(2), sig(3),
                  cw(0), cw(1), cw(2), cb(0), cb(1), cb(2),
                  pl.BlockSpec((None, 2, tc), lambda j, bb: (l, 0, j)),
                  _const_spec((b, b)), _const_spec((b, b)), _const_spec((b, b)),
                  kspec, kspec,
                  pl.BlockSpec((2, nd, 1, tc), lambda j, bb: (0, 0, 0, j))],
        out_specs=pl.BlockSpec((None, L, tc), lambda j, bb: (bb, 0, j)),
        scratch_shapes=[pltpu.VMEM((L, tc), F32), pltpu.VMEM((b, nb * tc), BF16),
                        pltpu.VMEM((b, nb * tc), BF16), pltpu.VMEM((b, nb * tc), BF16)],
        compiler_params=_params(("parallel", "arbitrary")),
        name="hyena_mixer",
    )(proj3, proj3, proj3, proj3, conv_w, conv_w, conv_w, conv_b, conv_b, conv_b,
      hyena_bias, cmat, smat_fwd, smat_inv, kr, ki, kn)


def _rope(x, cos, sin_lo, sin_hi):
    return (x * cos + pltpu.roll(x, HEAD_DIM - ROPE_HALF, 1) * sin_lo
            + pltpu.roll(x, ROPE_HALF, 1) * sin_hi)


def _attn_kernel(sink_ref, q_ref, k_ref, v_ref, zg_ref, cos_ref, slo_ref, shi_ref, o_ref,
                 kb_ref, vt_ref, *, layer):
    L = k_ref.shape[0]
    qb = q_ref.shape[0]
    kw = QSUB + 2 * WINDOW
    gq = GROUP * QSUB
    kvh = pl.program_id(1)
    n = pl.program_id(2)

    @pl.when(n == 0)
    def _():
        kb_ref[...] = _rope(k_ref[...], cos_ref[...], slo_ref[...], shi_ref[...]).astype(BF16)
        vt_ref[...] = v_ref[...].T.astype(BF16)

    log2e = math.log2(math.e)
    lane_head = lax.broadcasted_iota(jnp.int32, (1, gq), 1) // QSUB
    sk2 = jnp.zeros((1, gq), F32)
    for g in range(GROUP):
        sk2 = jnp.where(lane_head == g, sink_ref[layer, kvh * GROUP + g] * log2e, sk2)

    for sb in range(qb // QSUB):
        rows = slice(sb * QSUB, (sb + 1) * QSUB)
        q0 = pl.multiple_of(n * qb + sb * QSUB, QSUB)
        start = pl.multiple_of(jnp.clip(q0 - WINDOW, 0, L - kw), WINDOW)
        win = pl.ds(start, kw)
        tab = pl.ds(q0, QSUB)
        cq, slq, shq = cos_ref[tab, :], slo_ref[tab, :], shi_ref[tab, :]
        q = jnp.concatenate(
            [_rope(q_ref[rows, g * HEAD_DIM:(g + 1) * HEAD_DIM], cq, slq, shq).astype(BF16)
             for g in range(GROUP)], axis=0)
        raw = lax.dot_general(kb_ref[win, :], q, (((1,), (1,)), ((), ())),
                              preferred_element_type=F32)
        kpos = start + lax.broadcasted_iota(jnp.int32, (kw, 1), 0)
        qpos = q0 + lax.broadcasted_iota(jnp.int32, (1, QSUB), 1)
        cap = jnp.where(jnp.abs(kpos - qpos) <= WINDOW, jnp.inf, -jnp.inf)
        raw = jnp.minimum(raw, jnp.concatenate([cap] * GROUP, axis=1))
        c = (HEAD_DIM ** -0.5) * log2e
        m2 = jnp.maximum(jnp.max(raw, axis=0, keepdims=True) * c, sk2)
        p = jnp.exp2(raw * c - m2)
        denom = jnp.sum(p, axis=0, keepdims=True) + jnp.exp2(sk2 - m2)
        ot = jnp.dot(vt_ref[:, win], p.astype(BF16), preferred_element_type=F32) / denom
        for g in range(GROUP):
            cols = slice(g * HEAD_DIM, (g + 1) * HEAD_DIM)
            o = ot[:, g * QSUB:(g + 1) * QSUB].T
            zg = zg_ref[rows, cols]
            o_ref[rows, cols] = (o * (zg * jax.nn.sigmoid(zg))).astype(o_ref.dtype)


def _attention(proj3, sink, rope_tabs, col_q, col_k, col_v, col_zg, l, qb=2048):
    B, L, _ = proj3.shape
    gw = GROUP * HEAD_DIM
    cos_t, sin_lo, sin_hi = rope_tabs
    tab = pl.BlockSpec((L, HEAD_DIM), lambda b, h, n: (0, 0))
    return pl.pallas_call(
        functools.partial(_attn_kernel, layer=l),
        out_shape=jax.ShapeDtypeStruct((B, L, N_HEADS * HEAD_DIM), BF16),
        grid=(B, N_KV_HEADS, L // qb),
        in_specs=[pl.BlockSpec(memory_space=pltpu.SMEM),
                  pl.BlockSpec((None, qb, gw), lambda b, h, n: (b, n, col_q // gw + h)),
                  pl.BlockSpec((None, L, HEAD_DIM), lambda b, h, n: (b, 0, col_k // HEAD_DIM + h)),
                  pl.BlockSpec((None, L, HEAD_DIM), lambda b, h, n: (b, 0, col_v // HEAD_DIM + h)),
                  pl.BlockSpec((None, qb, gw), lambda b, h, n: (b, n, col_zg // gw + h)),
                  tab, tab, tab],
        out_specs=pl.BlockSpec((None, qb, gw), lambda b, h, n: (b, n, h)),
        scratch_shapes=[pltpu.VMEM((L, HEAD_DIM), BF16), pltpu.VMEM((HEAD_DIM, L), BF16)],
        compiler_params=_params(("parallel", "parallel", "arbitrary")),
        name="window_attention",
    )(sink, proj3, proj3, proj3, proj3, cos_t, sin_lo, sin_hi)


def _merge_out_kernel(*refs, n_gate_blocks, final):
    x_ref, yh_ref, ya_ref = refs[0:3]
    gh_refs = refs[3:3 + n_gate_blocks]
    ga_refs = refs[3 + n_gate_blocks:3 + 2 * n_gate_blocks]
    who_ref, wao_ref, wout_ref, fg_ref, o_ref, m_ref = refs[3 + 2 * n_gate_blocks:]
    gw = gh_refs[0].shape[1]
    yh = yh_ref[...]
    ya = ya_ref[...]
    for c in range(n_gate_blocks):
        cols = slice(c * gw, (c + 1) * gw)
        ph = jnp.dot(yh, who_ref[:, cols], preferred_element_type=F32)
        pa = jnp.dot(ya, wao_ref[:, cols], preferred_element_type=F32)
        merged = jax.nn.sigmoid(gh_refs[c][...]) * ph + jax.nn.sigmoid(ga_refs[c][...]) * pa
        m_ref[:, cols] = merged.astype(BF16)
    out = x_ref[...] + jnp.dot(m_ref[...], wout_ref[...], preferred_element_type=F32)
    if final:
        ms = jnp.mean(out * out, axis=-1, keepdims=True)
        out = out * lax.rsqrt(ms + EPS) * fg_ref[...]
    o_ref[...] = out


def _merge_out(x2d, proj, yh, ya, who, wao, wout, final_g, col_gh, col_ga, l, final, tm=256, gw=512):
    m, d = x2d.shape
    width = yh.shape[1]
    ngb = d // gw
    gspec = lambda col0, c: pl.BlockSpec((tm, gw), lambda i: (i, col0 // gw + c))
    wspec = lambda rows: pl.BlockSpec((None, rows, d), lambda i: (l, 0, 0), pipeline_mode=pl.Buffered(1))
    kern = functools.partial(_merge_out_kernel, n_gate_blocks=ngb, final=final)
    return pl.pallas_call(
        kern,
        out_shape=jax.ShapeDtypeStruct((m, d), F32),
        grid=(m // tm,),
        in_specs=[pl.BlockSpec((tm, d), lambda i: (i, 0)),
                  pl.BlockSpec((tm, width), lambda i: (i, 0)),
                  pl.BlockSpec((tm, width), lambda i: (i, 0))]
                 + [gspec(col_gh, c) for c in range(ngb)]
                 + [gspec(col_ga, c) for c in range(ngb)]
                 + [wspec(width), wspec(width), wspec(d),
                    pl.BlockSpec((1, d), lambda i: (0, 0))],
        out_specs=pl.BlockSpec((tm, d), lambda i: (i, 0)),
        scratch_shapes=[pltpu.VMEM((tm, d), BF16)],
        compiler_params=_params(("parallel",)),
        name="merge_out",
    )(x2d, yh, ya, *([proj] * (2 * ngb)), who, wao, wout, final_g.reshape(1, d))


def _dft_mats(b):
    idx = np.arange(b)
    ang = ((idx[:, None] * idx[None, :]) % (2 * b)) * (np.pi / b)
    cmat, smat = np.cos(ang), np.sin(ang)
    nyq_cos = np.where(idx % 2 == 0, 1.0, -1.0)
    smat_fwd, smat_inv = smat.copy(), smat.copy()
    smat_fwd[0, :] = nyq_cos
    smat_inv[:, 0] = nyq_cos
    return tuple(jnp.asarray(m).astype(BF16) for m in (cmat, smat, smat_fwd, smat_inv))


def _rope_tabs(L):
    inv = ROPE_THETA ** (-np.arange(0, ROPE_DIM, 2) / ROPE_DIM)
    ang = np.arange(L)[:, None] * inv[None, :]
    cos, sin = np.cos(ang), np.sin(ang)
    ones = np.ones((L, HEAD_DIM - ROPE_DIM))
    zeros = np.zeros((L, HEAD_DIM - ROPE_HALF))
    cos_t = np.concatenate([cos, cos, ones], axis=1)
    sin_lo = np.concatenate([-sin, zeros], axis=1)
    sin_hi = np.concatenate([np.zeros((L, ROPE_HALF)), sin, zeros[:, ROPE_HALF:]], axis=1)
    return tuple(jnp.asarray(t).astype(F32) for t in (cos_t, sin_lo, sin_hi))


def _filter_feats(L):
    t = np.linspace(0.0, 1.0, L)[:, None]
    bands = np.linspace(1e-4, FILTER_BANDS - 1, FILTER_BANDS)[None, :]
    ang = (2.0 * math.pi / L) * np.arange(L)[:, None] * bands
    feats = np.concatenate([t, np.cos(ang), -np.sin(ang)], axis=-1)
    feats = np.pad(feats, ((0, 0), (0, FEAT_PAD - feats.shape[1])))
    return jnp.asarray(feats).astype(F32), jnp.asarray(t).astype(F32)


def kernel(x, norm_g, w_in, conv_w, conv_b, filt_w1, filt_b1, filt_w2, filt_b2, filt_w3, filt_b3,
           filt_w4, filt_freq, hyena_bias, attn_sink, w_hyena_out, w_attn_out, w_out, final_norm):
    B, L, D = x.shape
    depth = norm_g.shape[0]
    hw = hyena_bias.shape[2]
    aw = N_HEADS * HEAD_DIM
    kvw = N_KV_HEADS * HEAD_DIM
    sizes = (3 * hw, hw, aw, kvw, kvw, aw, D, D)
    cols = [0]
    for s in sizes:
        cols.append(cols[-1] + s)
    col_zhy, col_q, col_k, col_v, col_zat, col_gh, col_ga = cols[1:8]

    cmat, smat, smat_fwd, smat_inv = _dft_mats(L // NB)
    rope_tabs = _rope_tabs(L)
    feats, tcol = _filter_feats(L)
    deltas = jnp.abs(jnp.linspace(MIN_DECAY, MAX_DECAY, hw, dtype=F32))[None, :]

    w_in_b, who_b, wao_b, wout_b = (w.astype(BF16) for w in (w_in, w_hyena_out, w_attn_out, w_out))
    row = lambda a: a[:, None, :]
    w1p = jnp.pad(filt_w1, ((0, 0), (0, FEAT_PAD - filt_w1.shape[1]), (0, 0)))
    xf = x.reshape(B * L, D)
    for l in range(depth):
        proj = _norm_proj(xf, row(norm_g), w_in_b, l)
        proj3 = proj.reshape(B, L, -1)
        kr, ki, kn = _filters(feats, tcol, w1p, row(filt_b1), filt_w2, row(filt_b2), filt_w3, row(filt_b3),
                              row(filt_freq), filt_w4, deltas, cmat, smat, l, NB)
        y_hy = _hyena(proj3, conv_w, row(conv_b), hyena_bias, cmat, smat_fwd, smat_inv, kr, ki, kn, l, NB)
        y_at = _attention(proj3, attn_sink, rope_tabs, col_q, col_k, col_v, col_zat, l)
        xf = _merge_out(xf, proj, y_hy.reshape(B * L, hw), y_at.reshape(B * L, aw),
                        who_b, wao_b, wout_b, final_norm, col_gh, col_ga, l, final=(l == depth - 1))
    return xf.reshape(B, L, D)
```

```python
import functools
import math

import numpy as np
import jax
import jax.numpy as jnp
from jax import lax
from jax.experimental import pallas as pl
from jax.experimental.pallas import tpu as pltpu

F32 = jnp.float32
BF16 = jnp.bfloat16

HEAD_DIM = 128
N_HEADS = 8
N_KV_HEADS = 2
GROUP = N_HEADS // N_KV_HEADS
WINDOW = 128
QSUB = 128
ROPE_THETA = 500000.0
ROPE_DIM = HEAD_DIM // 4
ROPE_HALF = ROPE_DIM // 2
EPS = 1e-6
FILTER_BANDS = 16
FEAT_PAD = 128
DECAY_TARGET = 1e-2
MIN_DECAY = math.log(DECAY_TARGET) / 0.3
MAX_DECAY = math.log(DECAY_TARGET) / 1.5

NB = 4
BF16_ROWS = 16
VMEM_LIMIT = 56 * 1024 * 1024


def _params(sem, vmem=VMEM_LIMIT):
    return pltpu.CompilerParams(dimension_semantics=sem, vmem_limit_bytes=vmem)


def _const_spec(shape):
    return pl.BlockSpec(shape, lambda *_: (0,) * len(shape), pipeline_mode=pl.Buffered(1))


def _split_bf16(x):
    hi = x.astype(BF16)
    return hi, (x - hi.astype(F32)).astype(BF16)


def _alt_sign(rows):
    return jnp.where((lax.broadcasted_iota(jnp.int32, (rows, 1), 0) & 1) == 0, 1.0, -1.0)


def _norm_proj_kernel(x_ref, g_ref, w_ref, o_ref, h_ref):
    @pl.when(pl.program_id(1) == 0)
    def _():
        x = x_ref[...]
        ms = jnp.mean(x * x, axis=-1, keepdims=True)
        h_ref[...] = (x * lax.rsqrt(ms + EPS) * g_ref[...]).astype(BF16)

    o_ref[...] = jnp.dot(h_ref[...], w_ref[...], preferred_element_type=F32)


def _norm_proj(x2d, g_all, w_all, l, tm=1024, tn=1792):
    m, d = x2d.shape
    n = w_all.shape[2]
    return pl.pallas_call(
        _norm_proj_kernel,
        out_shape=jax.ShapeDtypeStruct((m, n), F32),
        grid=(m // tm, n // tn),
        in_specs=[
            pl.BlockSpec((tm, d), lambda i, j: (i, 0)),
            pl.BlockSpec((None, 1, d), lambda i, j: (l, 0, 0)),
            pl.BlockSpec((None, d, tn), lambda i, j: (l, 0, j)),
        ],
        out_specs=pl.BlockSpec((tm, tn), lambda i, j: (i, j)),
        scratch_shapes=[pltpu.VMEM((tm, d), BF16)],
        compiler_params=_params(("parallel", "arbitrary")),
        name="norm_proj",
    )(x2d, g_all, w_all)


def _proj_kernel(h_ref, w_ref, o_ref):
    o_ref[...] = jnp.dot(h_ref[...], w_ref[...], preferred_element_type=F32)


def _proj(h2d, w_all, l, tm=1024, tn=1792):
    m, d = h2d.shape
    n = w_all.shape[2]
    return pl.pallas_call(
        _proj_kernel,
        out_shape=jax.ShapeDtypeStruct((m, n), F32),
        grid=(m // tm, n // tn),
        in_specs=[pl.BlockSpec((tm, d), lambda i, j: (i, 0)),
                  pl.BlockSpec((None, d, tn), lambda i, j: (l, 0, j))],
        out_specs=pl.BlockSpec((tm, tn), lambda i, j: (i, j)),
        compiler_params=_params(("parallel", "arbitrary")),
        name="proj",
    )(h2d, w_all)


def _filter_kernel(feats_ref, t_ref, w1_ref, b1_ref, w2_ref, b2_ref, w3_ref, b3_ref, fr_ref,
                   w4f_ref, w4b_ref, dl_ref, c_ref, s_ref,
                   kr_ref, ki_ref, kn_ref,
                   hhi_ref, hlo_ref, xf_ref, xb_ref, cf_ref, sf_ref, cb_ref, sb_ref, *, nb, row_chunk):
    L = feats_ref.shape[0]
    b = L // nb
    tc = dl_ref.shape[1]
    hp = lax.Precision.HIGHEST
    nchunks = b // row_chunk
    alt = _alt_sign(row_chunk)

    @pl.when((pl.program_id(0) == 0) & (pl.program_id(1) == 0))
    def _():
        half = L // 2
        hid = w2_ref.shape[0]
        twice = lambda a: jnp.concatenate([a, a], axis=1)

        def blockdiag(w):
            z = jnp.zeros_like(w)
            return jnp.concatenate([jnp.concatenate([w, z], axis=1), jnp.concatenate([z, w], axis=1)], axis=0)

        fr = twice(fr_ref[...])
        h = jnp.concatenate([feats_ref[0:half, :], feats_ref[half:, :]], axis=1)
        for w_ref, b_ref in ((w1_ref, b1_ref), (w2_ref, b2_ref), (w3_ref, b3_ref)):
            h = jnp.sin(fr * (jnp.dot(h, blockdiag(w_ref[...]), precision=hp,
                                      preferred_element_type=F32) + twice(b_ref[...])))
        hi, lo = _split_bf16(h)
        hhi_ref[0:half, :], hhi_ref[half:, :] = hi[:, :hid], hi[:, hid:]
        hlo_ref[0:half, :], hlo_ref[half:, :] = lo[:, :hid], lo[:, hid:]

    def dot_split(hh, hl, w_ref):
        wh, wl = _split_bf16(w_ref[...])
        return (jnp.dot(hh, wh, preferred_element_type=F32) + jnp.dot(hh, wl, preferred_element_type=F32)
                + jnp.dot(hl, wh, preferred_element_type=F32))

    def taps(rows):
        hh, hl = hhi_ref[rows, :], hlo_ref[rows, :]
        decay = jnp.exp(-t_ref[rows, :] * dl_ref[...])
        return dot_split(hh, hl, w4f_ref) * decay, dot_split(hh, hl, w4b_ref) * decay

    zero_row = jnp.zeros((1, tc), F32)
    f0, b0, af, ab = [], [], [], []
    for q in range(nb):
        head_f, head_b = taps(pl.ds(q * b, BF16_ROWS))
        f0.append(head_f[0:1, :])
        b0.append(head_b[0:1, :])

        def tap_chunk(i, carry, q=q):
            l0 = pl.multiple_of(i * row_chunk, row_chunk)
            fwd, bwd = taps(pl.ds(pl.multiple_of(q * b + l0, row_chunk), row_chunk))
            first = (lax.broadcasted_iota(jnp.int32, (row_chunk, 1), 0) + l0) == 0
            fz = jnp.where(first, 0.0, fwd)
            bz = jnp.where(first, 0.0, bwd)
            xf_ref[q, pl.ds(l0, row_chunk), :] = fz.astype(BF16)
            xb_ref[q, pl.ds(l0, row_chunk), :] = bz.astype(BF16)
            return (carry[0] + jnp.sum(fz * alt, axis=0, keepdims=True),
                    carry[1] + jnp.sum(bz * alt, axis=0, keepdims=True))

        a_f, a_b = lax.fori_loop(0, nchunks, tap_chunk, (zero_row, zero_row))
        af.append(a_f)
        ab.append(a_b)

        def spec_chunk(i, carry, q=q):
            r = pl.ds(pl.multiple_of(i * row_chunk, row_chunk), row_chunk)
            cf_ref[q, r, :] = jnp.dot(c_ref[r, :], xf_ref[q], preferred_element_type=F32)
            sf_ref[q, r, :] = jnp.dot(s_ref[r, :], xf_ref[q], preferred_element_type=F32)
            cb_ref[q, r, :] = jnp.dot(c_ref[r, :], xb_ref[q], preferred_element_type=F32)
            sb_ref[q, r, :] = jnp.dot(s_ref[r, :], xb_ref[q], preferred_element_type=F32)
            return carry

        lax.fori_loop(0, nchunks, spec_chunk, 0)

    inv_n = 1.0 / (2 * b)
    for d in range(-(nb - 1), nb):
        slot = d + nb - 1
        e = -d
        if d >= 1:
            kn = af[d] + f0[d] + af[d - 1]
        elif d == 0:
            kn = af[0] + f0[0] + ab[0]
        else:
            kn = b0[e] + ab[e] + ab[e - 1]
        kn_ref[slot] = kn * inv_n

        def combine(i, carry, d=d, e=e, slot=slot):
            r0 = pl.multiple_of(i * row_chunk, row_chunk)
            r = pl.ds(r0, row_chunk)
            first = (lax.broadcasted_iota(jnp.int32, (row_chunk, 1), 0) + r0) == 0
            wgt = jnp.where(first, inv_n, 2.0 * inv_n)
            if d >= 1:
                kr = cf_ref[d, r, :] + f0[d] + alt * cf_ref[d - 1, r, :]
                ki = -sf_ref[d, r, :] - alt * sf_ref[d - 1, r, :]
            elif d == 0:
                kr = cf_ref[0, r, :] + f0[0] + cb_ref[0, r, :]
                ki = sb_ref[0, r, :] - sf_ref[0, r, :]
            else:
                kr = b0[e] + cb_ref[e, r, :] + alt * cb_ref[e - 1, r, :]
                ki = sb_ref[e, r, :] + alt * sb_ref[e - 1, r, :]
            kr_ref[slot, r, :] = (kr * wgt).astype(kr_ref.dtype)
            ki_ref[slot, r, :] = (ki * wgt).astype(ki_ref.dtype)
            return carry

        lax.fori_loop(0, nchunks, combine, 0)


def _filters(feats, tcol, w1p, b1, w2, b2, w3, b3, freq, w4, deltas, cmat, smat, l, nb, tc=256,
             row_chunk=512):
    L = feats.shape[0]
    b = L // nb
    nd = 2 * nb - 1
    width = deltas.shape[1]
    nct = width // tc
    hid = w2.shape[1]
    small = lambda a: pl.BlockSpec(a.shape, lambda o, c: (0,) * a.ndim)
    layer = lambda a: pl.BlockSpec((None,) + a.shape[1:], lambda o, c: (l,) + (0,) * (a.ndim - 1))
    kern = functools.partial(_filter_kernel, nb=nb, row_chunk=row_chunk)
    kspec = pl.BlockSpec((None, nd, b, tc), lambda o, c: (o, 0, 0, c))
    blk = lambda dt: pltpu.VMEM((nb, b, tc), dt)
    return pl.pallas_call(
        kern,
        out_shape=(jax.ShapeDtypeStruct((2, nd, b, width), BF16),
                   jax.ShapeDtypeStruct((2, nd, b, width), BF16),
                   jax.ShapeDtypeStruct((2, nd, 1, width), F32)),
        grid=(2, nct),
        in_specs=[small(feats), small(tcol), layer(w1p), layer(b1), layer(w2), layer(b2),
                  layer(w3), layer(b3), layer(freq),
                  pl.BlockSpec((None, hid, tc), lambda o, c: (l, 0, o * 2 * nct + c)),
                  pl.BlockSpec((None, hid, tc), lambda o, c: (l, 0, o * 2 * nct + nct + c)),
                  pl.BlockSpec((1, tc), lambda o, c: (0, c)),
                  _const_spec((b, b)), _const_spec((b, b))],
        out_specs=(kspec, kspec, pl.BlockSpec((None, nd, 1, tc), lambda o, c: (o, 0, 0, c))),
        scratch_shapes=[pltpu.VMEM((L, hid), BF16), pltpu.VMEM((L, hid), BF16), blk(BF16), blk(BF16),
                        blk(F32), blk(F32), blk(F32), blk(F32)],
        compiler_params=_params(("arbitrary", "arbitrary")),
        name="hyena_filters",
    )(feats, tcol, w1p, b1, w2, b2, w3, b3, freq, w4, w4, deltas, cmat, smat)


def _sconv_chunk(u_ref, w_ref, b_ref, i, rows, nchunks):
    r0 = i * rows
    tc = u_ref.shape[1]
    u = u_ref[r0:r0 + rows, :]
    zero = jnp.zeros((1, tc), F32)
    up = u_ref[r0 - 8:r0, :][7:8, :] if i > 0 else zero
    dn = u_ref[r0 + rows:r0 + rows + 8, :][0:1, :] if i < nchunks - 1 else zero
    row = lax.broadcasted_iota(jnp.int32, (rows, 1), 0)
    prev = jnp.where(row == 0, up, pltpu.roll(u, 1, 0))
    nxt = jnp.where(row == rows - 1, dn, pltpu.roll(u, rows - 1, 0))
    return b_ref[...] + prev * w_ref[0:1, :] + u * w_ref[1:2, :] + nxt * w_ref[2:3, :]


def _hyena_kernel(v_ref, x1_ref, x2_ref, zg_ref, wv_ref, wx1_ref, wx2_ref, bv_ref, bx1_ref, bx2_ref,
                  hb_ref, c_ref, sf_ref, si_ref, kr_ref, ki_ref, kn_ref, o_ref,
                  u_ref, ub_ref, a_ref, bn_ref, *, nb, seq_chunk, freq_chunk):
    L, tc = u_ref.shape
    b = L // nb
    per_block = b // seq_chunk
    nseq = L // seq_chunk

    def lanes(j):
        return slice(j * tc, (j + 1) * tc)

    def put_signal(chunk, val):
        j, local = divmod(chunk, per_block)
        rows = slice(chunk * seq_chunk, (chunk + 1) * seq_chunk)
        u_ref[rows, :] = val
        ub_ref[local * seq_chunk:(local + 1) * seq_chunk, lanes(j)] = val.astype(BF16)

    def forward(order):
        for fc in range(b // freq_chunk):
            r = slice(fc * freq_chunk, (fc + 1) * freq_chunk)
            ur = jnp.dot(c_ref[r, :], ub_ref[...], preferred_element_type=F32).astype(BF16)
            us = jnp.dot(sf_ref[r, :], ub_ref[...], preferred_element_type=F32).astype(BF16)
            for i in range(nb):
                acc_a = acc_b = None
                for j in range(nb):
                    kr = kr_ref[order, i - j + nb - 1, r, :]
                    ki = ki_ref[order, i - j + nb - 1, r, :]
                    urj, usj = ur[:, lanes(j)], us[:, lanes(j)]
                    ta = urj * kr + usj * ki
                    tb = usj * kr - urj * ki
                    acc_a = ta if acc_a is None else acc_a + ta
                    acc_b = tb if acc_b is None else acc_b + tb
                if fc == 0:
                    nyq = None
                    for j in range(nb):
                        t = us[0:1, lanes(j)].astype(F32) * kn_ref[order, i - j + nb - 1]
                        nyq = t if nyq is None else nyq + t
                    first = lax.broadcasted_iota(jnp.int32, (BF16_ROWS, 1), 0) == 0
                    top = jnp.where(first, nyq.astype(BF16), acc_b[:BF16_ROWS])
                    acc_b = jnp.concatenate([top, acc_b[BF16_ROWS:]], axis=0)
                a_ref[r, lanes(i)] = acc_a
                bn_ref[r, lanes(i)] = acc_b

    def inverse(order, finish):
        for tcn in range(per_block):
            r = slice(tcn * seq_chunk, (tcn + 1) * seq_chunk)
            y2 = jnp.dot(c_ref[r, :], a_ref[...], preferred_element_type=F32)
            y2 = y2 + jnp.dot(si_ref[r, :], bn_ref[...], preferred_element_type=F32)
            for i in range(nb):
                chunk = i * per_block + tcn
                rows = slice(chunk * seq_chunk, (chunk + 1) * seq_chunk)
                finish(chunk, rows, y2[:, lanes(i)] + u_ref[rows, :] * hb_ref[order:order + 1, :])

    for chunk in range(nseq):
        put_signal(chunk, _sconv_chunk(v_ref, wv_ref, bv_ref, chunk, seq_chunk, nseq))

    forward(0)

    def finish_z(chunk, rows, y):
        put_signal(chunk, _sconv_chunk(x1_ref, wx1_ref, bx1_ref, chunk, seq_chunk, nseq) * y)

    inverse(0, finish_z)
    forward(1)

    def finish_out(chunk, rows, y):
        y = _sconv_chunk(x2_ref, wx2_ref, bx2_ref, chunk, seq_chunk, nseq) * y
        zg = zg_ref[rows, :]
        o_ref[rows, :] = (y * (zg * jax.nn.sigmoid(zg))).astype(o_ref.dtype)

    inverse(1, finish_out)


def _hyena(proj3, conv_w, conv_b, hyena_bias, cmat, smat_fwd, smat_inv, kr, ki, kn, l, nb, tc=256,
           seq_chunk=512, freq_chunk=512):
    B, L, _ = proj3.shape
    b = L // nb
    nd = 2 * nb - 1
    width = hyena_bias.shape[2]
    nct = width // tc
    sig = lambda part: pl.BlockSpec((None, L, tc), lambda j, bb: (bb, 0, part * nct + j))
    cw = lambda part: pl.BlockSpec((None, 3, tc), lambda j, bb: (l, 0, part * nct + j))
    cb = lambda part: pl.BlockSpec((None, 1, tc), lambda j, bb: (l, 0, part * nct + j))
    kspec = pl.BlockSpec((2, nd, b, tc), lambda j, bb: (0, 0, 0, j))
    kern = functools.partial(_hyena_kernel, nb=nb, seq_chunk=seq_chunk, freq_chunk=freq_chunk)
    return pl.pallas_call(
        kern,
        out_shape=jax.ShapeDtypeStruct((B, L, width), BF16),
        grid=(nct, B),
        in_specs=[sig(0), sig(1), sig(2), sig(3),
                  cw(0), cw(1), cw(2), cb(0), cb(1), cb(2),
                  pl.BlockSpec((None, 2, tc), lambda j, bb: (l, 0, j)),
                  _const_spec((b, b)), _const_spec((b, b)), _const_spec((b, b)),
                  kspec, kspec,
                  pl.BlockSpec((2, nd, 1, tc), lambda j, bb: (0, 0, 0, j))],
        out_specs=pl.BlockSpec((None, L, tc), lambda j, bb: (bb, 0, j)),
        scratch_shapes=[pltpu.VMEM((L, tc), F32), pltpu.VMEM((b, nb * tc), BF16),
                        pltpu.VMEM((b, nb * tc), BF16), pltpu.VMEM((b, nb * tc), BF16)],
        compiler_params=_params(("parallel", "arbitrary")),
        name="hyena_mixer",
    )(proj3, proj3, proj3, proj3, conv_w, conv_w, conv_w, conv_b, conv_b, conv_b,
      hyena_bias, cmat, smat_fwd, smat_inv, kr, ki, kn)


def _rope(x, cos, sin_lo, sin_hi):
    return (x * cos + pltpu.roll(x, HEAD_DIM - ROPE_HALF, 1) * sin_lo
            + pltpu.roll(x, ROPE_HALF, 1) * sin_hi)


def _attn_kernel(sink_ref, q_ref, k_ref, v_ref, zg_ref, cos_ref, slo_ref, shi_ref, o_ref,
                 kb_ref, vt_ref, *, layer):
    L = k_ref.shape[0]
    qb = q_ref.shape[0]
    kw = QSUB + 2 * WINDOW
    gq = GROUP * QSUB
    kvh = pl.program_id(1)
    n = pl.program_id(2)

    @pl.when(n == 0)
    def _():
        kb_ref[...] = _rope(k_ref[...], cos_ref[...], slo_ref[...], shi_ref[...]).astype(BF16)
        vt_ref[...] = v_ref[...].T.astype(BF16)

    log2e = math.log2(math.e)
    lane_head = lax.broadcasted_iota(jnp.int32, (1, gq), 1) // QSUB
    sk2 = jnp.zeros((1, gq), F32)
    for g in range(GROUP):
        sk2 = jnp.where(lane_head == g, sink_ref[layer, kvh * GROUP + g] * log2e, sk2)

    for sb in range(qb // QSUB):
        rows = slice(sb * QSUB, (sb + 1) * QSUB)
        q0 = pl.multiple_of(n * qb + sb * QSUB, QSUB)
        start = pl.multiple_of(jnp.clip(q0 - WINDOW, 0, L - kw), WINDOW)
        win = pl.ds(start, kw)
        tab = pl.ds(q0, QSUB)
        cq, slq, shq = cos_ref[tab, :], slo_ref[tab, :], shi_ref[tab, :]
        q = jnp.concatenate(
            [_rope(q_ref[rows, g * HEAD_DIM:(g + 1) * HEAD_DIM], cq, slq, shq).astype(BF16)
             for g in range(GROUP)], axis=0)
        raw = lax.dot_general(kb_ref[win, :], q, (((1,), (1,)), ((), ())),
                              preferred_element_type=F32)
        kpos = start + lax.broadcasted_iota(jnp.int32, (kw, 1), 0)
        qpos = q0 + lax.broadcasted_iota(jnp.int32, (1, QSUB), 1)
        cap = jnp.where(jnp.abs(kpos - qpos) <= WINDOW, jnp.inf, -jnp.inf)
        raw = jnp.minimum(raw, jnp.concatenate([cap] * GROUP, axis=1))
        c = (HEAD_DIM ** -0.5) * log2e
        m2 = jnp.maximum(jnp.max(raw, axis=0, keepdims=True) * c, sk2)
        p = jnp.exp2(raw * c - m2)
        denom = jnp.sum(p, axis=0, keepdims=True) + jnp.exp2(sk2 - m2)
        ot = jnp.dot(vt_ref[:, win], p.astype(BF16), preferred_element_type=F32) / denom
        for g in range(GROUP):
            cols = slice(g * HEAD_DIM, (g + 1) * HEAD_DIM)
            o = ot[:, g * QSUB:(g + 1) * QSUB].T
            zg = zg_ref[rows, cols]
            o_ref[rows, cols] = (o * (zg * jax.nn.sigmoid(zg))).astype(o_ref.dtype)


def _attention(proj3, sink, rope_tabs, col_q, col_k, col_v, col_zg, l, qb=2048):
    B, L, _ = proj3.shape
    gw = GROUP * HEAD_DIM
    cos_t, sin_lo, sin_hi = rope_tabs
    tab = pl.BlockSpec((L, HEAD_DIM), lambda b, h, n: (0, 0))
    return pl.pallas_call(
        functools.partial(_attn_kernel, layer=l),
        out_shape=jax.ShapeDtypeStruct((B, L, N_HEADS * HEAD_DIM), BF16),
        grid=(B, N_KV_HEADS, L // qb),
        in_specs=[pl.BlockSpec(memory_space=pltpu.SMEM),
                  pl.BlockSpec((None, qb, gw), lambda b, h, n: (b, n, col_q // gw + h)),
                  pl.BlockSpec((None, L, HEAD_DIM), lambda b, h, n: (b, 0, col_k // HEAD_DIM + h)),
                  pl.BlockSpec((None, L, HEAD_DIM), lambda b, h, n: (b, 0, col_v // HEAD_DIM + h)),
                  pl.BlockSpec((None, qb, gw), lambda b, h, n: (b, n, col_zg // gw + h)),
                  tab, tab, tab],
        out_specs=pl.BlockSpec((None, qb, gw), lambda b, h, n: (b, n, h)),
        scratch_shapes=[pltpu.VMEM((L, HEAD_DIM), BF16), pltpu.VMEM((HEAD_DIM, L), BF16)],
        compiler_params=_params(("parallel", "parallel", "arbitrary")),
        name="window_attention",
    )(sink, proj3, proj3, proj3, proj3, cos_t, sin_lo, sin_hi)


def _merge_out_kernel(*refs, n_gate_blocks, final):
    x_ref, yh_ref, ya_ref = refs[0:3]
    gh_refs = refs[3:3 + n_gate_blocks]
    ga_refs = refs[3 + n_gate_blocks:3 + 2 * n_gate_blocks]
    who_ref, wao_ref, wout_ref, fg_ref = refs[3 + 2 * n_gate_blocks:7 + 2 * n_gate_blocks]
    o_ref, *hn_ref, m_ref = refs[7 + 2 * n_gate_blocks:]
    gw = gh_refs[0].shape[1]
    yh = yh_ref[...]
    ya = ya_ref[...]
    for c in range(n_gate_blocks):
        cols = slice(c * gw, (c + 1) * gw)
        ph = jnp.dot(yh, who_ref[:, cols], preferred_element_type=F32)
        pa = jnp.dot(ya, wao_ref[:, cols], preferred_element_type=F32)
        merged = jax.nn.sigmoid(gh_refs[c][...]) * ph + jax.nn.sigmoid(ga_refs[c][...]) * pa
        m_ref[:, cols] = merged.astype(BF16)
    out = x_ref[...] + jnp.dot(m_ref[...], wout_ref[...], preferred_element_type=F32)
    ms = jnp.mean(out * out, axis=-1, keepdims=True)
    normed = out * lax.rsqrt(ms + EPS) * fg_ref[...]
    if final:
        o_ref[...] = normed
    else:
        o_ref[...] = out
        hn_ref[0][...] = normed.astype(BF16)


def _merge_out(x2d, proj, yh, ya, who, wao, wout, gain, col_gh, col_ga, l, final, tm=256, gw=512):
    m, d = x2d.shape
    width = yh.shape[1]
    ngb = d // gw
    gspec = lambda col0, c: pl.BlockSpec((tm, gw), lambda i: (i, col0 // gw + c))
    wspec = lambda rows: pl.BlockSpec((None, rows, d), lambda i: (l, 0, 0), pipeline_mode=pl.Buffered(1))
    kern = functools.partial(_merge_out_kernel, n_gate_blocks=ngb, final=final)
    row_spec = pl.BlockSpec((tm, d), lambda i: (i, 0))
    out_shape = jax.ShapeDtypeStruct((m, d), F32)
    out_specs = row_spec
    if not final:
        out_shape = (out_shape, jax.ShapeDtypeStruct((m, d), BF16))
        out_specs = (row_spec, row_spec)
    return pl.pallas_call(
        kern,
        out_shape=out_shape,
        grid=(m // tm,),
        in_specs=[pl.BlockSpec((tm, d), lambda i: (i, 0)),
                  pl.BlockSpec((tm, width), lambda i: (i, 0)),
                  pl.BlockSpec((tm, width), lambda i: (i, 0))]
                 + [gspec(col_gh, c) for c in range(ngb)]
                 + [gspec(col_ga, c) for c in range(ngb)]
                 + [wspec(width), wspec(width), wspec(d),
                    pl.BlockSpec((1, d), lambda i: (0, 0))],
        out_specs=out_specs,
        scratch_shapes=[pltpu.VMEM((tm, d), BF16)],
        compiler_params=_params(("parallel",)),
        name="merge_out",
    )(x2d, yh, ya, *([proj] * (2 * ngb)), who, wao, wout, gain.reshape(1, d))


def _dft_mats(b):
    idx = np.arange(b)
    ang = ((idx[:, None] * idx[None, :]) % (2 * b)) * (np.pi / b)
    cmat, smat = np.cos(ang), np.sin(ang)
    nyq_cos = np.where(idx % 2 == 0, 1.0, -1.0)
    smat_fwd, smat_inv = smat.copy(), smat.copy()
    smat_fwd[0, :] = nyq_cos
    smat_inv[:, 0] = nyq_cos
    return tuple(jnp.asarray(m).astype(BF16) for m in (cmat, smat, smat_fwd, smat_inv))


def _rope_tabs(L):
    inv = ROPE_THETA ** (-np.arange(0, ROPE_DIM, 2) / ROPE_DIM)
    ang = np.arange(L)[:, None] * inv[None, :]
    cos, sin = np.cos(ang), np.sin(ang)
    ones = np.ones((L, HEAD_DIM - ROPE_DIM))
    zeros = np.zeros((L, HEAD_DIM - ROPE_HALF))
    cos_t = np.concatenate([cos, cos, ones], axis=1)
    sin_lo = np.concatenate([-sin, zeros], axis=1)
    sin_hi = np.concatenate([np.zeros((L, ROPE_HALF)), sin, zeros[:, ROPE_HALF:]], axis=1)
    return tuple(jnp.asarray(t).astype(F32) for t in (cos_t, sin_lo, sin_hi))


def _filter_feats(L):
    t = np.linspace(0.0, 1.0, L)[:, None]
    bands = np.linspace(1e-4, FILTER_BANDS - 1, FILTER_BANDS)[None, :]
    ang = (2.0 * math.pi / L) * np.arange(L)[:, None] * bands
    feats = np.concatenate([t, np.cos(ang), -np.sin(ang)], axis=-1)
    feats = np.pad(feats, ((0, 0), (0, FEAT_PAD - feats.shape[1])))
    return jnp.asarray(feats).astype(F32), jnp.asarray(t).astype(F32)


def kernel(x, norm_g, w_in, conv_w, conv_b, filt_w1, filt_b1, filt_w2, filt_b2, filt_w3, filt_b3,
           filt_w4, filt_freq, hyena_bias, attn_sink, w_hyena_out, w_attn_out, w_out, final_norm):
    B, L, D = x.shape
    depth = norm_g.shape[0]
    hw = hyena_bias.shape[2]
    aw = N_HEADS * HEAD_DIM
    kvw = N_KV_HEADS * HEAD_DIM
    sizes = (3 * hw, hw, aw, kvw, kvw, aw, D, D)
    cols = [0]
    for s in sizes:
        cols.append(cols[-1] + s)
    col_zhy, col_q, col_k, col_v, col_zat, col_gh, col_ga = cols[1:8]

    cmat, smat, smat_fwd, smat_inv = _dft_mats(L // NB)
    rope_tabs = _rope_tabs(L)
    feats, tcol = _filter_feats(L)
    deltas = jnp.abs(jnp.linspace(MIN_DECAY, MAX_DECAY, hw, dtype=F32))[None, :]

    w_in_b, who_b, wao_b, wout_b = (w.astype(BF16) for w in (w_in, w_hyena_out, w_attn_out, w_out))
    row = lambda a: a[:, None, :]
    w1p = jnp.pad(filt_w1, ((0, 0), (0, FEAT_PAD - filt_w1.shape[1]), (0, 0)))
    xf = x.reshape(B * L, D)
    h = None
    for l in range(depth):
        proj = _norm_proj(xf, row(norm_g), w_in_b, l) if h is None else _proj(h, w_in_b, l)
        proj3 = proj.reshape(B, L, -1)
        kr, ki, kn = _filters(feats, tcol, w1p, row(filt_b1), filt_w2, row(filt_b2), filt_w3, row(filt_b3),
                              row(filt_freq), filt_w4, deltas, cmat, smat, l, NB)
        y_hy = _hyena(proj3, conv_w, row(conv_b), hyena_bias, cmat, smat_fwd, smat_inv, kr, ki, kn, l, NB)
        y_at = _attention(proj3, attn_sink, rope_tabs, col_q, col_k, col_v, col_zat, l)
        last = l == depth - 1
        res = _merge_out(xf, proj, y_hy.reshape(B * L, hw), y_at.reshape(B * L, aw), who_b, wao_b, wout_b,
                         final_norm if last else norm_g[l + 1], col_gh, col_ga, l, final=last)
        xf, h = (res, None) if last else res
    return xf.reshape(B, L, D)
```
